```python
import jax
import jax.numpy as jnp
from jax import lax
import numpy as np

D_MODEL = 1024
BATCH = 8
SEQ = 2048
DEPTH = 2
DEC_BATCH = 128
DEC_SEQ = 1
PAST_LEN = 2048
PAGE_SIZE = 128

D_POOL = D_MODEL // 2
POOL_WINDOWS = (2, 4, 8, 16)
POOL_GROUP = D_POOL // len(POOL_WINDOWS)
POOL_HIST = max(POOL_WINDOWS) - 1
D_CONV = D_MODEL // 2
CONV_WIDTH = 3
HEAD_DIM = 64
N_HEADS = D_MODEL // HEAD_DIM
N_KV = 4
GROUP = N_HEADS // N_KV
D_ATTN = N_HEADS * HEAD_DIM
D_KV = N_KV * HEAD_DIM
ROT_DIM = HEAD_DIM // 4
ROPE_THETA = 500000.0
CMP_LEN = 32
CMP_STRIDE = 16
SEL_LEN = 64
N_SEL = 8
WINDOW = 512
Q_BLOCK = 64
FORCE_SCORE = 1.0e4
LN_EPS = 1e-5
ALPHA = (2 * DEPTH) ** 0.25
BETA = (8 * DEPTH) ** -0.25
IN_SIZES = (D_POOL, D_POOL,
            D_CONV, D_CONV, D_CONV, D_CONV,
            D_ATTN, D_KV, D_KV, D_KV, D_KV, D_KV, D_KV, 3 * N_HEADS, D_ATTN,
            3 * D_MODEL)
D_IN = sum(IN_SIZES)

kernel_name = 'nsa_pool_conv_gated_hybrid_step'


def layer_norm(x, g, b):
    xf = x.astype(jnp.float32)
    mu = jnp.mean(xf, axis=-1, keepdims=True)
    var = jnp.mean(jnp.square(xf - mu), axis=-1, keepdims=True)
    return ((xf - mu) * lax.rsqrt(var + LN_EPS) * g + b).astype(x.dtype)


def masked_softmax(s, mask):
    s = jnp.where(mask, s.astype(jnp.float32), -jnp.inf)
    m = jnp.max(s, axis=-1, keepdims=True)
    m = jnp.where(jnp.isfinite(m), m, 0.0)
    e = jnp.where(mask, jnp.exp(s - m), 0.0)
    return e / jnp.maximum(jnp.sum(e, axis=-1, keepdims=True), 1e-30)


def partial_rope(x, pos):
    half = ROT_DIM // 2
    inv_freq = 1.0 / (ROPE_THETA ** (np.arange(0, ROT_DIM, 2, dtype=np.float32) / ROT_DIM))
    ang = jnp.asarray(pos, jnp.float32)[:, None] * jnp.asarray(inv_freq, jnp.float32)[None, :]
    cos = jnp.cos(ang)[:, None, :]
    sin = jnp.sin(ang)[:, None, :]
    xr = x[..., :ROT_DIM].astype(jnp.float32)
    x1, x2 = xr[..., :half], xr[..., half:]
    rot = jnp.concatenate([x1 * cos - x2 * sin, x2 * cos + x1 * sin], axis=-1)
    return jnp.concatenate([rot.astype(x.dtype), x[..., ROT_DIM:]], axis=-1)


def pool_mixer(u, hist, pos0, w_grp, scale):
    bn, t, _ = u.shape
    ext = jnp.concatenate([hist, u], axis=1).astype(jnp.float32)
    cs = jnp.cumsum(ext, axis=1)
    cs = jnp.concatenate([jnp.zeros_like(cs[:, :1]), cs], axis=1)
    upto = cs[:, POOL_HIST + 1:]
    pos = pos0 + np.arange(t, dtype=np.int32)
    means = []
    for g, w in enumerate(POOL_WINDOWS):
        ch = slice(g * POOL_GROUP, (g + 1) * POOL_GROUP)
        start = cs[:, POOL_HIST + 1 - w: POOL_HIST + 1 - w + t, ch]
        count = jnp.asarray(np.minimum(w, pos + 1), jnp.float32)[None, :, None]
        means.append((upto[..., ch] - start) / count)
    pooled = (jnp.concatenate(means, axis=-1) - ext[:, POOL_HIST:]).astype(u.dtype)
    pooled = pooled.reshape(bn, t, len(POOL_WINDOWS), POOL_GROUP)
    mixed = jnp.einsum('btgc,gcd->btgd', pooled, w_grp).reshape(bn, t, D_POOL)
    return mixed * scale


def short_conv(h, hist, w):
    t = h.shape[1]
    ext = jnp.concatenate([hist, h], axis=1)
    return sum(ext[:, j:j + t] * w[j] for j in range(CONV_WIDTH))


def compress(k, pe, wc):
    bn, total = k.shape[:2]
    reps = CMP_LEN // CMP_STRIDE
    n_cmp = (total - CMP_LEN) // CMP_STRIDE + 1
    n_chunk = total // CMP_STRIDE
    chunks = k[:, :n_chunk * CMP_STRIDE].reshape(bn, n_chunk, CMP_STRIDE, N_KV, HEAD_DIM)
    pe_r = pe.reshape(reps, CMP_STRIDE, 1, HEAD_DIM)
    w_r = wc.reshape(reps, CMP_STRIDE, HEAD_DIM, HEAD_DIM)
    return sum(jnp.einsum('bcsgh,she->bcge', chunks[:, r:r + n_cmp] + pe_r[r], w_r[r])
               for r in range(reps))


def cover_matrix(n_cmp, n_blk):
    c0 = np.arange(n_cmp)[:, None] * CMP_STRIDE
    s0 = np.arange(n_blk)[None, :] * SEL_LEN
    return ((c0 < s0 + SEL_LEN) & (c0 + CMP_LEN > s0)).astype(np.float32)


def nsa_block(blk, kc, vc, kc_end, ksb, vsb, cover):
    q, qpos, kw, vw, kwpos, gates = blk
    bn, qb = q.shape[:2]
    scale = HEAD_DIM ** -0.5
    qg = q.reshape(bn, qb, N_KV, GROUP, HEAD_DIM)
    s_c = jnp.einsum('bqgrh,bcgh->bqgrc', qg, kc) * scale
    vis_c = kc_end[None, :] <= qpos[:, None]
    p_c = masked_softmax(s_c, vis_c[None, :, None, None, :])
    o_c = jnp.einsum('bqgrc,bcgh->bqgrh', p_c.astype(vc.dtype), vc)
    imp = jnp.einsum('bqgrc,cj->bqgj', p_c, cover)
    n_blk = cover.shape[1]
    n_pick = min(N_SEL, n_blk)
    bidx = jnp.arange(n_blk)
    cur = qpos // SEL_LEN
    eligible = bidx[None, :] * SEL_LEN <= qpos[:, None]
    forced = (bidx[None, :] == 0) | (bidx[None, :] == cur[:, None]) | (bidx[None, :] == cur[:, None] - 1)
    score = jnp.where(forced[None, :, None, :], FORCE_SCORE,
                      jnp.where(eligible[None, :, None, :], imp, -jnp.inf))
    top_val, top_idx = lax.top_k(score, n_pick)
    valid = jnp.isfinite(top_val)
    gidx = jnp.transpose(top_idx, (0, 2, 1, 3)).reshape(bn, N_KV, qb * n_pick)
    b_ix = jnp.arange(bn)[:, None, None]
    g_ix = jnp.arange(N_KV)[None, :, None]
    ks = ksb[b_ix, g_ix, gidx].reshape(bn, N_KV, qb, n_pick * SEL_LEN, HEAD_DIM)
    vs = vsb[b_ix, g_ix, gidx].reshape(bn, N_KV, qb, n_pick * SEL_LEN, HEAD_DIM)
    kpos = (top_idx[..., None] * SEL_LEN + jnp.arange(SEL_LEN)).reshape(bn, qb, N_KV, n_pick * SEL_LEN)
    m_s = jnp.repeat(valid, SEL_LEN, axis=-1) & (kpos <= qpos[None, :, None, None])
    s_s = jnp.einsum('bqgrh,bgqkh->bqgrk', qg, ks) * scale
    p_s = masked_softmax(s_s, m_s[:, :, :, None, :])
    o_s = jnp.einsum('bqgrk,bgqkh->bqgrh', p_s.astype(vs.dtype), vs)
    m_w = (kwpos[None, :] <= qpos[:, None]) & (qpos[:, None] - kwpos[None, :] < WINDOW)
    s_w = jnp.einsum('bqgrh,bkgh->bqgrk', qg, kw) * scale
    p_w = masked_softmax(s_w, m_w[None, :, None, None, :])
    o_w = jnp.einsum('bqgrk,bkgh->bqgrh', p_w.astype(vw.dtype), vw)
    g = gates.reshape(bn, qb, N_KV, GROUP, 3)
    o = g[..., 0:1] * o_c + g[..., 1:2] * o_s + g[..., 2:3] * o_w
    return o.reshape(bn, qb, D_ATTN)


def nsa_mixer(q, kc_new, vc_new, ks_new, vs_new, kw_new, vw_new, gates, past, pos0, cmp_pe, cmp_w):
    kc_past, vc_past, ks_past, vs_past, kw_buf, vw_buf = past
    bn, t = q.shape[:2]
    total = pos0 + t
    qpos = pos0 + np.arange(t, dtype=np.int32)
    q = partial_rope(q, qpos)
    ks_new = partial_rope(ks_new, qpos)
    kw_new = partial_rope(kw_new, qpos)
    k_cmp_all = jnp.concatenate([kc_past, kc_new], axis=1)
    v_cmp_all = jnp.concatenate([vc_past, vc_new], axis=1)
    k_sel_all = jnp.concatenate([ks_past, ks_new], axis=1)
    v_sel_all = jnp.concatenate([vs_past, vs_new], axis=1)
    kc = compress(k_cmp_all, cmp_pe[0], cmp_w[0])
    vc = compress(v_cmp_all, cmp_pe[1], cmp_w[1])
    n_cmp = kc.shape[1]
    kc_end = np.arange(n_cmp, dtype=np.int32) * CMP_STRIDE + CMP_LEN - 1
    kc = partial_rope(kc, kc_end)
    n_blk = -(-total // SEL_LEN)
    pad = n_blk * SEL_LEN - total

    def to_blocks(a):
        a = jnp.pad(a, ((0, 0), (0, pad), (0, 0), (0, 0)))
        return a.reshape(bn, n_blk, SEL_LEN, N_KV, HEAD_DIM).transpose(0, 3, 1, 2, 4)

    ksb, vsb = to_blocks(k_sel_all), to_blocks(v_sel_all)
    cover = jnp.asarray(cover_matrix(n_cmp, n_blk))
    wb = kw_buf.shape[1]
    zpad = jnp.zeros((bn, WINDOW, N_KV, HEAD_DIM), kw_new.dtype)
    kw_all = jnp.concatenate([zpad, kw_buf, kw_new], axis=1)
    vw_all = jnp.concatenate([zpad.astype(vw_new.dtype), vw_buf, vw_new], axis=1)
    kw_pos = np.concatenate([np.full((WINDOW,), -WINDOW, np.int32),
                             pos0 - wb + np.arange(wb + t, dtype=np.int32)])
    qb = Q_BLOCK if t % Q_BLOCK == 0 else t
    nqb = t // qb
    band = np.arange(nqb)[:, None] * qb + wb + np.arange(WINDOW + qb)[None, :]

    def q_blocks(a):
        return jnp.moveaxis(a.reshape(bn, nqb, qb, *a.shape[2:]), 1, 0)

    xs = (q_blocks(q), jnp.asarray(qpos.reshape(nqb, qb)),
          jnp.moveaxis(kw_all[:, band], 1, 0), jnp.moveaxis(vw_all[:, band], 1, 0),
          jnp.asarray(kw_pos[band]), q_blocks(gates))
    kc_end_j = jnp.asarray(kc_end)
    o = lax.map(lambda blk: nsa_block(blk, kc, vc, kc_end_j, ksb, vsb, cover), xs)
    o = jnp.moveaxis(o, 0, 1).reshape(bn, t, D_ATTN)
    keep = min(WINDOW, total)
    new_state = (kc_new, vc_new, ks_new, vs_new, kw_all[:, -keep:], vw_all[:, -keep:])
    return o, new_state


def trunk_layer(x, pos0, hist, w_in, pool_w, pool_scale, conv_w, cmp_pe, cmp_w,
                proj_a, proj_b, proj_c, w_out, ln_g, ln_b):
    kc_past, vc_past, ks_past, vs_past, kw_buf, vw_buf, pool_hist, conv_hist = hist
    bn, t, _ = x.shape
    h = jnp.einsum('btd,de->bte', x, w_in)
    offs = np.cumsum(IN_SIZES)[:-1].tolist()
    (pu, pz, cb, cc, cx, cz, q, kc, vc, ks, vs, kw, vw, ng, az, mg) = jnp.split(h, offs, axis=-1)
    y_a = pool_mixer(pu, pool_hist, pos0, pool_w, pool_scale) * jax.nn.silu(pz)
    hc = cc * cx
    y_b = cb * short_conv(hc, conv_hist, conv_w) * jax.nn.silu(cz)
    kvh = lambda a: a.reshape(bn, t, N_KV, HEAD_DIM)
    gates = jax.nn.sigmoid(ng).reshape(bn, t, N_HEADS, 3)
    y_c, nsa_state = nsa_mixer(q.reshape(bn, t, N_HEADS, HEAD_DIM), kvh(kc), kvh(vc), kvh(ks), kvh(vs),
                               kvh(kw), kvh(vw), gates,
                               (kc_past, vc_past, ks_past, vs_past, kw_buf, vw_buf), pos0, cmp_pe, cmp_w)
    y_c = y_c * jax.nn.silu(az)
    g_a, g_b, g_c = jnp.split(jax.nn.sigmoid(mg), 3, axis=-1)
    merged = (g_a * jnp.einsum('btc,cd->btd', y_a, proj_a)
              + g_b * jnp.einsum('btc,cd->btd', y_b, proj_b)
              + g_c * jnp.einsum('btc,cd->btd', y_c, proj_c))
    y = jnp.einsum('btd,de->bte', merged, w_out)
    out = layer_norm(ALPHA * x + y, ln_g, ln_b)
    pool_new = jnp.concatenate([pool_hist, pu], axis=1)[:, -POOL_HIST:]
    conv_new = jnp.concatenate([conv_hist, hc], axis=1)[:, -(CONV_WIDTH - 1):]
    return out, nsa_state + (pool_new, conv_new)


def gather_pages(pool, page_table, past_len):
    return pool[page_table].reshape(page_table.shape[0], past_len, N_KV, HEAD_DIM)


def setup_inputs(seed: int = 0) -> dict:
    key = jax.random.key(seed)
    k = jax.random.split(key, 24)
    n_pages = PAST_LEN // PAGE_SIZE
    n_used = DEC_BATCH * n_pages
    n_pool = n_used + (n_used + 3) // 4
    win_rows = min(WINDOW, PAST_LEN)
    nrm = lambda kk, shape, s=1.0: s * jax.random.normal(kk, shape, jnp.float32)
    cache_shape = (DEPTH, n_pool, PAGE_SIZE, N_KV, HEAD_DIM)
    win_shape = (DEPTH, DEC_BATCH, win_rows, N_KV, HEAD_DIM)
    page_table = jax.random.permutation(k[6], n_pool)[:n_used].reshape(DEC_BATCH, n_pages).astype(jnp.int32)
    return {
        'x_prompt': nrm(k[0], (BATCH, SEQ, D_MODEL)),
        'x_sample': nrm(k[1], (DEC_BATCH, DEC_SEQ, D_MODEL)),
        'cache_k_cmp': nrm(k[2], cache_shape),
        'cache_v_cmp': nrm(k[3], cache_shape),
        'cache_k_sel': nrm(k[4], cache_shape),
        'cache_v_sel': nrm(k[5], cache_shape),
        'page_table': page_table,
        'state_k_win': nrm(k[7], win_shape),
        'state_v_win': nrm(k[8], win_shape),
        'state_pool': nrm(k[9], (DEPTH, DEC_BATCH, POOL_HIST, D_POOL)),
        'state_conv': nrm(k[10], (DEPTH, DEC_BATCH, CONV_WIDTH - 1, D_CONV)),
        'w_in': nrm(k[11], (DEPTH, D_MODEL, D_IN), D_MODEL ** -0.5),
        'pool_w': nrm(k[12], (DEPTH, len(POOL_WINDOWS), POOL_GROUP, POOL_GROUP), POOL_GROUP ** -0.5),
        'pool_scale': 1.0 + nrm(k[13], (DEPTH, D_POOL), 0.1),
        'conv_w': nrm(k[14], (DEPTH, CONV_WIDTH, D_CONV), CONV_WIDTH ** -0.5),
        'cmp_pe': nrm(k[15], (DEPTH, 2, CMP_LEN, HEAD_DIM), 0.1),
        'cmp_w': nrm(k[16], (DEPTH, 2, CMP_LEN * HEAD_DIM, HEAD_DIM), (CMP_LEN * HEAD_DIM) ** -0.5),
        'proj_a': nrm(k[17], (DEPTH, D_POOL, D_MODEL), BETA * D_POOL ** -0.5),
        'proj_b': nrm(k[18], (DEPTH, D_CONV, D_MODEL), BETA * D_CONV ** -0.5),
        'proj_c': nrm(k[19], (DEPTH, D_ATTN, D_MODEL), BETA * D_ATTN ** -0.5),
        'w_out': nrm(k[20], (DEPTH, D_MODEL, D_MODEL), BETA * D_MODEL ** -0.5),
        'ln_g': 1.0 + nrm(k[21], (DEPTH, D_MODEL), 0.02),
        'ln_b': nrm(k[22], (DEPTH, D_MODEL), 0.02),
    }


def reference(x_prompt, x_sample, cache_k_cmp, cache_v_cmp, cache_k_sel, cache_v_sel, page_table,
              state_k_win, state_v_win, state_pool, state_conv, w_in, pool_w, pool_scale, conv_w,
              cmp_pe, cmp_w, proj_a, proj_b, proj_c, w_out, ln_g, ln_b):
    bp = x_prompt.shape[0]
    past_len = page_table.shape[1] * cache_k_cmp.shape[2]
    dt = x_prompt.dtype
    empty_kv = jnp.zeros((bp, 0, N_KV, HEAD_DIM), dt)
    hist_prompt = (empty_kv, empty_kv, empty_kv, empty_kv, empty_kv, empty_kv,
                   jnp.zeros((bp, POOL_HIST, D_POOL), dt), jnp.zeros((bp, CONV_WIDTH - 1, D_CONV), dt))
    y_p, y_s = x_prompt, x_sample
    st_p, st_s = [], []
    for l in range(DEPTH):
        weights = (w_in[l], pool_w[l], pool_scale[l], conv_w[l], cmp_pe[l], cmp_w[l],
                   proj_a[l], proj_b[l], proj_c[l], w_out[l], ln_g[l], ln_b[l])
        y_p, new_p = trunk_layer(y_p, 0, hist_prompt, *weights)
        hist_sample = (gather_pages(cache_k_cmp[l], page_table, past_len),
                       gather_pages(cache_v_cmp[l], page_table, past_len),
                       gather_pages(cache_k_sel[l], page_table, past_len),
                       gather_pages(cache_v_sel[l], page_table, past_len),
                       state_k_win[l], state_v_win[l], state_pool[l], state_conv[l])
        y_s, new_s = trunk_layer(y_s, past_len, hist_sample, *weights)
        st_p.append(new_p)
        st_s.append(new_s)

    def stacked(states, i):
        return jnp.stack([s[i] for s in states], axis=0)

    p_k_cmp, p_v_cmp, p_k_sel, p_v_sel = stacked(st_p, 0), stacked(st_p, 1), stacked(st_p, 2), stacked(st_p, 3)
    p_k_win, p_v_win, p_pool, p_conv = stacked(st_p, 4), stacked(st_p, 5), stacked(st_p, 6), stacked(st_p, 7)
    s_k_cmp, s_v_cmp, s_k_sel, s_v_sel = stacked(st_s, 0), stacked(st_s, 1), stacked(st_s, 2), stacked(st_s, 3)
    s_k_win, s_v_win, s_pool, s_conv = stacked(st_s, 4), stacked(st_s, 5), stacked(st_s, 6), stacked(st_s, 7)
    return (y_p, y_s, p_k_cmp, p_v_cmp, p_k_sel, p_v_sel, p_k_win, p_v_win, p_pool, p_conv,
            s_k_cmp, s_v_cmp, s_k_sel, s_v_sel, s_k_win, s_v_win, s_pool, s_conv)
```

```python
import functools

import numpy as np
import jax
import jax.numpy as jnp
from jax import lax
from jax.experimental import pallas as pl
from jax.experimental.pallas import tpu as pltpu

D_MODEL = 1024
DEPTH = 2
D_POOL = 512
POOL_WINDOWS = (2, 4, 8, 16)
POOL_GROUP = D_POOL // len(POOL_WINDOWS)
POOL_HIST = max(POOL_WINDOWS) - 1
D_CONV = 512
CONV_WIDTH = 3
HEAD_DIM = 64
N_HEADS = 16
N_KV = 4
GROUP = N_HEADS // N_KV
D_ATTN = N_HEADS * HEAD_DIM
D_KV = N_KV * HEAD_DIM
ROT_DIM = HEAD_DIM // 4
ROPE_THETA = 500000.0
CMP_LEN = 32
CMP_STRIDE = 16
SEL_LEN = 64
N_SEL = 8
WINDOW = 512
FORCE_SCORE = 1.0e4
LN_EPS = 1e-5
ALPHA = (2 * DEPTH) ** 0.25
SCALE = HEAD_DIM ** -0.5

C_PU, C_PZ, C_CB, C_CC, C_CX, C_CZ = 0, 512, 1024, 1536, 2048, 2560
C_Q = 3072
C_AZ = 4096
C_MG = 5120
C_KV = 8192
C_NG = 9728
D_H = 9856
NG_PAD = 128

NEG = -1.0e30
LANE = 128
CHUNK = 512
VMEM_LIMIT = 48 * 1024 * 1024

BF16 = jnp.bfloat16
F32 = jnp.float32


def _cparams(sem):
    return pltpu.CompilerParams(dimension_semantics=sem, vmem_limit_bytes=VMEM_LIMIT)


def _dot(a, b):
    return jnp.dot(a, b, preferred_element_type=F32)


def _dot_nt(a, b):
    return lax.dot_general(a, b, (((1,), (1,)), ((), ())), preferred_element_type=F32)


def _silu(x):
    return x * jax.nn.sigmoid(x)


def _perm_w_in(w):
    ab = w[:, 0:3072]
    q = w[:, 3072:4096]
    kv = w[:, 4096:5632]
    ng = w[:, 5632:5680]
    az = w[:, 5680:6704]
    mg = w[:, 6704:9776]
    idx = np.full((NG_PAD,), 48, np.int32)
    for g in range(N_KV):
        for br in range(3):
            for r in range(GROUP):
                idx[g * 16 + br * 4 + r] = (GROUP * g + r) * 3 + br
    ng_ext = jnp.concatenate([ng, jnp.zeros((w.shape[0], 1), w.dtype)], axis=1)
    ng_p = jnp.take(ng_ext, jnp.asarray(idx), axis=1)
    return jnp.concatenate([ab, q, az, mg, kv, ng_p], axis=1).astype(BF16)


def _rope_tables(pos, width):
    pos = jnp.asarray(pos, F32)
    p = pos.shape[0]
    inv_freq = 1.0 / (ROPE_THETA ** (np.arange(0, ROT_DIM, 2, dtype=np.float32) / ROT_DIM))
    ang = pos[:, None] * jnp.asarray(inv_freq, F32)[None, :]
    cos = jnp.cos(ang)
    sin = jnp.sin(ang)
    z8 = jnp.zeros((p, 8), F32)
    z48 = jnp.zeros((p, 48), F32)
    c64 = jnp.concatenate([cos, cos, jnp.ones((p, 48), F32)], axis=1)
    s1 = jnp.concatenate([-sin, z8, z48], axis=1)
    s2 = jnp.concatenate([z8, sin, z48], axis=1)
    rep = width // HEAD_DIM
    return tuple(jnp.tile(t, (1, rep)) for t in (c64, s1, s2))


def _rope(x, c, s1, s2):
    w = x.shape[-1]
    up = pltpu.roll(x, w - 8, 1)
    dn = pltpu.roll(x, 8, 1)
    return x * c + up * s1 + dn * s2


def _tile_lanes(t, rep):
    return t if rep == 1 else jnp.concatenate([t] * rep, axis=1)


def _cmp_weights(cmp_w_l, cmp_pe_l):
    eye = jnp.eye(N_KV, dtype=F32)
    w = cmp_w_l.reshape(2, 2, CMP_STRIDE, HEAD_DIM, HEAD_DIM)
    wb = jnp.einsum('khsde,gf->khsgdfe', w, eye).reshape(2, 2, CMP_STRIDE * D_KV, D_KV)
    pe = cmp_pe_l.reshape(2, 2, CMP_STRIDE, 1, HEAD_DIM)
    pe = jnp.broadcast_to(pe, (2, 2, CMP_STRIDE, N_KV, HEAD_DIM)).reshape(2, 2, 1, CMP_STRIDE * D_KV)
    return wb.astype(BF16), pe


def _cover_t(n_cmp_pad, n_blk_pad, n_cmp, n_blk):
    c0 = np.arange(n_cmp_pad)[None, :] * CMP_STRIDE
    s0 = np.arange(n_blk_pad)[:, None] * SEL_LEN
    m = (c0 < s0 + SEL_LEN) & (c0 + CMP_LEN > s0)
    m &= (np.arange(n_cmp_pad)[None, :] < n_cmp) & (np.arange(n_blk_pad)[:, None] < n_blk)
    return m.astype(np.float32)


def _mm_kernel(x_ref, w_ref, o_ref):
    o_ref[...] = _dot(x_ref[...], w_ref[...])


def _inproj(x_bf, w_bf, tm):
    m = x_bf.shape[0]
    tn = 896
    return pl.pallas_call(
        _mm_kernel,
        grid=(D_H // tn, m // tm),
        in_specs=[pl.BlockSpec((tm, D_MODEL), lambda j, i: (i, 0)),
                  pl.BlockSpec((D_MODEL, tn), lambda j, i: (0, j))],
        out_specs=pl.BlockSpec((tm, tn), lambda j, i: (i, j)),
        out_shape=jax.ShapeDtypeStruct((m, D_H), F32),
        compiler_params=_cparams(("arbitrary", "arbitrary")),
        name="inproj",
    )(x_bf, w_bf)


def _prep_p_kernel(q_ref, kcv_ref, ksv_ref, kwv_ref, ng_ref, c_ref, s1_ref, s2_ref,
                   kvst_ref, qt_ref, ksg_ref, kwg_ref, vst_ref, vwt_ref, gt_ref):
    c, s1, s2 = c_ref[...], s1_ref[...], s2_ref[...]
    q = _rope(q_ref[...], _tile_lanes(c, 8), _tile_lanes(s1, 8), _tile_lanes(s2, 8)) * SCALE
    qt_ref[0] = q.T.astype(BF16)
    c2, s12, s22 = c_ref[:, :], s1_ref[:, :], s2_ref[:, :]
    c2, s12, s22 = _tile_lanes(c2, 2), _tile_lanes(s12, 2), _tile_lanes(s22, 2)
    kcv = kcv_ref[...]
    ksv = ksv_ref[...]
    kwv = kwv_ref[...]
    ks = _rope(ksv[:, :D_KV], c2, s12, s22)
    kw = _rope(kwv[:, :D_KV], c2, s12, s22)
    vs = ksv[:, D_KV:]
    vw = kwv[:, D_KV:]
    kvst_ref[:, 0:512] = kcv
    kvst_ref[:, 512:768] = ks
    kvst_ref[:, 768:1024] = vs
    kvst_ref[:, 1024:1280] = kw
    kvst_ref[:, 1280:1536] = vw
    for g in range(N_KV):
        ksg_ref[0, g] = ks[:, g * HEAD_DIM:(g + 1) * HEAD_DIM].astype(BF16)
        kwg_ref[0, g] = kw[:, g * HEAD_DIM:(g + 1) * HEAD_DIM].astype(BF16)
    vst_ref[0, 0] = vs.T.astype(BF16)
    vwt_ref[0, 0] = vw.T.astype(BF16)
    gt_ref[0] = jax.nn.sigmoid(ng_ref[...]).T[:64, :]


def _prep_prompt(h, tabs, b, t):
    tt = CHUNK
    nt = t // tt
    row = lambda bi, ti: bi * nt + ti
    c, s1, s2 = tabs
    tab_spec = pl.BlockSpec((tt, LANE), lambda bi, ti: (ti, 0))
    outs = pl.pallas_call(
        _prep_p_kernel,
        grid=(b, nt),
        in_specs=[pl.BlockSpec((tt, 1024), lambda bi, ti: (row(bi, ti), C_Q // 1024)),
                  pl.BlockSpec((tt, 512), lambda bi, ti: (row(bi, ti), C_KV // 512)),
                  pl.BlockSpec((tt, 512), lambda bi, ti: (row(bi, ti), C_KV // 512 + 1)),
                  pl.BlockSpec((tt, 512), lambda bi, ti: (row(bi, ti), C_KV // 512 + 2)),
                  pl.BlockSpec((tt, LANE), lambda bi, ti: (row(bi, ti), C_NG // LANE)),
                  tab_spec, tab_spec, tab_spec],
        out_specs=[pl.BlockSpec((tt, 1536), lambda bi, ti: (row(bi, ti), 0)),
                   pl.BlockSpec((1, 1024, tt), lambda bi, ti: (bi, 0, ti)),
                   pl.BlockSpec((1, N_KV, tt, HEAD_DIM), lambda bi, ti: (bi, 0, ti, 0)),
                   pl.BlockSpec((1, N_KV, tt, HEAD_DIM), lambda bi, ti: (bi, 0, ti, 0)),
                   pl.BlockSpec((1, 1, D_KV, tt), lambda bi, ti: (bi, ti, 0, 0)),
                   pl.BlockSpec((1, 1, D_KV, tt), lambda bi, ti: (bi, ti, 0, 0)),
                   pl.BlockSpec((1, 64, tt), lambda bi, ti: (bi, 0, ti))],
        out_shape=[jax.ShapeDtypeStruct((b * t, 1536), F32),
                   jax.ShapeDtypeStruct((b, 1024, t), BF16),
                   jax.ShapeDtypeStruct((b, N_KV, t, HEAD_DIM), BF16),
                   jax.ShapeDtypeStruct((b, N_KV, t, HEAD_DIM), BF16),
                   jax.ShapeDtypeStruct((b, nt, D_KV, tt), BF16),
                   jax.ShapeDtypeStruct((b, nt, D_KV, tt), BF16),
                   jax.ShapeDtypeStruct((b, 64, t), F32)],
        compiler_params=_cparams(("arbitrary", "arbitrary")),
        name="prep_prompt",
    )(h, h, h, h, h, c, s1, s2)
    return outs


def _compress_math(x16, w_ref, pe_ref, kv):
    top = _dot((x16 + pe_ref[kv, 0]).astype(BF16), w_ref[kv, 0])
    bot = _dot((x16 + pe_ref[kv, 1]).astype(BF16), w_ref[kv, 1])
    n = x16.shape[0]
    bot_up = jnp.concatenate([bot[1:], jnp.zeros((1, D_KV), F32)], axis=0)
    del n
    return top + bot_up


def _compress_p_kernel(k16_ref, v16_ref, w_ref, pe_ref, c_ref, s1_ref, s2_ref, kcg_ref, vct_ref):
    kc = _compress_math(k16_ref[0], w_ref, pe_ref, 0)
    kc = _rope(kc, c_ref[...], s1_ref[...], s2_ref[...])
    vc = _compress_math(v16_ref[0], w_ref, pe_ref, 1)
    for g in range(N_KV):
        kcg_ref[0, g] = kc[:, g * HEAD_DIM:(g + 1) * HEAD_DIM].astype(BF16)
    vct_ref[0] = vc.T.astype(BF16)


def _compress_prompt(k16, v16, wb, pe, ctabs, b):
    nch = k16.shape[1]
    c, s1, s2 = ctabs
    full = lambda shape: pl.BlockSpec(shape, lambda bi: (0,) * len(shape))
    return pl.pallas_call(
        _compress_p_kernel,
        grid=(b,),
        in_specs=[pl.BlockSpec((1, nch, 4096), lambda bi: (bi, 0, 0)),
                  pl.BlockSpec((1, nch, 4096), lambda bi: (bi, 0, 0)),
                  full((2, 2, 4096, D_KV)), full((2, 2, 1, 4096)),
                  full((nch, D_KV)), full((nch, D_KV)), full((nch, D_KV))],
        out_specs=[pl.BlockSpec((1, N_KV, nch, HEAD_DIM), lambda bi: (bi, 0, 0, 0)),
                   pl.BlockSpec((1, D_KV, nch), lambda bi: (bi, 0, 0))],
        out_shape=[jax.ShapeDtypeStruct((b, N_KV, nch, HEAD_DIM), BF16),
                   jax.ShapeDtypeStruct((b, D_KV, nch), BF16)],
        compiler_params=_cparams(("arbitrary",)),
        name="compress_prompt",
    )(k16, v16, wb, pe, c, s1, s2)


def _select_blocks(score, n_blk, axis):
    n = score.shape[axis]
    idx = lax.broadcasted_iota(jnp.int32, score.shape, axis)
    rank = jnp.zeros(score.shape, F32)
    for j in range(n_blk):
        if axis == 0:
            sj = score[j:j + 1, :]
        else:
            sj = score[:, j:j + 1]
        beats = (sj > score) | ((sj == score) & (idx > j))
        rank = rank + jnp.where(beats, 1.0, 0.0)
    del n
    keep = (rank < float(N_SEL)) & (score > -jnp.inf) & (idx < n_blk)
    return jnp.where(keep, 1.0, 0.0)


def _attn_p_kernel(qt_ref, kcg_ref, vct_ref, ksg_ref, vst_ref, kwg_ref, vwt_ref, gt_ref, cov_ref,
                   o_ref, sel_ref, m_ref, l_ref, acc_ref, *, tq, n_cmp, n_blk):
    qi = pl.program_id(2)
    t0 = qi * tq
    n = GROUP * tq
    qt = qt_ref[0]
    qcat = jnp.concatenate([qt[r * HEAD_DIM:(r + 1) * HEAD_DIM, :] for r in range(GROUP)], axis=1)
    qpos = t0 + (lax.broadcasted_iota(jnp.int32, (1, n), 1) & (tq - 1))

    ncp = kcg_ref.shape[2]
    s_c = _dot(kcg_ref[0, 0], qcat)
    cidx = lax.broadcasted_iota(jnp.int32, (ncp, n), 0)
    vis = (cidx * CMP_STRIDE + (CMP_LEN - 1) <= qpos) & (cidx < n_cmp)
    m_c = jnp.max(jnp.where(vis, s_c, NEG), axis=0, keepdims=True)
    e_c = jnp.where(vis, jnp.exp(s_c - m_c), 0.0)
    p_c = e_c / jnp.maximum(jnp.sum(e_c, axis=0, keepdims=True), 1e-30)
    p_cb = p_c.astype(BF16)
    o_cmp = _dot(vct_ref[0], p_cb)
    imp4 = _dot(cov_ref[...], p_cb)
    imp = imp4[:, 0:tq]
    for r in range(1, GROUP):
        imp = imp + imp4[:, r * tq:(r + 1) * tq]

    nb = imp.shape[0]
    bidx = lax.broadcasted_iota(jnp.int32, (nb, tq), 0)
    qp = t0 + lax.broadcasted_iota(jnp.int32, (nb, tq), 1)
    cur = qp // SEL_LEN
    eligible = bidx * SEL_LEN <= qp
    forced = (bidx == 0) | (bidx == cur) | (bidx == cur - 1)
    score = jnp.where(forced, FORCE_SCORE, jnp.where(eligible, imp, -jnp.inf))
    sel_ref[...] = _select_blocks(score, n_blk, 0)

    def run(k_ref, v_ref, lo, hi, selected):
        m_ref[...] = jnp.full((1, n), NEG, F32)
        l_ref[...] = jnp.zeros((1, n), F32)
        acc_ref[...] = jnp.zeros((HEAD_DIM, n), F32)

        def body(c, carry):
            base = pl.multiple_of(c * CHUNK, CHUNK)
            s = _dot(k_ref[0, 0, pl.ds(base, CHUNK), :], qcat)
            kpos = base + lax.broadcasted_iota(jnp.int32, (CHUNK, n), 0)
            ok = kpos <= qpos
            if selected:
                selc = sel_ref[pl.ds(pl.multiple_of(c * (CHUNK // SEL_LEN), CHUNK // SEL_LEN), CHUNK // SEL_LEN), :]
                mk = jnp.concatenate(
                    [jnp.broadcast_to(selc[j:j + 1, :], (SEL_LEN, tq)) for j in range(CHUNK // SEL_LEN)], axis=0)
                mk = jnp.concatenate([mk] * GROUP, axis=1)
                ok = ok & (mk > 0.5)
            else:
                ok = ok & (qpos - kpos < WINDOW)
            m_old = m_ref[...]
            m_new = jnp.maximum(m_old, jnp.max(jnp.where(ok, s, NEG), axis=0, keepdims=True))
            alpha = jnp.exp(m_old - m_new)
            e = jnp.where(ok, jnp.exp(s - m_new), 0.0)
            l_ref[...] = alpha * l_ref[...] + jnp.sum(e, axis=0, keepdims=True)
            acc_ref[...] = alpha * acc_ref[...] + _dot(v_ref[0, c], e.astype(BF16))
            m_ref[...] = m_new
            return carry

        lax.fori_loop(lo, hi, body, 0)
        return acc_ref[...] / jnp.maximum(l_ref[...], 1e-30)

    last = (t0 + tq - 1) // CHUNK
    o_sel = run(ksg_ref, vst_ref, 0, last + 1, True)
    first = jnp.maximum(t0 - (WINDOW - 1), 0) // CHUNK
    o_win = run(kwg_ref, vwt_ref, first, last + 1, False)

    gt = gt_ref[0]
    outs = []
    for r in range(GROUP):
        sl = slice(r * tq, (r + 1) * tq)
        o_r = (gt[r:r + 1, :] * o_cmp[:, sl] + gt[4 + r:5 + r, :] * o_sel[:, sl]
               + gt[8 + r:9 + r, :] * o_win[:, sl])
        outs.append(o_r.T)
    o_ref[...] = jnp.concatenate(outs, axis=1)


def _attn_prompt(qt, kcg, vct, ksg, vst, kwg, vwt, gt, cov, b, t, n_cmp, n_blk):
    tq = 256
    nq = t // tq
    nch = t // CHUNK
    ncp = kcg.shape[2]
    kern = functools.partial(_attn_p_kernel, tq=tq, n_cmp=n_cmp, n_blk=n_blk)
    n = GROUP * tq
    return pl.pallas_call(
        kern,
        grid=(b, N_KV, nq),
        in_specs=[pl.BlockSpec((1, D_KV, tq), lambda bi, g, qi: (bi, g, qi)),
                  pl.BlockSpec((1, 1, ncp, HEAD_DIM), lambda bi, g, qi: (bi, g, 0, 0)),
                  pl.BlockSpec((1, HEAD_DIM, ncp), lambda bi, g, qi: (bi, g, 0)),
                  pl.BlockSpec((1, 1, t, HEAD_DIM), lambda bi, g, qi: (bi, g, 0, 0)),
                  pl.BlockSpec((1, nch, HEAD_DIM, CHUNK), lambda bi, g, qi: (bi, 0, g, 0)),
                  pl.BlockSpec((1, 1, t, HEAD_DIM), lambda bi, g, qi: (bi, g, 0, 0)),
                  pl.BlockSpec((1, nch, HEAD_DIM, CHUNK), lambda bi, g, qi: (bi, 0, g, 0)),
                  pl.BlockSpec((1, 16, tq), lambda bi, g, qi: (bi, g, qi)),
                  pl.BlockSpec(cov.shape, lambda bi, g, qi: (0, 0))],
        out_specs=pl.BlockSpec((tq, D_KV), lambda bi, g, qi: (bi * nq + qi, g)),
        out_shape=jax.ShapeDtypeStruct((b * t, D_ATTN), F32),
        scratch_shapes=[pltpu.VMEM((cov.shape[0], tq), F32),
                        pltpu.VMEM((1, n), F32), pltpu.VMEM((1, n), F32),
                        pltpu.VMEM((HEAD_DIM, n), F32)],
        compiler_params=_cparams(("arbitrary", "arbitrary", "arbitrary")),
        name="attn_prompt",
    )(qt, kcg, vct, ksg, vst, kwg, vwt, gt, cov)


def _pool_group_mix(pooled, pw_ref, scale, pz):
    mixed = jnp.concatenate(
        [_dot(pooled[:, g * POOL_GROUP:(g + 1) * POOL_GROUP].astype(BF16), pw_ref[g])
         for g in range(len(POOL_WINDOWS))], axis=1)
    return (mixed * scale) * _silu(pz)


def _ab_p_kernel(pu_ref, pz_ref, cb_ref, cc_ref, cx_ref, cz_ref, pup_ref, ccp_ref, cxp_ref,
                 pw_ref, ps_ref, cw_ref, ya_ref, yb_ref, hct_ref, *, tm, tiles_per_seq):
    i = pl.program_id(0)
    ti = i % tiles_per_seq
    valid = jnp.where(ti > 0, 1.0, 0.0)
    pu = pu_ref[...]
    ext = jnp.concatenate([pup_ref[...] * valid, pu], axis=0)
    pos = ti * tm + lax.broadcasted_iota(jnp.int32, (tm, POOL_GROUP), 0)
    acc = pu
    means = []
    k = 1
    for gi, w in enumerate(POOL_WINDOWS):
        ch = slice(gi * POOL_GROUP, (gi + 1) * POOL_GROUP)
        while k < w:
            acc = acc + ext[16 - k:16 - k + tm, :]
            k += 1
        count = jnp.minimum(w, pos + 1).astype(F32)
        means.append(acc[:, ch] / count)
    pooled = jnp.concatenate(means, axis=1) - pu
    ya_ref[...] = _pool_group_mix(pooled, pw_ref, ps_ref[...], pz_ref[...]).astype(BF16)

    hc = cc_ref[...] * cx_ref[...]
    hprev = (ccp_ref[...] * cxp_ref[...]) * valid
    hext = jnp.concatenate([hprev, hc], axis=0)
    cw = cw_ref[...]
    conv = hext[14:14 + tm, :] * cw[0:1, :]
    conv = conv + hext[15:15 + tm, :] * cw[1:2, :]
    conv = conv + hc * cw[2:3, :]
    yb_ref[...] = ((cb_ref[...] * conv) * _silu(cz_ref[...])).astype(BF16)
    hct_ref[0] = hc[tm - 8:tm, :]


def _ab_prompt(h, pw_bf, pscale, cw, b, t):
    tm = 512
    m = b * t
    tps = t // tm
    colb = lambda c: (lambda i: (i, c // 512))
    prev = lambda c: (lambda i: (jnp.maximum(i * (tm // 16) - 1, 0), c // 512))
    kern = functools.partial(_ab_p_kernel, tm=tm, tiles_per_seq=tps)
    full = lambda shape: pl.BlockSpec(shape, lambda i: (0,) * len(shape))
    return pl.pallas_call(
        kern,
        grid=(m // tm,),
        in_specs=[pl.BlockSpec((tm, 512), colb(C_PU)), pl.BlockSpec((tm, 512), colb(C_PZ)),
                  pl.BlockSpec((tm, 512), colb(C_CB)), pl.BlockSpec((tm, 512), colb(C_CC)),
                  pl.BlockSpec((tm, 512), colb(C_CX)), pl.BlockSpec((tm, 512), colb(C_CZ)),
                  pl.BlockSpec((16, 512), prev(C_PU)), pl.BlockSpec((16, 512), prev(C_CC)),
                  pl.BlockSpec((16, 512), prev(C_CX)),
                  full((len(POOL_WINDOWS), POOL_GROUP, POOL_GROUP)), full((1, D_POOL)), full((CONV_WIDTH, D_CONV))],
        out_specs=[pl.BlockSpec((tm, D_POOL), lambda i: (i, 0)),
                   pl.BlockSpec((tm, D_CONV), lambda i: (i, 0)),
                   pl.BlockSpec((1, 8, D_CONV), lambda i: (i // tps, 0, 0))],
        out_shape=[jax.ShapeDtypeStruct((m, D_POOL), BF16),
                   jax.ShapeDtypeStruct((m, D_CONV), BF16),
                   jax.ShapeDtypeStruct((b, 8, D_CONV), F32)],
        compiler_params=_cparams(("arbitrary",)),
        name="ab_prompt",
    )(h, h, h, h, h, h, h, h, h, pw_bf, pscale, cw)


def _tail_kernel(ya_ref, yb_ref, yc_ref, az_ref, ga_ref, gb_ref, gc_ref, x_ref,
                 pa_ref, pb_ref, pc_ref, wo_ref, lg_ref, lb_ref, o_ref, obf_ref):
    yc = (yc_ref[...] * _silu(az_ref[...])).astype(BF16)
    merged = (jax.nn.sigmoid(ga_ref[...]) * _dot(ya_ref[...], pa_ref[...])
              + jax.nn.sigmoid(gb_ref[...]) * _dot(yb_ref[...], pb_ref[...])
              + jax.nn.sigmoid(gc_ref[...]) * _dot(yc, pc_ref[...]))
    y = _dot(merged.astype(BF16), wo_ref[...])
    z = ALPHA * x_ref[...] + y
    mu = jnp.mean(z, axis=-1, keepdims=True)
    var = jnp.mean(jnp.square(z - mu), axis=-1, keepdims=True)
    out = (z - mu) * lax.rsqrt(var + LN_EPS) * lg_ref[...] + lb_ref[...]
    o_ref[...] = out
    obf_ref[...] = out.astype(BF16)


def _tail(ya, yb, yc, h, x, pa, pb, pc, wo, lg, lb, tm):
    m = x.shape[0]
    full = lambda shape: pl.BlockSpec(shape, lambda i: (0,) * len(shape))
    hcol = lambda c: pl.BlockSpec((tm, 1024), lambda i: (i, c // 1024))
    return pl.pallas_call(
        _tail_kernel,
        grid=(m // tm,),
        in_specs=[pl.BlockSpec((tm, D_POOL), lambda i: (i, 0)),
                  pl.BlockSpec((tm, D_CONV), lambda i: (i, 0)),
                  pl.BlockSpec((tm, D_ATTN), lambda i: (i, 0)),
                  hcol(C_AZ), hcol(C_MG), hcol(C_MG + 1024), hcol(C_MG + 2048),
                  pl.BlockSpec((tm, D_MODEL), lambda i: (i, 0)),
                  full((D_POOL, D_MODEL)), full((D_CONV, D_MODEL)), full((D_ATTN, D_MODEL)),
                  full((D_MODEL, D_MODEL)), full((1, D_MODEL)), full((1, D_MODEL))],
        out_specs=[pl.BlockSpec((tm, D_MODEL), lambda i: (i, 0)),
                   pl.BlockSpec((tm, D_MODEL), lambda i: (i, 0))],
        out_shape=[jax.ShapeDtypeStruct((m, D_MODEL), F32),
                   jax.ShapeDtypeStruct((m, D_MODEL), BF16)],
        compiler_params=_cparams(("arbitrary",)),
        name="tail",
    )(ya, yb, yc, h, h, h, h, x, pa, pb, pc, wo, lg, lb)


def _prep_s_kernel(q_ref, kcv_ref, ksv_ref, kwv_ref, ng_ref, c_ref, s1_ref, s2_ref,
                   kvst_ref, qr_ref, g_ref):
    c, s1, s2 = c_ref[...], s1_ref[...], s2_ref[...]
    qr_ref[...] = _rope(q_ref[...], _tile_lanes(c, 8), _tile_lanes(s1, 8), _tile_lanes(s2, 8)) * SCALE
    c2, s12, s22 = _tile_lanes(c, 2), _tile_lanes(s1, 2), _tile_lanes(s2, 2)
    ksv = ksv_ref[...]
    kwv = kwv_ref[...]
    kvst_ref[:, 0:512] = kcv_ref[...]
    kvst_ref[:, 512:768] = _rope(ksv[:, :D_KV], c2, s12, s22)
    kvst_ref[:, 768:1024] = ksv[:, D_KV:]
    kvst_ref[:, 1024:1280] = _rope(kwv[:, :D_KV], c2, s12, s22)
    kvst_ref[:, 1280:1536] = kwv[:, D_KV:]
    g_ref[...] = jax.nn.sigmoid(ng_ref[...])


def _prep_sample(h, tabs):
    n = h.shape[0]
    c, s1, s2 = tabs
    tab = pl.BlockSpec((1, LANE), lambda i: (0, 0))
    return pl.pallas_call(
        _prep_s_kernel,
        grid=(1,),
        in_specs=[pl.BlockSpec((n, 1024), lambda i: (0, C_Q // 1024)),
                  pl.BlockSpec((n, 512), lambda i: (0, C_KV // 512)),
                  pl.BlockSpec((n, 512), lambda i: (0, C_KV // 512 + 1)),
                  pl.BlockSpec((n, 512), lambda i: (0, C_KV // 512 + 2)),
                  pl.BlockSpec((n, LANE), lambda i: (0, C_NG // LANE)),
                  tab, tab, tab],
        out_specs=[pl.BlockSpec((n, 1536), lambda i: (0, 0)),
                   pl.BlockSpec((n, 1024), lambda i: (0, 0)),
                   pl.BlockSpec((n, LANE), lambda i: (0, 0))],
        out_shape=[jax.ShapeDtypeStruct((n, 1536), F32),
                   jax.ShapeDtypeStruct((n, 1024), F32),
                   jax.ShapeDtypeStruct((n, LANE), F32)],
        compiler_params=_cparams(("arbitrary",)),
        name="prep_sample",
    )(h, h, h, h, h, c, s1, s2)


def _ab_s_kernel(pu_ref, pz_ref, cb_ref, cc_ref, cx_ref, cz_ref, ph_ref, ch_ref,
                 pw_ref, ps_ref, cw_ref, ya_ref, yb_ref, hc_ref, *, pos0):
    pu = pu_ref[...]
    acc = pu
    means = []
    k = 1
    for gi, w in enumerate(POOL_WINDOWS):
        ch = slice(gi * POOL_GROUP, (gi + 1) * POOL_GROUP)
        while k < w:
            acc = acc + ph_ref[POOL_HIST - k]
            k += 1
        means.append(acc[:, ch] / float(min(w, pos0 + 1)))
    pooled = jnp.concatenate(means, axis=1) - pu
    ya_ref[...] = _pool_group_mix(pooled, pw_ref, ps_ref[...], pz_ref[...]).astype(BF16)
    hc = cc_ref[...] * cx_ref[...]
    cw = cw_ref[...]
    conv = ch_ref[0] * cw[0:1, :]
    conv = conv + ch_ref[1] * cw[1:2, :]
    conv = conv + hc * cw[2:3, :]
    yb_ref[...] = ((cb_ref[...] * conv) * _silu(cz_ref[...])).astype(BF16)
    hc_ref[...] = hc


def _ab_sample(h, pool_hist_t, conv_hist_t, pw_bf, pscale, cw, pos0):
    n = h.shape[0]
    colb = lambda c: pl.BlockSpec((n, 512), lambda i: (0, c // 512))
    full = lambda shape: pl.BlockSpec(shape, lambda i: (0,) * len(shape))
    kern = functools.partial(_ab_s_kernel, pos0=pos0)
    return pl.pallas_call(
        kern,
        grid=(1,),
        in_specs=[colb(C_PU), colb(C_PZ), colb(C_CB), colb(C_CC), colb(C_CX), colb(C_CZ),
                  full((POOL_HIST, n, D_POOL)), full((CONV_WIDTH - 1, n, D_CONV)),
                  full((len(POOL_WINDOWS), POOL_GROUP, POOL_GROUP)), full((1, D_POOL)), full((CONV_WIDTH, D_CONV))],
        out_specs=[full((n, D_POOL)), full((n, D_CONV)), full((n, D_CONV))],
        out_shape=[jax.ShapeDtypeStruct((n, D_POOL), BF16),
                   jax.ShapeDtypeStruct((n, D_CONV), BF16),
                   jax.ShapeDtypeStruct((n, D_CONV), F32)],
        compiler_params=_cparams(("arbitrary",)),
        name="ab_sample",
    )(h, h, h, h, h, h, pool_hist_t, conv_hist_t, pw_bf, pscale, cw)


def _attn_s_kernel(pt_ref, *refs, n_pages, past_len, n_cmp, n_blk, wb):
    del pt_ref
    np_ = n_pages
    kc_pages = refs[0:np_]
    vc_pages = refs[np_:2 * np_]
    ks_pages = refs[2 * np_:3 * np_]
    vs_pages = refs[3 * np_:4 * np_]
    (kwin_ref, vwin_ref, qz_ref, new_ref, g_ref, w_ref, pe_ref, c_ref, s1_ref, s2_ref,
     cov_ref, exp_ref, hm_ref, o_ref) = refs[4 * np_:]
    qpos = past_len
    qz = qz_ref[0]
    qzb = qz.astype(BF16)
    new = new_ref[0]
    hm = hm_ref[...]

    k16 = jnp.concatenate([r[0, 0] for r in kc_pages], axis=0)
    v16 = jnp.concatenate([r[0, 0] for r in vc_pages], axis=0)
    kc = _rope(_compress_math(k16, w_ref, pe_ref, 0), c_ref[...], s1_ref[...], s2_ref[...])
    vc = _compress_math(v16, w_ref, pe_ref, 1)
    ncp = kc.shape[0]
    s_c = _dot_nt(qzb, kc.astype(BF16))
    cidx = lax.broadcasted_iota(jnp.int32, (N_HEADS, ncp), 1)
    vis = (cidx * CMP_STRIDE + (CMP_LEN - 1) <= qpos) & (cidx < n_cmp)
    m_c = jnp.max(jnp.where(vis, s_c, NEG), axis=1, keepdims=True)
    e_c = jnp.where(vis, jnp.exp(s_c - m_c), 0.0)
    p_c = e_c / jnp.maximum(jnp.sum(e_c, axis=1, keepdims=True), 1e-30)
    p_cb = p_c.astype(BF16)
    o_cmp = _dot(p_cb, vc.astype(BF16))
    imp_h = _dot(p_cb, cov_ref[...])
    imp = jnp.concatenate(
        [jnp.sum(imp_h[g * GROUP:(g + 1) * GROUP, :], axis=0, keepdims=True) for g in range(N_KV)], axis=0)

    nbp = imp.shape[1]
    bidx = lax.broadcasted_iota(jnp.int32, (N_KV, nbp), 1)
    cur = qpos // SEL_LEN
    eligible = bidx * SEL_LEN <= qpos
    forced = (bidx == 0) | (bidx == cur) | (bidx == cur - 1)
    score = jnp.where(forced, FORCE_SCORE, jnp.where(eligible, imp, -jnp.inf))
    sel = _select_blocks(score, n_blk, 1)
    sel_h = jnp.concatenate(
        [jnp.broadcast_to(sel[g:g + 1, :], (GROUP, nbp)) for g in range(N_KV)], axis=0)
    kmask = _dot(sel_h.astype(BF16), exp_ref[...])

    def attend(k_all, v_all, ok, k_new, v_new):
        s = _dot_nt(qzb, k_all.astype(BF16))
        s_new = jnp.sum(qzb.astype(F32) * k_new.astype(BF16).astype(F32), axis=1, keepdims=True)
        m = jnp.maximum(jnp.max(jnp.where(ok, s, NEG), axis=1, keepdims=True), s_new)
        e = jnp.where(ok, jnp.exp(s - m), 0.0)
        e_new = jnp.exp(s_new - m)
        den = jnp.maximum(jnp.sum(e, axis=1, keepdims=True) + e_new, 1e-30)
        p = (e / den).astype(BF16)
        p_new = (e_new / den).astype(BF16).astype(F32)
        return _dot(p, v_all.astype(BF16)) + p_new * v_new.astype(BF16).astype(F32)

    ks_all = jnp.concatenate([r[0, 0] for r in ks_pages], axis=0)
    vs_all = jnp.concatenate([r[0, 0] for r in vs_pages], axis=0)
    kpos = lax.broadcasted_iota(jnp.int32, (N_HEADS, past_len), 1)
    ok_s = (kmask > 0.5) & (kpos <= qpos)
    o_sel = attend(ks_all, vs_all, ok_s, new[0:1, :], new[1:2, :])

    wpos = (past_len - wb) + lax.broadcasted_iota(jnp.int32, (N_HEADS, wb), 1)
    ok_w = (wpos <= qpos) & (qpos - wpos < WINDOW)
    o_win = attend(kwin_ref[0, 0], vwin_ref[0, 0], ok_w, new[2:3, :], new[3:4, :])

    g = g_ref[0]
    o = g[:, 0:1] * o_cmp + g[:, 1:2] * o_sel + g[:, 2:3] * o_win
    o = o * hm
    o_ref[0] = (o[:, 0:64] + o[:, 64:128]) + (o[:, 128:192] + o[:, 192:256])


def _attn_sample(page_table, kc5, vc5, ks4, vs4, kwin, vwin, qz, new_rows, gates, wb_w, pe, ctabs,
                 cov, expand, headmask, layer, past_len, n_cmp, n_blk):
    n, n_pages = page_table.shape
    wbuf = kwin.shape[2]
    rows_pp = ks4.shape[2]
    c, s1, s2 = ctabs
    ncp = c.shape[0]

    def page_spec(shape, p):
        return pl.BlockSpec(shape, lambda i, pt: (layer, pt[i, p], 0, 0))

    full = lambda shape: pl.BlockSpec(shape, lambda i, pt: (0,) * len(shape))
    in_specs = ([page_spec((1, 1, rows_pp // CMP_STRIDE, 4096), p) for p in range(n_pages)]
                + [page_spec((1, 1, rows_pp // CMP_STRIDE, 4096), p) for p in range(n_pages)]
                + [page_spec((1, 1, rows_pp, D_KV), p) for p in range(n_pages)]
                + [page_spec((1, 1, rows_pp, D_KV), p) for p in range(n_pages)]
                + [pl.BlockSpec((1, 1, wbuf, D_KV), lambda i, pt: (layer, i, 0, 0)),
                   pl.BlockSpec((1, 1, wbuf, D_KV), lambda i, pt: (layer, i, 0, 0)),
                   pl.BlockSpec((1, N_HEADS, D_KV), lambda i, pt: (i, 0, 0)),
                   pl.BlockSpec((1, 8, D_KV), lambda i, pt: (i, 0, 0)),
                   pl.BlockSpec((1, N_HEADS, LANE), lambda i, pt: (i, 0, 0)),
                   full((2, 2, 4096, D_KV)), full((2, 2, 1, 4096)),
                   full((ncp, D_KV)), full((ncp, D_KV)), full((ncp, D_KV)),
                   full(cov.shape), full(expand.shape), full(headmask.shape)])
    kern = functools.partial(_attn_s_kernel, n_pages=n_pages, past_len=past_len, n_cmp=n_cmp,
                             n_blk=n_blk, wb=wbuf)
    grid_spec = pltpu.PrefetchScalarGridSpec(
        num_scalar_prefetch=1, grid=(n,), in_specs=in_specs,
        out_specs=pl.BlockSpec((1, N_HEADS, HEAD_DIM), lambda i, pt: (i, 0, 0)))
    args = ([kc5] * n_pages + [vc5] * n_pages + [ks4] * n_pages + [vs4] * n_pages
            + [kwin, vwin, qz, new_rows, gates, wb_w, pe, c, s1, s2, cov, expand, headmask])
    return pl.pallas_call(
        kern,
        grid_spec=grid_spec,
        out_shape=jax.ShapeDtypeStruct((n, N_HEADS, HEAD_DIM), F32),
        compiler_params=pltpu.CompilerParams(dimension_semantics=("arbitrary",),
                                             vmem_limit_bytes=56 * 1024 * 1024),
        name="attn_sample",
    )(page_table, *args)


def kernel(x_prompt, x_sample, cache_k_cmp, cache_v_cmp, cache_k_sel, cache_v_sel, page_table, state_k_win, state_v_win, state_pool, state_conv, w_in, pool_w, pool_scale, conv_w, cmp_pe, cmp_w, proj_a, proj_b, proj_c, w_out, ln_g, ln_b):
    b, t, _ = x_prompt.shape
    ns = x_sample.shape[0]
    assert x_sample.shape[1] == 1
    depth = w_in.shape[0]
    n_pool, page = cache_k_cmp.shape[1], cache_k_cmp.shape[2]
    n_pages = page_table.shape[1]
    past_len = n_pages * page
    wbuf = state_k_win.shape[2]
    assert t % CHUNK == 0 and page % CMP_STRIDE == 0 and past_len % CMP_STRIDE == 0

    n_cmp_p = (t - CMP_LEN) // CMP_STRIDE + 1
    n_blk_p = -(-t // SEL_LEN)
    nch_p = t // CMP_STRIDE
    tabs_p = _rope_tables(np.arange(t), LANE)
    ctabs_p = _rope_tables(np.arange(nch_p) * CMP_STRIDE + CMP_LEN - 1, D_KV)
    cov_p = jnp.asarray(_cover_t(nch_p, n_blk_p, n_cmp_p, n_blk_p), BF16)

    total_s = past_len + 1
    n_cmp_s = (total_s - CMP_LEN) // CMP_STRIDE + 1
    n_blk_s = -(-total_s // SEL_LEN)
    nch_s = past_len // CMP_STRIDE
    tabs_s = _rope_tables(np.asarray([past_len]), LANE)
    ctabs_s = _rope_tables(np.arange(nch_s) * CMP_STRIDE + CMP_LEN - 1, D_KV)
    cov_s = jnp.asarray(_cover_t(nch_s, LANE, n_cmp_s, n_blk_s).T, BF16)
    expand = jnp.asarray((np.arange(LANE)[:, None] == (np.arange(past_len)[None, :] // SEL_LEN)), BF16)
    headmask = jnp.asarray((np.arange(N_HEADS)[:, None] // GROUP) == (np.arange(D_KV)[None, :] // HEAD_DIM), F32)

    kc5 = cache_k_cmp.reshape(depth, n_pool, page // CMP_STRIDE, CMP_STRIDE * D_KV)
    vc5 = cache_v_cmp.reshape(depth, n_pool, page // CMP_STRIDE, CMP_STRIDE * D_KV)
    ks4 = cache_k_sel.reshape(depth, n_pool, page, D_KV)
    vs4 = cache_v_sel.reshape(depth, n_pool, page, D_KV)
    kwin = state_k_win.reshape(depth, ns, wbuf, D_KV)
    vwin = state_v_win.reshape(depth, ns, wbuf, D_KV)

    xp = x_prompt.reshape(b * t, D_MODEL)
    xs = x_sample.reshape(ns, D_MODEL)
    xp_bf = xp.astype(BF16)
    xs_bf = xs.astype(BF16)
    st_p, st_s = [], []
    for l in range(depth):
        w_bf = _perm_w_in(w_in[l])
        wb_w, pe = _cmp_weights(cmp_w[l], cmp_pe[l])
        pw_bf = pool_w[l].astype(BF16)
        pscale = pool_scale[l].reshape(1, D_POOL)
        cw = conv_w[l]
        pa, pb, pc, wo = (a[l].astype(BF16) for a in (proj_a, proj_b, proj_c, w_out))
        lg = ln_g[l].reshape(1, D_MODEL)
        lb = ln_b[l].reshape(1, D_MODEL)

        h = _inproj(xp_bf, w_bf, 512)
        kvst, qt, ksg, kwg, vst, vwt, gt = _prep_prompt(h, tabs_p, b, t)
        k16 = kvst[:, 0:D_KV].reshape(b, nch_p, CMP_STRIDE * D_KV)
        v16 = kvst[:, D_KV:2 * D_KV].reshape(b, nch_p, CMP_STRIDE * D_KV)
        kcg, vct = _compress_prompt(k16, v16, wb_w, pe, ctabs_p, b)
        yc = _attn_prompt(qt, kcg, vct, ksg, vst, kwg, vwt, gt, cov_p, b, t, n_cmp_p, n_blk_p)
        ya, yb, hct = _ab_prompt(h, pw_bf, pscale, cw, b, t)
        xp_new, xp_bf = _tail(ya, yb, yc, h, xp, pa, pb, pc, wo, lg, lb, 512)
        kv5 = kvst.reshape(b, t, 6, N_KV, HEAD_DIM)
        keep = min(WINDOW, t)
        st_p.append((kv5[:, :, 0], kv5[:, :, 1], kv5[:, :, 2], kv5[:, :, 3],
                     kv5[:, t - keep:, 4], kv5[:, t - keep:, 5],
                     h[:, C_PU:C_PU + D_POOL].reshape(b, t, D_POOL)[:, t - POOL_HIST:],
                     hct[:, 8 - (CONV_WIDTH - 1):]))
        xp = xp_new

        hs = _inproj(xs_bf, w_bf, ns)
        kvs, qrot, gates = _prep_sample(hs, tabs_s)
        q3 = qrot.reshape(ns, N_HEADS, 1, HEAD_DIM)
        qz = (q3 * jnp.eye(N_KV, dtype=F32)[jnp.arange(N_HEADS) // GROUP][None, :, :, None]).reshape(ns, N_HEADS, D_KV)
        new_rows = jnp.concatenate(
            [kvs[:, 2 * D_KV:6 * D_KV].reshape(ns, 4, D_KV), jnp.zeros((ns, 4, D_KV), F32)], axis=1)
        g4 = gates[:, :64].reshape(ns, N_KV, 4, GROUP)[:, :, :3, :]
        g_h = jnp.transpose(g4, (0, 1, 3, 2)).reshape(ns, N_HEADS, 3)
        g_h = jnp.concatenate([g_h, jnp.zeros((ns, N_HEADS, LANE - 3), F32)], axis=2)
        ycs = _attn_sample(page_table, kc5, vc5, ks4, vs4, kwin, vwin, qz, new_rows, g_h, wb_w, pe,
                           ctabs_s, cov_s, expand, headmask, l, past_len, n_cmp_s, n_blk_s)
        ycs = ycs.reshape(ns, D_ATTN)
        pool_hist = state_pool[l]
        conv_hist = state_conv[l]
        yas, ybs, hcs = _ab_sample(hs, jnp.transpose(pool_hist, (1, 0, 2)), jnp.transpose(conv_hist, (1, 0, 2)),
                                   pw_bf, pscale, cw, past_len)
        xs_new, xs_bf = _tail(yas, ybs, ycs, hs, xs, pa, pb, pc, wo, lg, lb, ns)
        kvs5 = kvs.reshape(ns, 1, 6, N_KV, HEAD_DIM)
        keep_s = min(WINDOW, past_len + 1)
        kw_all = jnp.concatenate([state_k_win[l], kvs5[:, :, 4]], axis=1)
        vw_all = jnp.concatenate([state_v_win[l], kvs5[:, :, 5]], axis=1)
        st_s.append((kvs5[:, :, 0], kvs5[:, :, 1], kvs5[:, :, 2], kvs5[:, :, 3],
                     kw_all[:, kw_all.shape[1] - keep_s:], vw_all[:, vw_all.shape[1] - keep_s:],
                     jnp.concatenate([pool_hist, hs[:, None, C_PU:C_PU + D_POOL]], axis=1)[:, 1:],
                     jnp.concatenate([conv_hist, hcs[:, None, :]], axis=1)[:, 1:]))
        xs = xs_new

    stack = lambda states, i: jnp.stack([s[i] for s in states], axis=0)
    return ((xp.reshape(b, t, D_MODEL), xs.reshape(ns, 1, D_MODEL))
            + tuple(stack(st_p, i) for i in range(8)) + tuple(stack(st_s, i) for i in range(8)))
```

```python
import functools

import numpy as np
import jax
import jax.numpy as jnp
from jax import lax
from jax.experimental import pallas as pl
from jax.experimental.pallas import tpu as pltpu

D_MODEL = 1024
DEPTH = 2
D_POOL = 512
POOL_WINDOWS = (2, 4, 8, 16)
POOL_GROUP = D_POOL // len(POOL_WINDOWS)
POOL_HIST = max(POOL_WINDOWS) - 1
D_CONV = 512
CONV_WIDTH = 3
HEAD_DIM = 64
N_HEADS = 16
N_KV = 4
GROUP = N_HEADS // N_KV
D_ATTN = N_HEADS * HEAD_DIM
D_KV = N_KV * HEAD_DIM
ROT_DIM = HEAD_DIM // 4
ROPE_THETA = 500000.0
CMP_LEN = 32
CMP_STRIDE = 16
SEL_LEN = 64
SEL_SHIFT = 6
N_SEL = 8
WINDOW = 512
FORCE_SCORE = 1.0e4
LN_EPS = 1e-5
ALPHA = (2 * DEPTH) ** 0.25
SCALE = HEAD_DIM ** -0.5

C_PU, C_PZ, C_CB, C_CC, C_CX, C_CZ = 0, 512, 1024, 1536, 2048, 2560
C_Q = 3072
C_AZ = 4096
C_MG = 5120
C_KV = 8192
C_NG = 9728
D_H = 9856
NG_PAD = 128

NEG = -1.0e30
LANE = 128
TQ = 512
KCH = 512
NSPLIT = 1
AUG = 16
VMEM_LIMIT = 48 * 1024 * 1024

BF16 = jnp.bfloat16
F32 = jnp.float32


def _cparams(sem):
    return pltpu.CompilerParams(dimension_semantics=sem, vmem_limit_bytes=VMEM_LIMIT)


def _dot(a, b):
    return jnp.dot(a, b, preferred_element_type=F32)


def _dot_nt(a, b):
    return lax.dot_general(a, b, (((1,), (1,)), ((), ())), preferred_element_type=F32)


def _silu(x):
    return x * jax.nn.sigmoid(x)


def _perm_w_in(w):
    ab = w[:, 0:3072]
    q = w[:, 3072:4096]
    kv = w[:, 4096:5632]
    ng = w[:, 5632:5680]
    az = w[:, 5680:6704]
    mg = w[:, 6704:9776]
    idx = np.full((NG_PAD,), 48, np.int32)
    for g in range(N_KV):
        for br in range(3):
            for r in range(GROUP):
                idx[g * 16 + br * 4 + r] = (GROUP * g + r) * 3 + br
    ng_ext = jnp.concatenate([ng, jnp.zeros((w.shape[0], 1), w.dtype)], axis=1)
    ng_p = jnp.take(ng_ext, jnp.asarray(idx), axis=1)
    return jnp.concatenate([ab, q, az, mg, kv, ng_p], axis=1).astype(BF16)


def _rope_tables(pos, width):
    pos = jnp.asarray(pos, F32)
    p = pos.shape[0]
    inv_freq = 1.0 / (ROPE_THETA ** (np.arange(0, ROT_DIM, 2, dtype=np.float32) / ROT_DIM))
    ang = pos[:, None] * jnp.asarray(inv_freq, F32)[None, :]
    cos = jnp.cos(ang)
    sin = jnp.sin(ang)
    z8 = jnp.zeros((p, 8), F32)
    z48 = jnp.zeros((p, 48), F32)
    c64 = jnp.concatenate([cos, cos, jnp.ones((p, 48), F32)], axis=1)
    s1 = jnp.concatenate([-sin, z8, z48], axis=1)
    s2 = jnp.concatenate([z8, sin, z48], axis=1)
    rep = width // HEAD_DIM
    return tuple(jnp.tile(t, (1, rep)) for t in (c64, s1, s2))


def _rope(x, c, s1, s2):
    w = x.shape[-1]
    up = pltpu.roll(x, w - 8, 1)
    dn = pltpu.roll(x, 8, 1)
    return x * c + up * s1 + dn * s2


def _tile_lanes(t, rep):
    return t if rep == 1 else jnp.concatenate([t] * rep, axis=1)


def _cmp_weights(cmp_w_l, cmp_pe_l):
    eye = jnp.eye(N_KV, dtype=F32)
    w = cmp_w_l.reshape(2, 2, CMP_STRIDE, HEAD_DIM, HEAD_DIM)
    wb = jnp.einsum('khsde,gf->khsgdfe', w, eye).reshape(2, 2, CMP_STRIDE * D_KV, D_KV)
    pe = cmp_pe_l.reshape(2, 2, CMP_STRIDE, 1, HEAD_DIM)
    pe = jnp.broadcast_to(pe, (2, 2, CMP_STRIDE, N_KV, HEAD_DIM)).reshape(2, 2, 1, CMP_STRIDE * D_KV)
    return wb.astype(BF16), pe


def _cmp_weights_rows(cmp_w_l, cmp_pe_l):
    eye2 = jnp.eye(2, dtype=F32)
    w = cmp_w_l.reshape(2, 2, CMP_STRIDE, HEAD_DIM, HEAD_DIM)
    ws = jnp.einsum('ktshe,gf->ksghtfe', w, eye2).reshape(2, CMP_STRIDE, 2 * HEAD_DIM, 4 * HEAD_DIM)
    pec = jnp.einsum('kph,kphe->ke', cmp_pe_l, cmp_w_l.reshape(2, CMP_LEN, HEAD_DIM, HEAD_DIM),
                     precision=lax.Precision.HIGHEST)
    pec = jnp.tile(pec, (1, N_KV)).reshape(2, 1, D_KV)
    return ws.astype(BF16), pec


def _cover_t(n_cmp_pad, n_blk_pad, n_cmp, n_blk):
    c0 = np.arange(n_cmp_pad)[None, :] * CMP_STRIDE
    s0 = np.arange(n_blk_pad)[:, None] * SEL_LEN
    m = (c0 < s0 + SEL_LEN) & (c0 + CMP_LEN > s0)
    m &= (np.arange(n_cmp_pad)[None, :] < n_cmp) & (np.arange(n_blk_pad)[:, None] < n_blk)
    return m.astype(np.float32)


def _mm_kernel(x_ref, w_ref, o_ref):
    o_ref[...] = _dot(x_ref[...], w_ref[...])


def _inproj(x_bf, w_bf, tm):
    m = x_bf.shape[0]
    tn = 896
    return pl.pallas_call(
        _mm_kernel,
        grid=(D_H // tn, m // tm),
        in_specs=[pl.BlockSpec((tm, D_MODEL), lambda j, i: (i, 0)),
                  pl.BlockSpec((D_MODEL, tn), lambda j, i: (0, j))],
        out_specs=pl.BlockSpec((tm, tn), lambda j, i: (i, j)),
        out_shape=jax.ShapeDtypeStruct((m, D_H), F32),
        compiler_params=_cparams(("arbitrary", "arbitrary")),
        name="inproj",
    )(x_bf, w_bf)


def _prep_p_kernel(q_ref, kcv_ref, ksv_ref, kwv_ref, ng_ref, c_ref, s1_ref, s2_ref,
                   kcvn_ref, stt_ref, qt_ref, ksa_ref, kwg_ref, vst_ref, vwt_ref, gt_ref):
    tt = q_ref.shape[0]
    c, s1, s2 = c_ref[...], s1_ref[...], s2_ref[...]
    q = _rope(q_ref[...], _tile_lanes(c, 8), _tile_lanes(s1, 8), _tile_lanes(s2, 8)) * SCALE
    qt_ref[0] = q.T.astype(BF16)
    c2, s12, s22 = _tile_lanes(c, 2), _tile_lanes(s1, 2), _tile_lanes(s2, 2)
    kcv = kcv_ref[...]
    ksv = ksv_ref[...]
    kwv = kwv_ref[...]
    ks = _rope(ksv[:, :D_KV], c2, s12, s22)
    kw = _rope(kwv[:, :D_KV], c2, s12, s22)
    kcvn_ref[...] = kcv
    kcvt = kcv.T
    stt_ref[0, 0] = kcvt[:D_KV]
    stt_ref[0, 1] = kcvt[D_KV:]
    kst = ks.T
    vst = ksv[:, D_KV:].T
    kwt = kw.T
    vwt = kwv[:, D_KV:].T
    stt_ref[0, 2] = kst
    stt_ref[0, 3] = vst
    stt_ref[0, 4] = kwt
    stt_ref[0, 5] = vwt
    for j in range(tt // KCH):
        vst_ref[0, j] = vst[:, j * KCH:(j + 1) * KCH].astype(BF16)
        vwt_ref[0, j] = vwt[:, j * KCH:(j + 1) * KCH].astype(BF16)
    row = lax.broadcasted_iota(jnp.int32, (tt, LANE), 0)
    lane = lax.broadcasted_iota(jnp.int32, (tt, LANE), 1)
    blk = jnp.right_shift(jnp.bitwise_and(row, KCH - 1), SEL_SHIFT)
    onehot = jnp.where((lane >= HEAD_DIM) & (lane - HEAD_DIM == blk), 1.0, 0.0)
    zpad = jnp.zeros((tt, LANE - HEAD_DIM), F32)
    for g in range(N_KV):
        kg = ks[:, g * HEAD_DIM:(g + 1) * HEAD_DIM]
        ksa_ref[0, g] = (jnp.concatenate([kg, zpad], axis=1) + onehot).astype(BF16)
        kwg_ref[0, g] = kw[:, g * HEAD_DIM:(g + 1) * HEAD_DIM].astype(BF16)
    gt_ref[0] = jax.nn.sigmoid(ng_ref[...]).T[:64, :]


def _prep_prompt(h, tabs, b, t):
    tt = 512
    nt = t // tt
    row = lambda bi, ti: bi * nt + ti
    c, s1, s2 = tabs
    tab_spec = pl.BlockSpec((tt, LANE), lambda bi, ti: (ti, 0))
    return pl.pallas_call(
        _prep_p_kernel,
        grid=(b, nt),
        in_specs=[pl.BlockSpec((tt, 1024), lambda bi, ti: (row(bi, ti), C_Q // 1024)),
                  pl.BlockSpec((tt, 512), lambda bi, ti: (row(bi, ti), C_KV // 512)),
                  pl.BlockSpec((tt, 512), lambda bi, ti: (row(bi, ti), C_KV // 512 + 1)),
                  pl.BlockSpec((tt, 512), lambda bi, ti: (row(bi, ti), C_KV // 512 + 2)),
                  pl.BlockSpec((tt, LANE), lambda bi, ti: (row(bi, ti), C_NG // LANE)),
                  tab_spec, tab_spec, tab_spec],
        out_specs=[pl.BlockSpec((tt, 512), lambda bi, ti: (row(bi, ti), 0)),
                   pl.BlockSpec((1, 6, D_KV, tt), lambda bi, ti: (bi, 0, 0, ti)),
                   pl.BlockSpec((1, 1024, tt), lambda bi, ti: (bi, 0, ti)),
                   pl.BlockSpec((1, N_KV, tt, LANE), lambda bi, ti: (bi, 0, ti, 0)),
                   pl.BlockSpec((1, N_KV, tt, HEAD_DIM), lambda bi, ti: (bi, 0, ti, 0)),
                   pl.BlockSpec((1, tt // KCH, D_KV, KCH), lambda bi, ti: (bi, ti, 0, 0)),
                   pl.BlockSpec((1, tt // KCH, D_KV, KCH), lambda bi, ti: (bi, ti, 0, 0)),
                   pl.BlockSpec((1, 64, tt), lambda bi, ti: (bi, 0, ti))],
        out_shape=[jax.ShapeDtypeStruct((b * t, 512), F32),
                   jax.ShapeDtypeStruct((b, 6, D_KV, t), F32),
                   jax.ShapeDtypeStruct((b, 1024, t), BF16),
                   jax.ShapeDtypeStruct((b, N_KV, t, LANE), BF16),
                   jax.ShapeDtypeStruct((b, N_KV, t, HEAD_DIM), BF16),
                   jax.ShapeDtypeStruct((b, t // KCH, D_KV, KCH), BF16),
                   jax.ShapeDtypeStruct((b, t // KCH, D_KV, KCH), BF16),
                   jax.ShapeDtypeStruct((b, 64, t), F32)],
        compiler_params=_cparams(("arbitrary", "arbitrary")),
        name="prep_prompt",
    )(h, h, h, h, h, c, s1, s2)


def _compress_math(x16, w_ref, pe_ref, kv):
    top = _dot((x16 + pe_ref[kv, 0]).astype(BF16), w_ref[kv, 0])
    bot = _dot((x16 + pe_ref[kv, 1]).astype(BF16), w_ref[kv, 1])
    bot_up = jnp.concatenate([bot[1:], jnp.zeros((1, D_KV), F32)], axis=0)
    return top + bot_up


def _compress_p_kernel(k16_ref, v16_ref, w_ref, pe_ref, c_ref, s1_ref, s2_ref, kcg_ref, vct_ref):
    kc = _compress_math(k16_ref[0], w_ref, pe_ref, 0)
    kc = _rope(kc, c_ref[...], s1_ref[...], s2_ref[...])
    vc = _compress_math(v16_ref[0], w_ref, pe_ref, 1)
    for g in range(N_KV):
        kcg_ref[0, g] = kc[:, g * HEAD_DIM:(g + 1) * HEAD_DIM].astype(BF16)
    vct_ref[0] = vc.T.astype(BF16)


def _compress_prompt(k16, v16, wb, pe, ctabs, b):
    nch = k16.shape[1]
    c, s1, s2 = ctabs
    full = lambda shape: pl.BlockSpec(shape, lambda bi: (0,) * len(shape))
    return pl.pallas_call(
        _compress_p_kernel,
        grid=(b,),
        in_specs=[pl.BlockSpec((1, nch, 4096), lambda bi: (bi, 0, 0)),
                  pl.BlockSpec((1, nch, 4096), lambda bi: (bi, 0, 0)),
                  full((2, 2, 4096, D_KV)), full((2, 2, 1, 4096)),
                  full((nch, D_KV)), full((nch, D_KV)), full((nch, D_KV))],
        out_specs=[pl.BlockSpec((1, N_KV, nch, HEAD_DIM), lambda bi: (bi, 0, 0, 0)),
                   pl.BlockSpec((1, D_KV, nch), lambda bi: (bi, 0, 0))],
        out_shape=[jax.ShapeDtypeStruct((b, N_KV, nch, HEAD_DIM), BF16),
                   jax.ShapeDtypeStruct((b, D_KV, nch), BF16)],
        compiler_params=_cparams(("arbitrary",)),
        name="compress_prompt",
    )(k16, v16, wb, pe, c, s1, s2)


def _select_blocks(score, n_blk, axis):
    idx = lax.broadcasted_iota(jnp.int32, score.shape, axis)
    rank = jnp.zeros(score.shape, F32)
    for j in range(n_blk):
        if axis == 0:
            sj = score[j:j + 1, :]
        else:
            sj = score[:, j:j + 1]
        beats = (sj > score) | ((sj == score) & (idx > j))
        rank = rank + jnp.where(beats, 1.0, 0.0)
    keep = (rank < float(N_SEL)) & (score > -jnp.inf) & (idx < n_blk)
    return jnp.where(keep, 1.0, 0.0)


def _attn_p_kernel(qt_ref, kcg_ref, vct_ref, ksa_ref, vst_ref, kwg_ref, vwt_ref, gt_ref, cov_ref, cb_ref, tri_ref,
                   o_ref, selb_ref, m_ref, l_ref, acc_ref, comb_ref, *, n_blk):
    qi = pl.program_id(2)
    t0 = qi * TQ
    n = GROUP * TQ
    qt = qt_ref[0]
    qcat = jnp.concatenate([qt[r * HEAD_DIM:(r + 1) * HEAD_DIM, :] for r in range(GROUP)], axis=1)
    gt = gt_ref[0]

    def heads(x):
        return jnp.concatenate([x] * GROUP, axis=1)

    def gate(branch):
        return jnp.concatenate([gt[branch * GROUP + r:branch * GROUP + r + 1, :] for r in range(GROUP)], axis=1)

    s = _dot(kcg_ref[0, 0], qcat) + heads(cb_ref[0])
    e = jnp.exp(s - jnp.max(s, axis=0, keepdims=True))
    qpos = t0 + jnp.bitwise_and(lax.broadcasted_iota(jnp.int32, (1, n), 1), TQ - 1)
    inv = jnp.where(qpos >= CMP_LEN - 1, 1.0, 0.0) / jnp.sum(e, axis=0, keepdims=True)
    p = (e * inv).astype(BF16)
    comb_ref[...] = gate(0) * _dot(vct_ref[0], p)
    imp4 = _dot(cov_ref[...], p)
    imp = imp4[:, 0:TQ]
    for r in range(1, GROUP):
        imp = imp + imp4[:, r * TQ:(r + 1) * TQ]

    nb = imp.shape[0]
    per = KCH // SEL_LEN
    bidx = lax.broadcasted_iota(jnp.int32, (nb, TQ), 0)
    qp = t0 + lax.broadcasted_iota(jnp.int32, (nb, TQ), 1)
    cur = jnp.right_shift(qp, SEL_SHIFT)
    eligible = bidx * SEL_LEN <= qp
    forced = (bidx == 0) | (bidx == cur) | (bidx == cur - 1)
    score = jnp.where(forced, FORCE_SCORE, jnp.where(eligible, imp, -jnp.inf))
    selb = jnp.where(_select_blocks(score, n_blk, 0) > 0.5, 0.0, NEG)
    zrows = jnp.zeros((AUG - per, TQ), F32)
    for c in range(nb // per):
        selb_ref[c] = jnp.concatenate([selb[c * per:(c + 1) * per, :], zrows], axis=0).astype(BF16)

    def reset():
        m_ref[...] = jnp.full((1, n), NEG, F32)
        l_ref[...] = jnp.zeros((1, n), F32)
        acc_ref[...] = jnp.zeros((HEAD_DIM, n), F32)

    def step(kchunk, vchunk, aug, bias):
        w = n // NSPLIT
        for part in range(NSPLIT):
            ln = slice(part * w, (part + 1) * w)
            rhs = qcat[:, ln]
            if aug is not None:
                rhs = jnp.concatenate([rhs, heads(aug)[:, ln], jnp.zeros((LANE - HEAD_DIM - AUG, w), BF16)], axis=0)
            s = _dot(kchunk, rhs)
            if bias is not None:
                s = s + heads(bias)[:, ln]
            m_old = m_ref[:, ln]
            m_new = jnp.maximum(m_old, jnp.max(s, axis=0, keepdims=True))
            alpha = jnp.exp(m_old - m_new)
            e = jnp.exp(s - m_new)
            l_ref[:, ln] = alpha * l_ref[:, ln] + jnp.sum(e, axis=0, keepdims=True)
            acc_ref[:, ln] = alpha * acc_ref[:, ln] + _dot(vchunk, e.astype(BF16))
            m_ref[:, ln] = m_new

    def flush(branch):
        comb_ref[...] = comb_ref[...] + gate(branch) * (acc_ref[...] / jnp.maximum(l_ref[...], 1e-30))

    def kslice(ref, c):
        return ref[0, 0, pl.ds(pl.multiple_of(c * KCH, KCH), KCH), :]

    reset()

    def sel_body(c, carry):
        step(kslice(ksa_ref, c), vst_ref[0, c], selb_ref[c], None)
        return carry

    lax.fori_loop(0, qi, sel_body, 0)
    step(kslice(ksa_ref, qi), vst_ref[0, qi], selb_ref[qi], tri_ref[0])
    flush(1)

    reset()
    wch = WINDOW // KCH

    def win_body(c, carry):
        which = jnp.where(c == qi, 0, jnp.where(c == qi - wch, 1, 2))
        step(kslice(kwg_ref, c), vwt_ref[0, c], None, tri_ref[which])
        return carry

    lax.fori_loop(jnp.maximum(qi - wch, 0), qi + 1, win_body, 0)
    flush(2)

    comb = comb_ref[...]
    o_ref[...] = jnp.concatenate([comb[:, r * TQ:(r + 1) * TQ].T for r in range(GROUP)], axis=1)


def _attn_prompt(qt, kcg, vct, ksa, vst, kwg, vwt, gt, cov, cb, tri, b, t, n_blk):
    nq = t // TQ
    nch = t // KCH
    ncp = kcg.shape[2]
    assert WINDOW % KCH == 0 and KCH == TQ and KCH // SEL_LEN <= AUG
    kern = functools.partial(_attn_p_kernel, n_blk=n_blk)
    n = GROUP * TQ
    return pl.pallas_call(
        kern,
        grid=(b, N_KV, nq),
        in_specs=[pl.BlockSpec((1, D_KV, TQ), lambda bi, g, qi: (bi, g, qi)),
                  pl.BlockSpec((1, 1, ncp, HEAD_DIM), lambda bi, g, qi: (bi, g, 0, 0)),
                  pl.BlockSpec((1, HEAD_DIM, ncp), lambda bi, g, qi: (bi, g, 0)),
                  pl.BlockSpec((1, 1, t, LANE), lambda bi, g, qi: (bi, g, 0, 0)),
                  pl.BlockSpec((1, nch, HEAD_DIM, KCH), lambda bi, g, qi: (bi, 0, g, 0)),
                  pl.BlockSpec((1, 1, t, HEAD_DIM), lambda bi, g, qi: (bi, g, 0, 0)),
                  pl.BlockSpec((1, nch, HEAD_DIM, KCH), lambda bi, g, qi: (bi, 0, g, 0)),
                  pl.BlockSpec((1, 16, TQ), lambda bi, g, qi: (bi, g, qi)),
                  pl.BlockSpec(cov.shape, lambda bi, g, qi: (0, 0)),
                  pl.BlockSpec((1, ncp, TQ), lambda bi, g, qi: (qi, 0, 0)),
                  pl.BlockSpec(tri.shape, lambda bi, g, qi: (0, 0, 0))],
        out_specs=pl.BlockSpec((TQ, D_KV), lambda bi, g, qi: (bi * nq + qi, g)),
        out_shape=jax.ShapeDtypeStruct((b * t, D_ATTN), F32),
        scratch_shapes=[pltpu.VMEM((cov.shape[0] * SEL_LEN // KCH, AUG, TQ), BF16),
                        pltpu.VMEM((1, n), F32), pltpu.VMEM((1, n), F32),
                        pltpu.VMEM((HEAD_DIM, n), F32), pltpu.VMEM((HEAD_DIM, n), F32)],
        compiler_params=_cparams(("arbitrary", "arbitrary", "arbitrary")),
        name="attn_prompt",
    )(qt, kcg, vct, ksa, vst, kwg, vwt, gt, cov, cb, tri)


def _attn_bias_tiles(t, ncp, n_cmp):
    k = np.arange(KCH)[:, None]
    q = np.arange(TQ)[None, :]
    tri = np.stack([np.where(k <= q, 0.0, NEG), np.where(k > q, 0.0, NEG), np.zeros((KCH, TQ))]).astype(np.float32)
    c = np.arange(ncp)[None, :, None]
    qpos = (np.arange(t // TQ)[:, None, None] * TQ) + np.arange(TQ)[None, None, :]
    cb = np.where((c * CMP_STRIDE + CMP_LEN - 1 <= qpos) & (c < n_cmp), 0.0, NEG).astype(np.float32)
    return jnp.asarray(cb), jnp.asarray(tri)


def _pool_group_mix(pooled, pw_ref, scale, pz):
    mixed = jnp.concatenate(
        [_dot(pooled[:, g * POOL_GROUP:(g + 1) * POOL_GROUP].astype(BF16), pw_ref[g])
         for g in range(len(POOL_WINDOWS))], axis=1)
    return (mixed * scale) * _silu(pz)


def _ab_p_kernel(pu_ref, pz_ref, cb_ref, cc_ref, cx_ref, cz_ref, pup_ref, ccp_ref, cxp_ref,
                 pw_ref, ps_ref, cw_ref, ya_ref, yb_ref, hct_ref, *, tm, tiles_per_seq):
    i = pl.program_id(0)
    ti = i % tiles_per_seq
    valid = jnp.where(ti > 0, 1.0, 0.0)
    pu = pu_ref[...]
    ext = jnp.concatenate([pup_ref[...] * valid, pu], axis=0)
    pos = ti * tm + lax.broadcasted_iota(jnp.int32, (tm, POOL_GROUP), 0)
    acc = pu
    means = []
    k = 1
    for gi, w in enumerate(POOL_WINDOWS):
        ch = slice(gi * POOL_GROUP, (gi + 1) * POOL_GROUP)
        while k < w:
            acc = acc + ext[16 - k:16 - k + tm, :]
            k += 1
        count = jnp.minimum(w, pos + 1).astype(F32)
        means.append(acc[:, ch] / count)
    pooled = jnp.concatenate(means, axis=1) - pu
    ya_ref[...] = _pool_group_mix(pooled, pw_ref, ps_ref[...], pz_ref[...]).astype(BF16)

    hc = cc_ref[...] * cx_ref[...]
    hprev = (ccp_ref[...] * cxp_ref[...]) * valid
    hext = jnp.concatenate([hprev, hc], axis=0)
    cw = cw_ref[...]
    conv = hext[14:14 + tm, :] * cw[0:1, :]
    conv = conv + hext[15:15 + tm, :] * cw[1:2, :]
    conv = conv + hc * cw[2:3, :]
    yb_ref[...] = ((cb_ref[...] * conv) * _silu(cz_ref[...])).astype(BF16)
    hct_ref[0] = hc[tm - 8:tm, :]


def _ab_prompt(h, pw_bf, pscale, cw, b, t):
    tm = 512
    m = b * t
    tps = t // tm
    colb = lambda c: (lambda i: (i, c // 512))
    prev = lambda c: (lambda i: (jnp.maximum(i * (tm // 16) - 1, 0), c // 512))
    kern = functools.partial(_ab_p_kernel, tm=tm, tiles_per_seq=tps)
    full = lambda shape: pl.BlockSpec(shape, lambda i: (0,) * len(shape))
    return pl.pallas_call(
        kern,
        grid=(m // tm,),
        in_specs=[pl.BlockSpec((tm, 512), colb(C_PU)), pl.BlockSpec((tm, 512), colb(C_PZ)),
                  pl.BlockSpec((tm, 512), colb(C_CB)), pl.BlockSpec((tm, 512), colb(C_CC)),
                  pl.BlockSpec((tm, 512), colb(C_CX)), pl.BlockSpec((tm, 512), colb(C_CZ)),
                  pl.BlockSpec((16, 512), prev(C_PU)), pl.BlockSpec((16, 512), prev(C_CC)),
                  pl.BlockSpec((16, 512), prev(C_CX)),
                  full((len(POOL_WINDOWS), POOL_GROUP, POOL_GROUP)), full((1, D_POOL)), full((CONV_WIDTH, D_CONV))],
        out_specs=[pl.BlockSpec((tm, D_POOL), lambda i: (i, 0)),
                   pl.BlockSpec((tm, D_CONV), lambda i: (i, 0)),
                   pl.BlockSpec((1, 8, D_CONV), lambda i: (i // tps, 0, 0))],
        out_shape=[jax.ShapeDtypeStruct((m, D_POOL), BF16),
                   jax.ShapeDtypeStruct((m, D_CONV), BF16),
                   jax.ShapeDtypeStruct((b, 8, D_CONV), F32)],
        compiler_params=_cparams(("arbitrary",)),
        name="ab_prompt",
    )(h, h, h, h, h, h, h, h, h, pw_bf, pscale, cw)


def _tail_kernel(ya_ref, yb_ref, yc_ref, az_ref, ga_ref, gb_ref, gc_ref, x_ref,
                 pa_ref, pb_ref, pc_ref, wo_ref, lg_ref, lb_ref, o_ref, obf_ref):
    yc = (yc_ref[...] * _silu(az_ref[...])).astype(BF16)
    merged = (jax.nn.sigmoid(ga_ref[...]) * _dot(ya_ref[...], pa_ref[...])
              + jax.nn.sigmoid(gb_ref[...]) * _dot(yb_ref[...], pb_ref[...])
              + jax.nn.sigmoid(gc_ref[...]) * _dot(yc, pc_ref[...]))
    y = _dot(merged.astype(BF16), wo_ref[...])
    z = ALPHA * x_ref[...] + y
    mu = jnp.mean(z, axis=-1, keepdims=True)
    var = jnp.mean(jnp.square(z - mu), axis=-1, keepdims=True)
    out = (z - mu) * lax.rsqrt(var + LN_EPS) * lg_ref[...] + lb_ref[...]
    o_ref[...] = out
    obf_ref[...] = out.astype(BF16)


def _tail(ya, yb, yc, h, x, pa, pb, pc, wo, lg, lb, tm):
    m = x.shape[0]
    full = lambda shape: pl.BlockSpec(shape, lambda i: (0,) * len(shape))
    hcol = lambda c: pl.BlockSpec((tm, 1024), lambda i: (i, c // 1024))
    return pl.pallas_call(
        _tail_kernel,
        grid=(m // tm,),
        in_specs=[pl.BlockSpec((tm, D_POOL), lambda i: (i, 0)),
                  pl.BlockSpec((tm, D_CONV), lambda i: (i, 0)),
                  pl.BlockSpec((tm, D_ATTN), lambda i: (i, 0)),
                  hcol(C_AZ), hcol(C_MG), hcol(C_MG + 1024), hcol(C_MG + 2048),
                  pl.BlockSpec((tm, D_MODEL), lambda i: (i, 0)),
                  full((D_POOL, D_MODEL)), full((D_CONV, D_MODEL)), full((D_ATTN, D_MODEL)),
                  full((D_MODEL, D_MODEL)), full((1, D_MODEL)), full((1, D_MODEL))],
        out_specs=[pl.BlockSpec((tm, D_MODEL), lambda i: (i, 0)),
                   pl.BlockSpec((tm, D_MODEL), lambda i: (i, 0))],
        out_shape=[jax.ShapeDtypeStruct((m, D_MODEL), F32),
                   jax.ShapeDtypeStruct((m, D_MODEL), BF16)],
        compiler_params=_cparams(("arbitrary",)),
        name="tail",
    )(ya, yb, yc, h, h, h, h, x, pa, pb, pc, wo, lg, lb)


def _prep_s_kernel(q_ref, kcv_ref, ksv_ref, kwv_ref, ng_ref, c_ref, s1_ref, s2_ref,
                   kvst_ref, qr_ref, g_ref):
    c, s1, s2 = c_ref[...], s1_ref[...], s2_ref[...]
    qr_ref[...] = _rope(q_ref[...], _tile_lanes(c, 8), _tile_lanes(s1, 8), _tile_lanes(s2, 8)) * SCALE
    c2, s12, s22 = _tile_lanes(c, 2), _tile_lanes(s1, 2), _tile_lanes(s2, 2)
    ksv = ksv_ref[...]
    kwv = kwv_ref[...]
    kvst_ref[:, 0:512] = kcv_ref[...]
    kvst_ref[:, 512:768] = _rope(ksv[:, :D_KV], c2, s12, s22)
    kvst_ref[:, 768:1024] = ksv[:, D_KV:]
    kvst_ref[:, 1024:1280] = _rope(kwv[:, :D_KV], c2, s12, s22)
    kvst_ref[:, 1280:1536] = kwv[:, D_KV:]
    g_ref[...] = jax.nn.sigmoid(ng_ref[...])


def _prep_sample(h, tabs):
    n = h.shape[0]
    c, s1, s2 = tabs
    tab = pl.BlockSpec((1, LANE), lambda i: (0, 0))
    return pl.pallas_call(
        _prep_s_kernel,
        grid=(1,),
        in_specs=[pl.BlockSpec((n, 1024), lambda i: (0, C_Q // 1024)),
                  pl.BlockSpec((n, 512), lambda i: (0, C_KV // 512)),
                  pl.BlockSpec((n, 512), lambda i: (0, C_KV // 512 + 1)),
                  pl.BlockSpec((n, 512), lambda i: (0, C_KV // 512 + 2)),
                  pl.BlockSpec((n, LANE), lambda i: (0, C_NG // LANE)),
                  tab, tab, tab],
        out_specs=[pl.BlockSpec((n, 1536), lambda i: (0, 0)),
                   pl.BlockSpec((n, 1024), lambda i: (0, 0)),
                   pl.BlockSpec((n, LANE), lambda i: (0, 0))],
        out_shape=[jax.ShapeDtypeStruct((n, 1536), F32),
                   jax.ShapeDtypeStruct((n, 1024), F32),
                   jax.ShapeDtypeStruct((n, LANE), F32)],
        compiler_params=_cparams(("arbitrary",)),
        name="prep_sample",
    )(h, h, h, h, h, c, s1, s2)


def _ab_s_kernel(pu_ref, pz_ref, cb_ref, cc_ref, cx_ref, cz_ref, ph_ref, ch_ref,
                 pw_ref, ps_ref, cw_ref, ya_ref, yb_ref, hc_ref, *, pos0):
    pu = pu_ref[...]
    acc = pu
    means = []
    k = 1
    for gi, w in enumerate(POOL_WINDOWS):
        ch = slice(gi * POOL_GROUP, (gi + 1) * POOL_GROUP)
        while k < w:
            acc = acc + ph_ref[POOL_HIST - k]
            k += 1
        means.append(acc[:, ch] / float(min(w, pos0 + 1)))
    pooled = jnp.concatenate(means, axis=1) - pu
    ya_ref[...] = _pool_group_mix(pooled, pw_ref, ps_ref[...], pz_ref[...]).astype(BF16)
    hc = cc_ref[...] * cx_ref[...]
    cw = cw_ref[...]
    conv = ch_ref[0] * cw[0:1, :]
    conv = conv + ch_ref[1] * cw[1:2, :]
    conv = conv + hc * cw[2:3, :]
    yb_ref[...] = ((cb_ref[...] * conv) * _silu(cz_ref[...])).astype(BF16)
    hc_ref[...] = hc


def _ab_sample(h, pool_hist_t, conv_hist_t, pw_bf, pscale, cw, pos0):
    n = h.shape[0]
    colb = lambda c: pl.BlockSpec((n, 512), lambda i: (0, c // 512))
    full = lambda shape: pl.BlockSpec(shape, lambda i: (0,) * len(shape))
    kern = functools.partial(_ab_s_kernel, pos0=pos0)
    return pl.pallas_call(
        kern,
        grid=(1,),
        in_specs=[colb(C_PU), colb(C_PZ), colb(C_CB), colb(C_CC), colb(C_CX), colb(C_CZ),
                  full((POOL_HIST, n, D_POOL)), full((CONV_WIDTH - 1, n, D_CONV)),
                  full((len(POOL_WINDOWS), POOL_GROUP, POOL_GROUP)), full((1, D_POOL)), full((CONV_WIDTH, D_CONV))],
        out_specs=[full((n, D_POOL)), full((n, D_CONV)), full((n, D_CONV))],
        out_shape=[jax.ShapeDtypeStruct((n, D_POOL), BF16),
                   jax.ShapeDtypeStruct((n, D_CONV), BF16),
                   jax.ShapeDtypeStruct((n, D_CONV), F32)],
        compiler_params=_cparams(("arbitrary",)),
        name="ab_sample",
    )(h, h, h, h, h, h, pool_hist_t, conv_hist_t, pw_bf, pscale, cw)


def _attn_s_kernel(pt_ref, *refs, n_pages, past_len, n_cmp, n_blk, wb):
    del pt_ref
    np_ = n_pages
    kc_pages = refs[0:np_]
    vc_pages = refs[np_:2 * np_]
    ks_pages = refs[2 * np_:3 * np_]
    vs_pages = refs[3 * np_:4 * np_]
    (kwin_ref, vwin_ref, qz_ref, new_ref, newc_ref, g_ref, ws_ref, pec_ref, c_ref, s1_ref, s2_ref,
     cov_ref, exp_ref, hm_ref, o_ref, kwo_ref, vwo_ref, knat_ref, vnat_ref) = refs[4 * np_:]
    qpos = past_len
    rows = past_len // np_
    qz = qz_ref[0]
    qzb = qz.astype(BF16)
    new = new_ref[0]
    hm = hm_ref[...]
    nch = past_len // CMP_STRIDE

    def page_t(r):
        return r[0, 0].reshape(D_KV, rows)

    for p in range(np_):
        kn = page_t(kc_pages[p]).T
        vn = page_t(vc_pages[p]).T
        for hf in range(2):
            knat_ref[hf, p * rows:(p + 1) * rows, :] = kn[:, hf * LANE:(hf + 1) * LANE]
            vnat_ref[hf, p * rows:(p + 1) * rows, :] = vn[:, hf * LANE:(hf + 1) * LANE]

    def compress(nat_ref, kv):
        halves = []
        for hf in range(2):
            acc = jnp.zeros((nch, 2 * LANE), F32)
            for s in range(CMP_STRIDE):
                x = nat_ref[hf, pl.ds(s, nch, stride=CMP_STRIDE), :].astype(BF16)
                acc = acc + _dot(x, ws_ref[kv, s])
            bot_up = jnp.concatenate([acc[1:, LANE:], jnp.zeros((1, LANE), F32)], axis=0)
            halves.append(acc[:, :LANE] + bot_up)
        return jnp.concatenate(halves, axis=1) + pec_ref[kv]

    kc = _rope(compress(knat_ref, 0), c_ref[...], s1_ref[...], s2_ref[...])
    vc = compress(vnat_ref, 1)
    ncp = kc.shape[0]
    s_c = _dot_nt(qzb, kc.astype(BF16))
    cidx = lax.broadcasted_iota(jnp.int32, (N_HEADS, ncp), 1)
    vis = (cidx * CMP_STRIDE + (CMP_LEN - 1) <= qpos) & (cidx < n_cmp)
    m_c = jnp.max(jnp.where(vis, s_c, NEG), axis=1, keepdims=True)
    e_c = jnp.where(vis, jnp.exp(s_c - m_c), 0.0)
    p_c = e_c / jnp.maximum(jnp.sum(e_c, axis=1, keepdims=True), 1e-30)
    p_cb = p_c.astype(BF16)
    o_cmp = _dot(p_cb, vc.astype(BF16))
    imp_h = _dot(p_cb, cov_ref[...])
    imp = jnp.concatenate(
        [jnp.sum(imp_h[g * GROUP:(g + 1) * GROUP, :], axis=0, keepdims=True) for g in range(N_KV)], axis=0)

    nbp = imp.shape[1]
    bidx = lax.broadcasted_iota(jnp.int32, (N_KV, nbp), 1)
    cur = qpos // SEL_LEN
    eligible = bidx * SEL_LEN <= qpos
    forced = (bidx == 0) | (bidx == cur) | (bidx == cur - 1)
    score = jnp.where(forced, FORCE_SCORE, jnp.where(eligible, imp, -jnp.inf))
    sel = _select_blocks(score, n_blk, 1)
    sel_h = jnp.concatenate(
        [jnp.broadcast_to(sel[g:g + 1, :], (GROUP, nbp)) for g in range(N_KV)], axis=0)
    kmask = _dot(sel_h.astype(BF16), exp_ref[...])

    def attend(k_t, v_t, ok, k_new, v_new):
        s = _dot(qzb, k_t)
        s_new = jnp.sum(qzb.astype(F32) * k_new.astype(BF16).astype(F32), axis=1, keepdims=True)
        m = jnp.maximum(jnp.max(jnp.where(ok, s, NEG), axis=1, keepdims=True), s_new)
        e = jnp.where(ok, jnp.exp(s - m), 0.0)
        e_new = jnp.exp(s_new - m)
        den = jnp.maximum(jnp.sum(e, axis=1, keepdims=True) + e_new, 1e-30)
        p = (e / den).astype(BF16)
        p_new = (e_new / den).astype(BF16).astype(F32)
        return _dot_nt(p, v_t) + p_new * v_new.astype(BF16).astype(F32)

    ks_t = jnp.concatenate([page_t(r).astype(BF16) for r in ks_pages], axis=1)
    vs_t = jnp.concatenate([page_t(r).astype(BF16) for r in vs_pages], axis=1)
    kpos = lax.broadcasted_iota(jnp.int32, (N_HEADS, past_len), 1)
    ok_s = (kmask > 0.5) & (kpos <= qpos)
    o_sel = attend(ks_t, vs_t, ok_s, new[0:1, :], new[1:2, :])

    kw_t = kwin_ref[0, 0].reshape(D_KV, wb)
    vw_t = vwin_ref[0, 0].reshape(D_KV, wb)
    wpos = (past_len - wb) + lax.broadcasted_iota(jnp.int32, (N_HEADS, wb), 1)
    ok_w = (wpos <= qpos) & (qpos - wpos < WINDOW)
    o_win = attend(kw_t.astype(BF16), vw_t.astype(BF16), ok_w, new[2:3, :], new[3:4, :])

    g = g_ref[0]
    o = g[:, 0:1] * o_cmp + g[:, 1:2] * o_sel + g[:, 2:3] * o_win
    o = o * hm
    o_ref[0] = (o[:, 0:64] + o[:, 64:128]) + (o[:, 128:192] + o[:, 192:256])

    cols = newc_ref[0]
    lane = lax.broadcasted_iota(jnp.int32, (D_KV, wb), 1)
    kwo = jnp.where(lane == wb - 1, cols[:, 2:3], pltpu.roll(kw_t, wb - 1, 1))
    vwo = jnp.where(lane == wb - 1, cols[:, 3:4], pltpu.roll(vw_t, wb - 1, 1))
    kwo_ref[0] = kwo.reshape(N_KV, HEAD_DIM, wb)
    vwo_ref[0] = vwo.reshape(N_KV, HEAD_DIM, wb)


def _attn_sample(page_table, kc_t, vc_t, ks_t, vs_t, kwin_t, vwin_t, qz, new_rows, gates, ws, pec, ctabs,
                 cov, expand, headmask, layer, past_len, n_cmp, n_blk):
    n, n_pages = page_table.shape
    wbuf = kwin_t.shape[-1]
    rows_pp = ks_t.shape[-1]
    c, s1, s2 = ctabs
    ncp = c.shape[0]

    def page_spec(p):
        return pl.BlockSpec((1, 1, N_KV, HEAD_DIM, rows_pp), lambda i, pt: (layer, pt[i, p], 0, 0, 0))

    full = lambda shape: pl.BlockSpec(shape, lambda i, pt: (0,) * len(shape))
    win_spec = pl.BlockSpec((1, 1, N_KV, HEAD_DIM, wbuf), lambda i, pt: (layer, i, 0, 0, 0))
    in_specs = ([page_spec(p) for p in range(n_pages)] * 4
                + [win_spec, win_spec,
                   pl.BlockSpec((1, N_HEADS, D_KV), lambda i, pt: (i, 0, 0)),
                   pl.BlockSpec((1, 8, D_KV), lambda i, pt: (i, 0, 0)),
                   pl.BlockSpec((1, D_KV, 8), lambda i, pt: (i, 0, 0)),
                   pl.BlockSpec((1, N_HEADS, LANE), lambda i, pt: (i, 0, 0)),
                   full(ws.shape), full(pec.shape),
                   full((ncp, D_KV)), full((ncp, D_KV)), full((ncp, D_KV)),
                   full(cov.shape), full(expand.shape), full(headmask.shape)])
    kern = functools.partial(_attn_s_kernel, n_pages=n_pages, past_len=past_len, n_cmp=n_cmp,
                             n_blk=n_blk, wb=wbuf)
    wout = pl.BlockSpec((1, N_KV, HEAD_DIM, wbuf), lambda i, pt: (i, 0, 0, 0))
    grid_spec = pltpu.PrefetchScalarGridSpec(
        num_scalar_prefetch=1, grid=(n,), in_specs=in_specs,
        out_specs=[pl.BlockSpec((1, N_HEADS, HEAD_DIM), lambda i, pt: (i, 0, 0)), wout, wout],
        scratch_shapes=[pltpu.VMEM((2, past_len, LANE), F32), pltpu.VMEM((2, past_len, LANE), F32)])
    args = ([kc_t] * n_pages + [vc_t] * n_pages + [ks_t] * n_pages + [vs_t] * n_pages
            + [kwin_t, vwin_t, qz, new_rows, jnp.transpose(new_rows, (0, 2, 1)), gates, ws, pec, c, s1, s2,
               cov, expand, headmask])
    return pl.pallas_call(
        kern,
        grid_spec=grid_spec,
        out_shape=[jax.ShapeDtypeStruct((n, N_HEADS, HEAD_DIM), F32),
                   jax.ShapeDtypeStruct((n, N_KV, HEAD_DIM, wbuf), F32),
                   jax.ShapeDtypeStruct((n, N_KV, HEAD_DIM, wbuf), F32)],
        compiler_params=_cparams(("arbitrary",)),
        name="attn_sample",
    )(page_table, *args)


def kernel(x_prompt, x_sample, cache_k_cmp, cache_v_cmp, cache_k_sel, cache_v_sel, page_table, state_k_win, state_v_win, state_pool, state_conv, w_in, pool_w, pool_scale, conv_w, cmp_pe, cmp_w, proj_a, proj_b, proj_c, w_out, ln_g, ln_b):
    b, t, _ = x_prompt.shape
    ns = x_sample.shape[0]
    assert x_sample.shape[1] == 1
    depth = w_in.shape[0]
    page = cache_k_cmp.shape[2]
    n_pages = page_table.shape[1]
    past_len = n_pages * page
    wbuf = state_k_win.shape[2]
    assert t % 512 == 0 and page % CMP_STRIDE == 0 and wbuf == WINDOW and past_len >= WINDOW

    n_cmp_p = (t - CMP_LEN) // CMP_STRIDE + 1
    n_blk_p = -(-t // SEL_LEN)
    nch_p = t // CMP_STRIDE
    tabs_p = _rope_tables(np.arange(t), LANE)
    ctabs_p = _rope_tables(np.arange(nch_p) * CMP_STRIDE + CMP_LEN - 1, D_KV)
    cov_p = jnp.asarray(_cover_t(nch_p, n_blk_p, n_cmp_p, n_blk_p), BF16)
    cb_p, tri_p = _attn_bias_tiles(t, nch_p, n_cmp_p)

    total_s = past_len + 1
    n_cmp_s = (total_s - CMP_LEN) // CMP_STRIDE + 1
    n_blk_s = -(-total_s // SEL_LEN)
    nch_s = past_len // CMP_STRIDE
    tabs_s = _rope_tables(np.asarray([past_len]), LANE)
    ctabs_s = _rope_tables(np.arange(nch_s) * CMP_STRIDE + CMP_LEN - 1, D_KV)
    cov_s = jnp.asarray(_cover_t(nch_s, LANE, n_cmp_s, n_blk_s).T, BF16)
    expand = jnp.asarray((np.arange(LANE)[:, None] == (np.arange(past_len)[None, :] // SEL_LEN)), BF16)
    headmask = jnp.asarray((np.arange(N_HEADS)[:, None] // GROUP) == (np.arange(D_KV)[None, :] // HEAD_DIM), F32)

    fm = lambda a: jnp.transpose(a, (0, 1, 3, 4, 2))
    kc_t, vc_t, ks_t, vs_t = fm(cache_k_cmp), fm(cache_v_cmp), fm(cache_k_sel), fm(cache_v_sel)
    kwin_t, vwin_t = fm(state_k_win), fm(state_v_win)
    rm = lambda a: jnp.transpose(a, (0, 3, 1, 2))

    xp = x_prompt.reshape(b * t, D_MODEL)
    xs = x_sample.reshape(ns, D_MODEL)
    xp_bf = xp.astype(BF16)
    xs_bf = xs.astype(BF16)
    st_p, st_s = [], []
    for l in range(depth):
        w_bf = _perm_w_in(w_in[l])
        wb_w, pe = _cmp_weights(cmp_w[l], cmp_pe[l])
        ws, pec = _cmp_weights_rows(cmp_w[l], cmp_pe[l])
        pw_bf = pool_w[l].astype(BF16)
        pscale = pool_scale[l].reshape(1, D_POOL)
        cw = conv_w[l]
        pa, pb, pc, wo = (a[l].astype(BF16) for a in (proj_a, proj_b, proj_c, w_out))
        lg = ln_g[l].reshape(1, D_MODEL)
        lb = ln_b[l].reshape(1, D_MODEL)

        h = _inproj(xp_bf, w_bf, 1024)
        kcvn, stt, qt, ksa, kwg, vst, vwt, gt = _prep_prompt(h, tabs_p, b, t)
        k16 = kcvn[:, 0:D_KV].reshape(b, nch_p, CMP_STRIDE * D_KV)
        v16 = kcvn[:, D_KV:2 * D_KV].reshape(b, nch_p, CMP_STRIDE * D_KV)
        kcg, vct = _compress_prompt(k16, v16, wb_w, pe, ctabs_p, b)
        yc = _attn_prompt(qt, kcg, vct, ksa, vst, kwg, vwt, gt, cov_p, cb_p, tri_p, b, t, n_blk_p)
        ya, yb, hct = _ab_prompt(h, pw_bf, pscale, cw, b, t)
        xp_new, xp_bf = _tail(ya, yb, yc, h, xp, pa, pb, pc, wo, lg, lb, 512)
        st5 = stt.reshape(b, 6, N_KV, HEAD_DIM, t)
        keep = min(WINDOW, t)
        st_p.append((rm(st5[:, 0]), rm(st5[:, 1]), rm(st5[:, 2]), rm(st5[:, 3]),
                     rm(st5[:, 4, :, :, t - keep:]), rm(st5[:, 5, :, :, t - keep:]),
                     h[:, C_PU:C_PU + D_POOL].reshape(b, t, D_POOL)[:, t - POOL_HIST:],
                     hct[:, 8 - (CONV_WIDTH - 1):]))
        xp = xp_new

        hs = _inproj(xs_bf, w_bf, ns)
        kvs, qrot, gates = _prep_sample(hs, tabs_s)
        q3 = qrot.reshape(ns, N_HEADS, 1, HEAD_DIM)
        qz = (q3 * jnp.eye(N_KV, dtype=F32)[jnp.arange(N_HEADS) // GROUP][None, :, :, None]).reshape(ns, N_HEADS, D_KV)
        new_rows = jnp.concatenate(
            [kvs[:, 2 * D_KV:6 * D_KV].reshape(ns, 4, D_KV), jnp.zeros((ns, 4, D_KV), F32)], axis=1)
        g4 = gates[:, :64].reshape(ns, N_KV, 4, GROUP)[:, :, :3, :]
        g_h = jnp.transpose(g4, (0, 1, 3, 2)).reshape(ns, N_HEADS, 3)
        g_h = jnp.concatenate([g_h, jnp.zeros((ns, N_HEADS, LANE - 3), F32)], axis=2)
        ycs, kwo, vwo = _attn_sample(page_table, kc_t, vc_t, ks_t, vs_t, kwin_t, vwin_t, qz, new_rows, g_h, ws, pec,
                                     ctabs_s, cov_s, expand, headmask, l, past_len, n_cmp_s, n_blk_s)
        ycs = ycs.reshape(ns, D_ATTN)
        pool_hist = state_pool[l]
        conv_hist = state_conv[l]
        yas, ybs, hcs = _ab_sample(hs, jnp.transpose(pool_hist, (1, 0, 2)), jnp.transpose(conv_hist, (1, 0, 2)),
                                   pw_bf, pscale, cw, past_len)
        xs_new, xs_bf = _tail(yas, ybs, ycs, hs, xs, pa, pb, pc, wo, lg, lb, ns)
        kvs5 = kvs.reshape(ns, 1, 6, N_KV, HEAD_DIM)
        st_s.append((kvs5[:, :, 0], kvs5[:, :, 1], kvs5[:, :, 2], kvs5[:, :, 3], rm(kwo), rm(vwo),
                     jnp.concatenate([pool_hist, hs[:, None, C_PU:C_PU + D_POOL]], axis=1)[:, 1:],
                     jnp.concatenate([conv_hist, hcs[:, None, :]], axis=1)[:, 1:]))
        xs = xs_new

    stack = lambda states, i: jnp.stack([s[i] for s in states], axis=0)
    return ((xp.reshape(b, t, D_MODEL), xs.reshape(ns, 1, D_MODEL))
            + tuple(stack(st_p, i) for i in range(8)) + tuple(stack(st_s, i) for i in range(8)))
```

```python
import functools

import numpy as np
import jax
import jax.numpy as jnp
from jax import lax
from jax.experimental import pallas as pl
from jax.experimental.pallas import tpu as pltpu

D_MODEL = 1024
DEPTH = 2
D_POOL = 512
POOL_WINDOWS = (2, 4, 8, 16)
POOL_GROUP = D_POOL // len(POOL_WINDOWS)
POOL_HIST = max(POOL_WINDOWS) - 1
D_CONV = 512
CONV_WIDTH = 3
HEAD_DIM = 64
N_HEADS = 16
N_KV = 4
GROUP = N_HEADS // N_KV
D_ATTN = N_HEADS * HEAD_DIM
D_KV = N_KV * HEAD_DIM
ROT_DIM = HEAD_DIM // 4
ROPE_THETA = 500000.0
CMP_LEN = 32
CMP_STRIDE = 16
SEL_LEN = 64
SEL_SHIFT = 6
N_SEL = 8
WINDOW = 512
FORCE_SCORE = 1.0e4
LN_EPS = 1e-5
ALPHA = (2 * DEPTH) ** 0.25
SCALE = HEAD_DIM ** -0.5

C_PU, C_PZ, C_CB, C_CC, C_CX, C_CZ = 0, 512, 1024, 1536, 2048, 2560
C_Q = 3072
C_AZ = 4096
C_MG = 5120
C_KV = 8192
C_NG = 9728
D_H = 9984
NG_PAD = 128

NEG = -1.0e30
LANE = 128
TQ = 512
KCH = 512
NSPLIT = 1
AUG = 16
VMEM_LIMIT = 48 * 1024 * 1024

BF16 = jnp.bfloat16
F32 = jnp.float32


def _cparams(sem):
    return pltpu.CompilerParams(dimension_semantics=sem, vmem_limit_bytes=VMEM_LIMIT)


def _dot(a, b):
    return jnp.dot(a, b, preferred_element_type=F32)


def _dot_nt(a, b):
    return lax.dot_general(a, b, (((1,), (1,)), ((), ())), preferred_element_type=F32)


def _silu(x):
    return x * jax.nn.sigmoid(x)


def _perm_w_in(w):
    ab = w[:, 0:3072]
    q = w[:, 3072:4096]
    kv = w[:, 4096:5632]
    ng = w[:, 5632:5680]
    az = w[:, 5680:6704]
    mg = w[:, 6704:9776]
    idx = np.full((NG_PAD,), 48, np.int32)
    for g in range(N_KV):
        for br in range(3):
            for r in range(GROUP):
                idx[g * 16 + br * 4 + r] = (GROUP * g + r) * 3 + br
    ng_ext = jnp.concatenate([ng, jnp.zeros((w.shape[0], 1), w.dtype)], axis=1)
    ng_p = jnp.take(ng_ext, jnp.asarray(idx), axis=1)
    pad = jnp.zeros((w.shape[0], D_H - C_NG - NG_PAD), w.dtype)
    return jnp.concatenate([ab, q, az, mg, kv, ng_p, pad], axis=1).astype(BF16)


def _rope_tables(pos, width):
    pos = np.asarray(pos, np.float32)
    p = pos.shape[0]
    inv_freq = (1.0 / (ROPE_THETA ** (np.arange(0, ROT_DIM, 2, dtype=np.float32) / ROT_DIM))).astype(np.float32)
    ang = (pos[:, None] * inv_freq[None, :]).astype(np.float32).astype(np.float64)
    cos = np.cos(ang).astype(np.float32)
    sin = np.sin(ang).astype(np.float32)
    z8 = np.zeros((p, 8), np.float32)
    z48 = np.zeros((p, 48), np.float32)
    c64 = np.concatenate([cos, cos, np.ones((p, 48), np.float32)], axis=1)
    s1 = np.concatenate([-sin, z8, z48], axis=1)
    s2 = np.concatenate([z8, sin, z48], axis=1)
    rep = width // HEAD_DIM
    return tuple(jnp.asarray(np.tile(t, (1, rep))) for t in (c64, s1, s2))


def _rope(x, c, s1, s2):
    w = x.shape[-1]
    up = pltpu.roll(x, w - 8, 1)
    dn = pltpu.roll(x, 8, 1)
    return x * c + up * s1 + dn * s2


def _tile_lanes(t, rep):
    return t if rep == 1 else jnp.concatenate([t] * rep, axis=1)


def _cmp_weights(cmp_w_l, cmp_pe_l):
    eye = jnp.eye(N_KV, dtype=F32)
    w = cmp_w_l.reshape(2, 2, CMP_STRIDE, HEAD_DIM, HEAD_DIM)
    wb = jnp.einsum('khsde,gf->khsgdfe', w, eye).reshape(2, 2, CMP_STRIDE * D_KV, D_KV)
    pe = cmp_pe_l.reshape(2, 2, CMP_STRIDE, 1, HEAD_DIM)
    pe = jnp.broadcast_to(pe, (2, 2, CMP_STRIDE, N_KV, HEAD_DIM)).reshape(2, 2, 1, CMP_STRIDE * D_KV)
    return wb.astype(BF16), pe


def _cmp_weights_rows(cmp_w_l, cmp_pe_l):
    eye2 = jnp.eye(2, dtype=F32)
    w = cmp_w_l.reshape(2, 2, CMP_STRIDE, HEAD_DIM, HEAD_DIM)
    ws = jnp.einsum('ktshe,gf->ksghtfe', w, eye2).reshape(2, CMP_STRIDE, 2 * HEAD_DIM, 4 * HEAD_DIM)
    pec = jnp.einsum('kph,kphe->ke', cmp_pe_l, cmp_w_l.reshape(2, CMP_LEN, HEAD_DIM, HEAD_DIM),
                     precision=lax.Precision.HIGHEST)
    pec = jnp.tile(pec, (1, N_KV)).reshape(2, 1, D_KV)
    return ws.astype(BF16), pec


def _cover_t(n_cmp_pad, n_blk_pad, n_cmp, n_blk):
    c0 = np.arange(n_cmp_pad)[None, :] * CMP_STRIDE
    s0 = np.arange(n_blk_pad)[:, None] * SEL_LEN
    m = (c0 < s0 + SEL_LEN) & (c0 + CMP_LEN > s0)
    m &= (np.arange(n_cmp_pad)[None, :] < n_cmp) & (np.arange(n_blk_pad)[:, None] < n_blk)
    return m.astype(np.float32)


def _mm_kernel(x_ref, w_ref, o_ref):
    o_ref[...] = _dot(x_ref[...], w_ref[...])


def _inproj(x_bf, w_bf, tm):
    m = x_bf.shape[0]
    tn = 768
    return pl.pallas_call(
        _mm_kernel,
        grid=(D_H // tn, m // tm),
        in_specs=[pl.BlockSpec((tm, D_MODEL), lambda j, i: (i, 0)),
                  pl.BlockSpec((D_MODEL, tn), lambda j, i: (0, j))],
        out_specs=pl.BlockSpec((tm, tn), lambda j, i: (i, j)),
        out_shape=jax.ShapeDtypeStruct((m, D_H), F32),
        compiler_params=_cparams(("arbitrary", "arbitrary")),
        name="inproj",
    )(x_bf, w_bf)


def _prep_p_kernel(q_ref, kcv_ref, ksv_ref, kwv_ref, ng_ref, c_ref, s1_ref, s2_ref,
                   *rest):
    (kcvn_ref, kct_ref, vct_ref, kstt_ref, vstt_ref, kwtt_ref, vwtt_ref,
     qt_ref, ksa_ref, kwg_ref, vst_ref, vwt_ref, gt_ref) = rest[-13:]
    tt = q_ref.shape[0]
    c, s1, s2 = c_ref[...], s1_ref[...], s2_ref[...]
    q = _rope(q_ref[...], _tile_lanes(c, 8), _tile_lanes(s1, 8), _tile_lanes(s2, 8)) * SCALE
    qt_ref[0] = q.T.astype(BF16)
    c2, s12, s22 = _tile_lanes(c, 2), _tile_lanes(s1, 2), _tile_lanes(s2, 2)
    kcv = kcv_ref[...]
    ksv = ksv_ref[...]
    kwv = kwv_ref[...]
    ks = _rope(ksv[:, :D_KV], c2, s12, s22)
    kw = _rope(kwv[:, :D_KV], c2, s12, s22)
    kcvn_ref[...] = kcv
    kcvt = kcv.T
    kct_ref[0, 0] = kcvt[:D_KV]
    vct_ref[0, 0] = kcvt[D_KV:]
    kst = ks.T
    vst = ksv[:, D_KV:].T
    kwt = kw.T
    vwt = kwv[:, D_KV:].T
    kstt_ref[0, 0] = kst
    vstt_ref[0, 0] = vst
    kwtt_ref[0, 0] = kwt
    vwtt_ref[0, 0] = vwt
    for j in range(tt // KCH):
        vst_ref[0, j] = vst[:, j * KCH:(j + 1) * KCH].astype(BF16)
        vwt_ref[0, j] = vwt[:, j * KCH:(j + 1) * KCH].astype(BF16)
    row = lax.broadcasted_iota(jnp.int32, (tt, LANE), 0)
    lane = lax.broadcasted_iota(jnp.int32, (tt, LANE), 1)
    blk = jnp.right_shift(jnp.bitwise_and(row, KCH - 1), SEL_SHIFT)
    onehot = jnp.where((lane >= HEAD_DIM) & (lane - HEAD_DIM == blk), 1.0, 0.0)
    zpad = jnp.zeros((tt, LANE - HEAD_DIM), F32)
    for g in range(N_KV):
        kg = ks[:, g * HEAD_DIM:(g + 1) * HEAD_DIM]
        ksa_ref[0, g] = (jnp.concatenate([kg, zpad], axis=1) + onehot).astype(BF16)
        kwg_ref[0, g] = kw[:, g * HEAD_DIM:(g + 1) * HEAD_DIM].astype(BF16)
    gt_ref[0] = jax.nn.sigmoid(ng_ref[...]).T[:64, :]


def _prep_prompt(h, tabs, b, t, layer, depth, prev):
    tt = 512
    assert min(WINDOW, t) == tt
    nt = t // tt
    row = lambda bi, ti: bi * nt + ti
    c, s1, s2 = tabs
    tab_spec = pl.BlockSpec((tt, LANE), lambda bi, ti: (ti, 0))
    in_specs = [pl.BlockSpec((tt, 1024), lambda bi, ti: (row(bi, ti), C_Q // 1024)),
                pl.BlockSpec((tt, 512), lambda bi, ti: (row(bi, ti), C_KV // 512)),
                pl.BlockSpec((tt, 512), lambda bi, ti: (row(bi, ti), C_KV // 512 + 1)),
                pl.BlockSpec((tt, 512), lambda bi, ti: (row(bi, ti), C_KV // 512 + 2)),
                pl.BlockSpec((tt, LANE), lambda bi, ti: (row(bi, ti), C_NG // LANE)),
                tab_spec, tab_spec, tab_spec]
    args = [h, h, h, h, h, c, s1, s2]
    aliases = {}
    if prev is not None:
        for j, a in enumerate(prev):
            aliases[len(args)] = 1 + j
            args.append(a)
            in_specs.append(pl.BlockSpec(memory_space=pl.ANY))
    full_t = pl.BlockSpec((1, 1, D_KV, tt), lambda bi, ti: (layer, bi, 0, ti))
    win_t = pl.BlockSpec((1, 1, D_KV, tt), lambda bi, ti: (layer, bi, 0, 0))
    st_full = jax.ShapeDtypeStruct((depth, b, D_KV, t), F32)
    st_win = jax.ShapeDtypeStruct((depth, b, D_KV, tt), F32)
    return pl.pallas_call(
        _prep_p_kernel,
        grid=(b, nt),
        in_specs=in_specs,
        out_specs=[pl.BlockSpec((tt, 512), lambda bi, ti: (row(bi, ti), 0)),
                   full_t, full_t, full_t, full_t, win_t, win_t,
                   pl.BlockSpec((1, 1024, tt), lambda bi, ti: (bi, 0, ti)),
                   pl.BlockSpec((1, N_KV, tt, LANE), lambda bi, ti: (bi, 0, ti, 0)),
                   pl.BlockSpec((1, N_KV, tt, HEAD_DIM), lambda bi, ti: (bi, 0, ti, 0)),
                   pl.BlockSpec((1, tt // KCH, D_KV, KCH), lambda bi, ti: (bi, ti, 0, 0)),
                   pl.BlockSpec((1, tt // KCH, D_KV, KCH), lambda bi, ti: (bi, ti, 0, 0)),
                   pl.BlockSpec((1, 64, tt), lambda bi, ti: (bi, 0, ti))],
        out_shape=[jax.ShapeDtypeStruct((b * t, 512), F32),
                   st_full, st_full, st_full, st_full, st_win, st_win,
                   jax.ShapeDtypeStruct((b, 1024, t), BF16),
                   jax.ShapeDtypeStruct((b, N_KV, t, LANE), BF16),
                   jax.ShapeDtypeStruct((b, N_KV, t, HEAD_DIM), BF16),
                   jax.ShapeDtypeStruct((b, t // KCH, D_KV, KCH), BF16),
                   jax.ShapeDtypeStruct((b, t // KCH, D_KV, KCH), BF16),
                   jax.ShapeDtypeStruct((b, 64, t), F32)],
        input_output_aliases=aliases,
        compiler_params=_cparams(("arbitrary", "arbitrary")),
        name="prep_prompt",
    )(*args)


def _compress_math(x16, w_ref, pe_ref, kv):
    top = _dot((x16 + pe_ref[kv, 0]).astype(BF16), w_ref[kv, 0])
    bot = _dot((x16 + pe_ref[kv, 1]).astype(BF16), w_ref[kv, 1])
    bot_up = jnp.concatenate([bot[1:], jnp.zeros((1, D_KV), F32)], axis=0)
    return top + bot_up


def _compress_p_kernel(k16_ref, v16_ref, w_ref, pe_ref, c_ref, s1_ref, s2_ref, kcg_ref, vct_ref):
    kc = _compress_math(k16_ref[0], w_ref, pe_ref, 0)
    kc = _rope(kc, c_ref[...], s1_ref[...], s2_ref[...])
    vc = _compress_math(v16_ref[0], w_ref, pe_ref, 1)
    for g in range(N_KV):
        kcg_ref[0, g] = kc[:, g * HEAD_DIM:(g + 1) * HEAD_DIM].astype(BF16)
    vct_ref[0] = vc.T.astype(BF16)


def _compress_prompt(k16, v16, wb, pe, ctabs, b):
    nch = k16.shape[1]
    c, s1, s2 = ctabs
    full = lambda shape: pl.BlockSpec(shape, lambda bi: (0,) * len(shape))
    return pl.pallas_call(
        _compress_p_kernel,
        grid=(b,),
        in_specs=[pl.BlockSpec((1, nch, 4096), lambda bi: (bi, 0, 0)),
                  pl.BlockSpec((1, nch, 4096), lambda bi: (bi, 0, 0)),
                  full((2, 2, 4096, D_KV)), full((2, 2, 1, 4096)),
                  full((nch, D_KV)), full((nch, D_KV)), full((nch, D_KV))],
        out_specs=[pl.BlockSpec((1, N_KV, nch, HEAD_DIM), lambda bi: (bi, 0, 0, 0)),
                   pl.BlockSpec((1, D_KV, nch), lambda bi: (bi, 0, 0))],
        out_shape=[jax.ShapeDtypeStruct((b, N_KV, nch, HEAD_DIM), BF16),
                   jax.ShapeDtypeStruct((b, D_KV, nch), BF16)],
        compiler_params=_cparams(("arbitrary",)),
        name="compress_prompt",
    )(k16, v16, wb, pe, c, s1, s2)


def _select_blocks(score, n_blk, axis):
    idx = lax.broadcasted_iota(jnp.int32, score.shape, axis)
    rank = jnp.zeros(score.shape, F32)
    for j in range(n_blk):
        if axis == 0:
            sj = score[j:j + 1, :]
        else:
            sj = score[:, j:j + 1]
        beats = (sj > score) | ((sj == score) & (idx > j))
        rank = rank + jnp.where(beats, 1.0, 0.0)
    keep = (rank < float(N_SEL)) & (score > -jnp.inf) & (idx < n_blk)
    return jnp.where(keep, 1.0, 0.0)


def _attn_p_kernel(qt_ref, kcg_ref, vct_ref, ksa_ref, vst_ref, kwg_ref, vwt_ref, gt_ref, cov_ref, cb_ref, tri_ref,
                   o_ref, selb_ref, m_ref, l_ref, acc_ref, comb_ref, *, n_blk):
    qi = pl.program_id(2)
    t0 = qi * TQ
    n = GROUP * TQ
    qt = qt_ref[0]
    qcat = jnp.concatenate([qt[r * HEAD_DIM:(r + 1) * HEAD_DIM, :] for r in range(GROUP)], axis=1)
    gt = gt_ref[0]

    def heads(x):
        return jnp.concatenate([x] * GROUP, axis=1)

    def gate(branch):
        return jnp.concatenate([gt[branch * GROUP + r:branch * GROUP + r + 1, :] for r in range(GROUP)], axis=1)

    s = _dot(kcg_ref[0, 0], qcat) + heads(cb_ref[0])
    e = jnp.exp(s - jnp.max(s, axis=0, keepdims=True))
    qpos = t0 + jnp.bitwise_and(lax.broadcasted_iota(jnp.int32, (1, n), 1), TQ - 1)
    inv = jnp.where(qpos >= CMP_LEN - 1, 1.0, 0.0) / jnp.sum(e, axis=0, keepdims=True)
    p = (e * inv).astype(BF16)
    comb_ref[...] = gate(0) * _dot(vct_ref[0], p)
    imp4 = _dot(cov_ref[...], p)
    imp = imp4[:, 0:TQ]
    for r in range(1, GROUP):
        imp = imp + imp4[:, r * TQ:(r + 1) * TQ]

    nb = imp.shape[0]
    per = KCH // SEL_LEN
    bidx = lax.broadcasted_iota(jnp.int32, (nb, TQ), 0)
    qp = t0 + lax.broadcasted_iota(jnp.int32, (nb, TQ), 1)
    cur = jnp.right_shift(qp, SEL_SHIFT)
    eligible = bidx * SEL_LEN <= qp
    forced = (bidx == 0) | (bidx == cur) | (bidx == cur - 1)
    score = jnp.where(forced, FORCE_SCORE, jnp.where(eligible, imp, -jnp.inf))
    selb = jnp.where(_select_blocks(score, n_blk, 0) > 0.5, 0.0, NEG)
    zrows = jnp.zeros((AUG - per, TQ), F32)
    for c in range(nb // per):
        selb_ref[c] = jnp.concatenate([selb[c * per:(c + 1) * per, :], zrows], axis=0).astype(BF16)

    def reset():
        m_ref[...] = jnp.full((1, n), NEG, F32)
        l_ref[...] = jnp.zeros((1, n), F32)
        acc_ref[...] = jnp.zeros((HEAD_DIM, n), F32)

    def step(kchunk, vchunk, aug, bias):
        w = n // NSPLIT
        for part in range(NSPLIT):
            ln = slice(part * w, (part + 1) * w)
            rhs = qcat[:, ln]
            if aug is not None:
                rhs = jnp.concatenate([rhs, heads(aug)[:, ln], jnp.zeros((LANE - HEAD_DIM - AUG, w), BF16)], axis=0)
            s = _dot(kchunk, rhs)
            if bias is not None:
                s = s + heads(bias)[:, ln]
            m_old = m_ref[:, ln]
            m_new = jnp.maximum(m_old, jnp.max(s, axis=0, keepdims=True))
            alpha = jnp.exp(m_old - m_new)
            e = jnp.exp(s - m_new)
            l_ref[:, ln] = alpha * l_ref[:, ln] + jnp.sum(e, axis=0, keepdims=True)
            acc_ref[:, ln] = alpha * acc_ref[:, ln] + _dot(vchunk, e.astype(BF16))
            m_ref[:, ln] = m_new

    def flush(branch):
        comb_ref[...] = comb_ref[...] + gate(branch) * (acc_ref[...] / jnp.maximum(l_ref[...], 1e-30))

    def kslice(ref, c):
        return ref[0, 0, pl.ds(pl.multiple_of(c * KCH, KCH), KCH), :]

    reset()

    def sel_body(c, carry):
        step(kslice(ksa_ref, c), vst_ref[0, c], selb_ref[c], None)
        return carry

    lax.fori_loop(0, qi, sel_body, 0)
    step(kslice(ksa_ref, qi), vst_ref[0, qi], selb_ref[qi], tri_ref[0])
    flush(1)

    reset()
    wch = WINDOW // KCH

    def win_body(c, carry):
        which = jnp.where(c == qi, 0, jnp.where(c == qi - wch, 1, 2))
        step(kslice(kwg_ref, c), vwt_ref[0, c], None, tri_ref[which])
        return carry

    lax.fori_loop(jnp.maximum(qi - wch, 0), qi + 1, win_body, 0)
    flush(2)

    comb = comb_ref[...]
    o_ref[...] = jnp.concatenate([comb[:, r * TQ:(r + 1) * TQ].T for r in range(GROUP)], axis=1)


def _attn_prompt(qt, kcg, vct, ksa, vst, kwg, vwt, gt, cov, cb, tri, b, t, n_blk):
    nq = t // TQ
    nch = t // KCH
    ncp = kcg.shape[2]
    assert WINDOW % KCH == 0 and KCH == TQ and KCH // SEL_LEN <= AUG
    kern = functools.partial(_attn_p_kernel, n_blk=n_blk)
    n = GROUP * TQ
    return pl.pallas_call(
        kern,
        grid=(b, N_KV, nq),
        in_specs=[pl.BlockSpec((1, D_KV, TQ), lambda bi, g, qi: (bi, g, qi)),
                  pl.BlockSpec((1, 1, ncp, HEAD_DIM), lambda bi, g, qi: (bi, g, 0, 0)),
                  pl.BlockSpec((1, HEAD_DIM, ncp), lambda bi, g, qi: (bi, g, 0)),
                  pl.BlockSpec((1, 1, t, LANE), lambda bi, g, qi: (bi, g, 0, 0)),
                  pl.BlockSpec((1, nch, HEAD_DIM, KCH), lambda bi, g, qi: (bi, 0, g, 0)),
                  pl.BlockSpec((1, 1, t, HEAD_DIM), lambda bi, g, qi: (bi, g, 0, 0)),
                  pl.BlockSpec((1, nch, HEAD_DIM, KCH), lambda bi, g, qi: (bi, 0, g, 0)),
                  pl.BlockSpec((1, 16, TQ), lambda bi, g, qi: (bi, g, qi)),
                  pl.BlockSpec(cov.shape, lambda bi, g, qi: (0, 0)),
                  pl.BlockSpec((1, ncp, TQ), lambda bi, g, qi: (qi, 0, 0)),
                  pl.BlockSpec(tri.shape, lambda bi, g, qi: (0, 0, 0))],
        out_specs=pl.BlockSpec((TQ, D_KV), lambda bi, g, qi: (bi * nq + qi, g)),
        out_shape=jax.ShapeDtypeStruct((b * t, D_ATTN), F32),
        scratch_shapes=[pltpu.VMEM((cov.shape[0] * SEL_LEN // KCH, AUG, TQ), BF16),
                        pltpu.VMEM((1, n), F32), pltpu.VMEM((1, n), F32),
                        pltpu.VMEM((HEAD_DIM, n), F32), pltpu.VMEM((HEAD_DIM, n), F32)],
        compiler_params=_cparams(("arbitrary", "arbitrary", "arbitrary")),
        name="attn_prompt",
    )(qt, kcg, vct, ksa, vst, kwg, vwt, gt, cov, cb, tri)


def _attn_bias_tiles(t, ncp, n_cmp):
    k = np.arange(KCH)[:, None]
    q = np.arange(TQ)[None, :]
    tri = np.stack([np.where(k <= q, 0.0, NEG), np.where(k > q, 0.0, NEG), np.zeros((KCH, TQ))]).astype(np.float32)
    c = np.arange(ncp)[None, :, None]
    qpos = (np.arange(t // TQ)[:, None, None] * TQ) + np.arange(TQ)[None, None, :]
    cb = np.where((c * CMP_STRIDE + CMP_LEN - 1 <= qpos) & (c < n_cmp), 0.0, NEG).astype(np.float32)
    return jnp.asarray(cb), jnp.asarray(tri)


def _pool_group_mix(pooled, pw_ref, scale, pz):
    mixed = jnp.concatenate(
        [_dot(pooled[:, g * POOL_GROUP:(g + 1) * POOL_GROUP].astype(BF16), pw_ref[g])
         for g in range(len(POOL_WINDOWS))], axis=1)
    return (mixed * scale) * _silu(pz)


def _ab_p_kernel(pu_ref, pz_ref, cb_ref, cc_ref, cx_ref, cz_ref, pup_ref, ccp_ref, cxp_ref,
                 pw_ref, ps_ref, cw_ref, ya_ref, yb_ref, hct_ref, *, tm, tiles_per_seq):
    i = pl.program_id(0)
    ti = i % tiles_per_seq
    valid = jnp.where(ti > 0, 1.0, 0.0)
    pu = pu_ref[...]
    ext = jnp.concatenate([pup_ref[...] * valid, pu], axis=0)
    pos = ti * tm + lax.broadcasted_iota(jnp.int32, (tm, POOL_GROUP), 0)
    acc = pu
    means = []
    k = 1
    for gi, w in enumerate(POOL_WINDOWS):
        ch = slice(gi * POOL_GROUP, (gi + 1) * POOL_GROUP)
        while k < w:
            acc = acc + ext[16 - k:16 - k + tm, :]
            k += 1
        count = jnp.minimum(w, pos + 1).astype(F32)
        means.append(acc[:, ch] / count)
    pooled = jnp.concatenate(means, axis=1) - pu
    ya_ref[...] = _pool_group_mix(pooled, pw_ref, ps_ref[...], pz_ref[...]).astype(BF16)

    hc = cc_ref[...] * cx_ref[...]
    hprev = (ccp_ref[...] * cxp_ref[...]) * valid
    hext = jnp.concatenate([hprev, hc], axis=0)
    cw = cw_ref[...]
    conv = hext[14:14 + tm, :] * cw[0:1, :]
    conv = conv + hext[15:15 + tm, :] * cw[1:2, :]
    conv = conv + hc * cw[2:3, :]
    yb_ref[...] = ((cb_ref[...] * conv) * _silu(cz_ref[...])).astype(BF16)
    hct_ref[0] = hc[tm - 8:tm, :]


def _ab_prompt(h, pw_bf, pscale, cw, b, t):
    tm = 512
    m = b * t
    tps = t // tm
    colb = lambda c: (lambda i: (i, c // 512))
    prev = lambda c: (lambda i: (jnp.maximum(i * (tm // 16) - 1, 0), c // 512))
    kern = functools.partial(_ab_p_kernel, tm=tm, tiles_per_seq=tps)
    full = lambda shape: pl.BlockSpec(shape, lambda i: (0,) * len(shape))
    return pl.pallas_call(
        kern,
        grid=(m // tm,),
        in_specs=[pl.BlockSpec((tm, 512), colb(C_PU)), pl.BlockSpec((tm, 512), colb(C_PZ)),
                  pl.BlockSpec((tm, 512), colb(C_CB)), pl.BlockSpec((tm, 512), colb(C_CC)),
                  pl.BlockSpec((tm, 512), colb(C_CX)), pl.BlockSpec((tm, 512), colb(C_CZ)),
                  pl.BlockSpec((16, 512), prev(C_PU)), pl.BlockSpec((16, 512), prev(C_CC)),
                  pl.BlockSpec((16, 512), prev(C_CX)),
                  full((len(POOL_WINDOWS), POOL_GROUP, POOL_GROUP)), full((1, D_POOL)), full((CONV_WIDTH, D_CONV))],
        out_specs=[pl.BlockSpec((tm, D_POOL), lambda i: (i, 0)),
                   pl.BlockSpec((tm, D_CONV), lambda i: (i, 0)),
                   pl.BlockSpec((1, 8, D_CONV), lambda i: (i // tps, 0, 0))],
        out_shape=[jax.ShapeDtypeStruct((m, D_POOL), BF16),
                   jax.ShapeDtypeStruct((m, D_CONV), BF16),
                   jax.ShapeDtypeStruct((b, 8, D_CONV), F32)],
        compiler_params=_cparams(("arbitrary",)),
        name="ab_prompt",
    )(h, h, h, h, h, h, h, h, h, pw_bf, pscale, cw)


def _tail_kernel(ya_ref, yb_ref, yc_ref, az_ref, ga_ref, gb_ref, gc_ref, x_ref,
                 pa_ref, pb_ref, pc_ref, wo_ref, lg_ref, lb_ref, o_ref, obf_ref):
    yc = (yc_ref[...] * _silu(az_ref[...])).astype(BF16)
    merged = (jax.nn.sigmoid(ga_ref[...]) * _dot(ya_ref[...], pa_ref[...])
              + jax.nn.sigmoid(gb_ref[...]) * _dot(yb_ref[...], pb_ref[...])
              + jax.nn.sigmoid(gc_ref[...]) * _dot(yc, pc_ref[...]))
    y = _dot(merged.astype(BF16), wo_ref[...])
    z = ALPHA * x_ref[...] + y
    mu = jnp.mean(z, axis=-1, keepdims=True)
    var = jnp.mean(jnp.square(z - mu), axis=-1, keepdims=True)
    out = (z - mu) * lax.rsqrt(var + LN_EPS) * lg_ref[...] + lb_ref[...]
    o_ref[...] = out
    obf_ref[...] = out.astype(BF16)


def _tail(ya, yb, yc, h, x, pa, pb, pc, wo, lg, lb, tm):
    m = x.shape[0]
    full = lambda shape: pl.BlockSpec(shape, lambda i: (0,) * len(shape))
    hcol = lambda c: pl.BlockSpec((tm, 1024), lambda i: (i, c // 1024))
    return pl.pallas_call(
        _tail_kernel,
        grid=(m // tm,),
        in_specs=[pl.BlockSpec((tm, D_POOL), lambda i: (i, 0)),
                  pl.BlockSpec((tm, D_CONV), lambda i: (i, 0)),
                  pl.BlockSpec((tm, D_ATTN), lambda i: (i, 0)),
                  hcol(C_AZ), hcol(C_MG), hcol(C_MG + 1024), hcol(C_MG + 2048),
                  pl.BlockSpec((tm, D_MODEL), lambda i: (i, 0)),
                  full((D_POOL, D_MODEL)), full((D_CONV, D_MODEL)), full((D_ATTN, D_MODEL)),
                  full((D_MODEL, D_MODEL)), full((1, D_MODEL)), full((1, D_MODEL))],
        out_specs=[pl.BlockSpec((tm, D_MODEL), lambda i: (i, 0)),
                   pl.BlockSpec((tm, D_MODEL), lambda i: (i, 0))],
        out_shape=[jax.ShapeDtypeStruct((m, D_MODEL), F32),
                   jax.ShapeDtypeStruct((m, D_MODEL), BF16)],
        compiler_params=_cparams(("arbitrary",)),
        name="tail",
    )(ya, yb, yc, h, h, h, h, x, pa, pb, pc, wo, lg, lb)


def _prep_s_kernel(q_ref, kcv_ref, ksv_ref, kwv_ref, ng_ref, c_ref, s1_ref, s2_ref,
                   kvst_ref, qr_ref, g_ref):
    c, s1, s2 = c_ref[...], s1_ref[...], s2_ref[...]
    qr_ref[...] = _rope(q_ref[...], _tile_lanes(c, 8), _tile_lanes(s1, 8), _tile_lanes(s2, 8)) * SCALE
    c2, s12, s22 = _tile_lanes(c, 2), _tile_lanes(s1, 2), _tile_lanes(s2, 2)
    ksv = ksv_ref[...]
    kwv = kwv_ref[...]
    kvst_ref[:, 0:512] = kcv_ref[...]
    kvst_ref[:, 512:768] = _rope(ksv[:, :D_KV], c2, s12, s22)
    kvst_ref[:, 768:1024] = ksv[:, D_KV:]
    kvst_ref[:, 1024:1280] = _rope(kwv[:, :D_KV], c2, s12, s22)
    kvst_ref[:, 1280:1536] = kwv[:, D_KV:]
    g_ref[...] = jax.nn.sigmoid(ng_ref[...])


def _prep_sample(h, tabs):
    n = h.shape[0]
    c, s1, s2 = tabs
    tab = pl.BlockSpec((1, LANE), lambda i: (0, 0))
    return pl.pallas_call(
        _prep_s_kernel,
        grid=(1,),
        in_specs=[pl.BlockSpec((n, 1024), lambda i: (0, C_Q // 1024)),
                  pl.BlockSpec((n, 512), lambda i: (0, C_KV // 512)),
                  pl.BlockSpec((n, 512), lambda i: (0, C_KV // 512 + 1)),
                  pl.BlockSpec((n, 512), lambda i: (0, C_KV // 512 + 2)),
                  pl.BlockSpec((n, LANE), lambda i: (0, C_NG // LANE)),
                  tab, tab, tab],
        out_specs=[pl.BlockSpec((n, 1536), lambda i: (0, 0)),
                   pl.BlockSpec((n, 1024), lambda i: (0, 0)),
                   pl.BlockSpec((n, LANE), lambda i: (0, 0))],
        out_shape=[jax.ShapeDtypeStruct((n, 1536), F32),
                   jax.ShapeDtypeStruct((n, 1024), F32),
                   jax.ShapeDtypeStruct((n, LANE), F32)],
        compiler_params=_cparams(("arbitrary",)),
        name="prep_sample",
    )(h, h, h, h, h, c, s1, s2)


def _ab_s_kernel(pu_ref, pz_ref, cb_ref, cc_ref, cx_ref, cz_ref, ph_ref, ch_ref,
                 pw_ref, ps_ref, cw_ref, ya_ref, yb_ref, hc_ref, *, pos0):
    pu = pu_ref[...]
    acc = pu
    means = []
    k = 1
    for gi, w in enumerate(POOL_WINDOWS):
        ch = slice(gi * POOL_GROUP, (gi + 1) * POOL_GROUP)
        while k < w:
            acc = acc + ph_ref[POOL_HIST - k]
            k += 1
        means.append(acc[:, ch] / float(min(w, pos0 + 1)))
    pooled = jnp.concatenate(means, axis=1) - pu
    ya_ref[...] = _pool_group_mix(pooled, pw_ref, ps_ref[...], pz_ref[...]).astype(BF16)
    hc = cc_ref[...] * cx_ref[...]
    cw = cw_ref[...]
    conv = ch_ref[0] * cw[0:1, :]
    conv = conv + ch_ref[1] * cw[1:2, :]
    conv = conv + hc * cw[2:3, :]
    yb_ref[...] = ((cb_ref[...] * conv) * _silu(cz_ref[...])).astype(BF16)
    hc_ref[...] = hc


def _ab_sample(h, pool_hist_t, conv_hist_t, pw_bf, pscale, cw, pos0):
    n = h.shape[0]
    colb = lambda c: pl.BlockSpec((n, 512), lambda i: (0, c // 512))
    full = lambda shape: pl.BlockSpec(shape, lambda i: (0,) * len(shape))
    kern = functools.partial(_ab_s_kernel, pos0=pos0)
    return pl.pallas_call(
        kern,
        grid=(1,),
        in_specs=[colb(C_PU), colb(C_PZ), colb(C_CB), colb(C_CC), colb(C_CX), colb(C_CZ),
                  full((POOL_HIST, n, D_POOL)), full((CONV_WIDTH - 1, n, D_CONV)),
                  full((len(POOL_WINDOWS), POOL_GROUP, POOL_GROUP)), full((1, D_POOL)), full((CONV_WIDTH, D_CONV))],
        out_specs=[full((n, D_POOL)), full((n, D_CONV)), full((n, D_CONV))],
        out_shape=[jax.ShapeDtypeStruct((n, D_POOL), BF16),
                   jax.ShapeDtypeStruct((n, D_CONV), BF16),
                   jax.ShapeDtypeStruct((n, D_CONV), F32)],
        compiler_params=_cparams(("arbitrary",)),
        name="ab_sample",
    )(h, h, h, h, h, h, pool_hist_t, conv_hist_t, pw_bf, pscale, cw)


def _attn_s_kernel(pt_ref, *refs, n_pages, past_len, n_cmp, n_blk, wb):
    del pt_ref
    np_ = n_pages
    kc_pages = refs[0:np_]
    vc_pages = refs[np_:2 * np_]
    ks_pages = refs[2 * np_:3 * np_]
    vs_pages = refs[3 * np_:4 * np_]
    (kwin_ref, vwin_ref, qz_ref, new_ref, newc_ref, g_ref, ws_ref, pec_ref, c_ref, s1_ref, s2_ref,
     cov_ref, exp_ref, hm_ref, perm_ref, *rest) = refs[4 * np_:]
    o_ref, kwo_ref, vwo_ref, kx_ref, vx_ref = rest[-5:]
    qpos = past_len
    rows = past_len // np_
    qz = qz_ref[0]
    qzb = qz.astype(BF16)
    new = new_ref[0]
    hm = hm_ref[...]
    nch = past_len // CMP_STRIDE

    def page_t(r):
        return r[0, 0].reshape(D_KV, rows)

    perm = perm_ref[...]
    cpp = rows // CMP_STRIDE
    for p in range(np_):
        kx = _dot_nt(perm, page_t(kc_pages[p]).astype(BF16))
        vx = _dot_nt(perm, page_t(vc_pages[p]).astype(BF16))
        for s_ in range(CMP_STRIDE):
            kx_ref[s_, p * cpp:(p + 1) * cpp, :] = kx[s_ * cpp:(s_ + 1) * cpp, :]
            vx_ref[s_, p * cpp:(p + 1) * cpp, :] = vx[s_ * cpp:(s_ + 1) * cpp, :]

    def compress(x_ref, kv):
        halves = []
        for hf in range(2):
            acc = jnp.zeros((nch, 2 * LANE), F32)
            for s_ in range(CMP_STRIDE):
                acc = acc + _dot(x_ref[s_, :, hf * LANE:(hf + 1) * LANE].astype(BF16), ws_ref[kv, s_])
            bot_up = jnp.concatenate([acc[1:, LANE:], jnp.zeros((1, LANE), F32)], axis=0)
            halves.append(acc[:, :LANE] + bot_up)
        return jnp.concatenate(halves, axis=1) + pec_ref[kv]

    kc = _rope(compress(kx_ref, 0), c_ref[...], s1_ref[...], s2_ref[...])
    vc = compress(vx_ref, 1)
    ncp = kc.shape[0]
    s_c = _dot_nt(qzb, kc.astype(BF16))
    cidx = lax.broadcasted_iota(jnp.int32, (N_HEADS, ncp), 1)
    vis = (cidx * CMP_STRIDE + (CMP_LEN - 1) <= qpos) & (cidx < n_cmp)
    m_c = jnp.max(jnp.where(vis, s_c, NEG), axis=1, keepdims=True)
    e_c = jnp.where(vis, jnp.exp(s_c - m_c), 0.0)
    p_c = e_c / jnp.maximum(jnp.sum(e_c, axis=1, keepdims=True), 1e-30)
    p_cb = p_c.astype(BF16)
    o_cmp = _dot(p_cb, vc.astype(BF16))
    imp_h = _dot(p_cb, cov_ref[...])
    imp = jnp.concatenate(
        [jnp.sum(imp_h[g * GROUP:(g + 1) * GROUP, :], axis=0, keepdims=True) for g in range(N_KV)], axis=0)

    nbp = imp.shape[1]
    bidx = lax.broadcasted_iota(jnp.int32, (N_KV, nbp), 1)
    cur = qpos // SEL_LEN
    eligible = bidx * SEL_LEN <= qpos
    forced = (bidx == 0) | (bidx == cur) | (bidx == cur - 1)
    score = jnp.where(forced, FORCE_SCORE, jnp.where(eligible, imp, -jnp.inf))
    sel = _select_blocks(score, n_blk, 1)
    sel_h = jnp.concatenate(
        [jnp.broadcast_to(sel[g:g + 1, :], (GROUP, nbp)) for g in range(N_KV)], axis=0)
    kmask = _dot(sel_h.astype(BF16), exp_ref[...])

    def attend(k_t, v_t, ok, k_new, v_new):
        s = _dot(qzb, k_t)
        s_new = jnp.sum(qzb.astype(F32) * k_new.astype(BF16).astype(F32), axis=1, keepdims=True)
        m = jnp.maximum(jnp.max(jnp.where(ok, s, NEG), axis=1, keepdims=True), s_new)
        e = jnp.where(ok, jnp.exp(s - m), 0.0)
        e_new = jnp.exp(s_new - m)
        den = jnp.maximum(jnp.sum(e, axis=1, keepdims=True) + e_new, 1e-30)
        p = (e / den).astype(BF16)
        p_new = (e_new / den).astype(BF16).astype(F32)
        return _dot_nt(p, v_t) + p_new * v_new.astype(BF16).astype(F32)

    ks_t = jnp.concatenate([page_t(r).astype(BF16) for r in ks_pages], axis=1)
    vs_t = jnp.concatenate([page_t(r).astype(BF16) for r in vs_pages], axis=1)
    kpos = lax.broadcasted_iota(jnp.int32, (N_HEADS, past_len), 1)
    ok_s = (kmask > 0.5) & (kpos <= qpos)
    o_sel = attend(ks_t, vs_t, ok_s, new[0:1, :], new[1:2, :])

    kw_t = kwin_ref[0, 0].reshape(D_KV, wb)
    vw_t = vwin_ref[0, 0].reshape(D_KV, wb)
    wpos = (past_len - wb) + lax.broadcasted_iota(jnp.int32, (N_HEADS, wb), 1)
    ok_w = (wpos <= qpos) & (qpos - wpos < WINDOW)
    o_win = attend(kw_t.astype(BF16), vw_t.astype(BF16), ok_w, new[2:3, :], new[3:4, :])

    g = g_ref[0]
    o = g[:, 0:1] * o_cmp + g[:, 1:2] * o_sel + g[:, 2:3] * o_win
    o = o * hm
    o_ref[0] = (o[:, 0:64] + o[:, 64:128]) + (o[:, 128:192] + o[:, 192:256])

    cols = newc_ref[0]
    lane = lax.broadcasted_iota(jnp.int32, (D_KV, wb), 1)
    kwo = jnp.where(lane == wb - 1, cols[:, 2:3], pltpu.roll(kw_t, wb - 1, 1))
    vwo = jnp.where(lane == wb - 1, cols[:, 3:4], pltpu.roll(vw_t, wb - 1, 1))
    kwo_ref[0, 0] = kwo.reshape(N_KV, HEAD_DIM, wb)
    vwo_ref[0, 0] = vwo.reshape(N_KV, HEAD_DIM, wb)


def _attn_sample(page_table, kc_t, vc_t, ks_t, vs_t, kwin_t, vwin_t, qz, new_rows, gates, ws, pec, ctabs,
                 cov, expand, headmask, layer, past_len, n_cmp, n_blk, win_prev):
    n, n_pages = page_table.shape
    depth = kwin_t.shape[0]
    wbuf = kwin_t.shape[-1]
    rows_pp = ks_t.shape[-1]
    c, s1, s2 = ctabs
    ncp = c.shape[0]
    cpp = rows_pp // CMP_STRIDE
    i_ = np.arange(rows_pp)
    perm = jnp.asarray(i_[None, :] == ((i_ % cpp) * CMP_STRIDE + i_ // cpp)[:, None], BF16)

    def page_spec(p):
        return pl.BlockSpec((1, 1, N_KV, HEAD_DIM, rows_pp), lambda i, pt: (layer, pt[i, p], 0, 0, 0))

    full = lambda shape: pl.BlockSpec(shape, lambda i, pt: (0,) * len(shape))
    win_spec = pl.BlockSpec((1, 1, N_KV, HEAD_DIM, wbuf), lambda i, pt: (layer, i, 0, 0, 0))
    in_specs = ([page_spec(p) for p in range(n_pages)] * 4
                + [win_spec, win_spec,
                   pl.BlockSpec((1, N_HEADS, D_KV), lambda i, pt: (i, 0, 0)),
                   pl.BlockSpec((1, 8, D_KV), lambda i, pt: (i, 0, 0)),
                   pl.BlockSpec((1, D_KV, 8), lambda i, pt: (i, 0, 0)),
                   pl.BlockSpec((1, N_HEADS, LANE), lambda i, pt: (i, 0, 0)),
                   full(ws.shape), full(pec.shape),
                   full((ncp, D_KV)), full((ncp, D_KV)), full((ncp, D_KV)),
                   full(cov.shape), full(expand.shape), full(headmask.shape), full(perm.shape)])
    args = ([kc_t] * n_pages + [vc_t] * n_pages + [ks_t] * n_pages + [vs_t] * n_pages
            + [kwin_t, vwin_t, qz, new_rows, jnp.transpose(new_rows, (0, 2, 1)), gates, ws, pec, c, s1, s2,
               cov, expand, headmask, perm])
    aliases = {}
    if win_prev is not None:
        for j, a in enumerate(win_prev):
            aliases[1 + len(args)] = 1 + j
            args.append(a)
            in_specs.append(pl.BlockSpec(memory_space=pl.ANY))
    kern = functools.partial(_attn_s_kernel, n_pages=n_pages, past_len=past_len, n_cmp=n_cmp,
                             n_blk=n_blk, wb=wbuf)
    wout = pl.BlockSpec((1, 1, N_KV, HEAD_DIM, wbuf), lambda i, pt: (layer, i, 0, 0, 0))
    grid_spec = pltpu.PrefetchScalarGridSpec(
        num_scalar_prefetch=1, grid=(n,), in_specs=in_specs,
        out_specs=[pl.BlockSpec((1, N_HEADS, HEAD_DIM), lambda i, pt: (i, 0, 0)), wout, wout],
        scratch_shapes=[pltpu.VMEM((CMP_STRIDE, past_len // CMP_STRIDE, D_KV), F32),
                        pltpu.VMEM((CMP_STRIDE, past_len // CMP_STRIDE, D_KV), F32)])
    wshape = jax.ShapeDtypeStruct((depth, n, N_KV, HEAD_DIM, wbuf), F32)
    return pl.pallas_call(
        kern,
        grid_spec=grid_spec,
        out_shape=[jax.ShapeDtypeStruct((n, N_HEADS, HEAD_DIM), F32), wshape, wshape],
        input_output_aliases=aliases,
        compiler_params=_cparams(("arbitrary",)),
        name="attn_sample",
    )(page_table, *args)


def kernel(x_prompt, x_sample, cache_k_cmp, cache_v_cmp, cache_k_sel, cache_v_sel, page_table, state_k_win, state_v_win, state_pool, state_conv, w_in, pool_w, pool_scale, conv_w, cmp_pe, cmp_w, proj_a, proj_b, proj_c, w_out, ln_g, ln_b):
    b, t, _ = x_prompt.shape
    ns = x_sample.shape[0]
    assert x_sample.shape[1] == 1
    depth = w_in.shape[0]
    page = cache_k_cmp.shape[2]
    n_pages = page_table.shape[1]
    past_len = n_pages * page
    wbuf = state_k_win.shape[2]
    assert t % 512 == 0 and page % CMP_STRIDE == 0 and wbuf == WINDOW and past_len >= WINDOW

    n_cmp_p = (t - CMP_LEN) // CMP_STRIDE + 1
    n_blk_p = -(-t // SEL_LEN)
    nch_p = t // CMP_STRIDE
    tabs_p = _rope_tables(np.arange(t), LANE)
    ctabs_p = _rope_tables(np.arange(nch_p) * CMP_STRIDE + CMP_LEN - 1, D_KV)
    cov_p = jnp.asarray(_cover_t(nch_p, n_blk_p, n_cmp_p, n_blk_p), BF16)
    cb_p, tri_p = _attn_bias_tiles(t, nch_p, n_cmp_p)

    total_s = past_len + 1
    n_cmp_s = (total_s - CMP_LEN) // CMP_STRIDE + 1
    n_blk_s = -(-total_s // SEL_LEN)
    nch_s = past_len // CMP_STRIDE
    tabs_s = _rope_tables(np.asarray([past_len]), LANE)
    ctabs_s = _rope_tables(np.arange(nch_s) * CMP_STRIDE + CMP_LEN - 1, D_KV)
    cov_s = jnp.asarray(_cover_t(nch_s, LANE, n_cmp_s, n_blk_s).T, BF16)
    expand = jnp.asarray((np.arange(LANE)[:, None] == (np.arange(past_len)[None, :] // SEL_LEN)), BF16)
    headmask = jnp.asarray((np.arange(N_HEADS)[:, None] // GROUP) == (np.arange(D_KV)[None, :] // HEAD_DIM), F32)

    fm = lambda a: jnp.transpose(a, (0, 1, 3, 4, 2))
    kc_t, vc_t, ks_t, vs_t = fm(cache_k_cmp), fm(cache_v_cmp), fm(cache_k_sel), fm(cache_v_sel)
    kwin_t, vwin_t = fm(state_k_win), fm(state_v_win)

    xp = x_prompt.reshape(b * t, D_MODEL)
    xs = x_sample.reshape(ns, D_MODEL)
    xp_bf = xp.astype(BF16)
    xs_bf = xs.astype(BF16)
    st_p, st_s = [], []
    win_s = None
    st6 = None
    for l in range(depth):
        w_bf = _perm_w_in(w_in[l])
        wb_w, pe = _cmp_weights(cmp_w[l], cmp_pe[l])
        ws, pec = _cmp_weights_rows(cmp_w[l], cmp_pe[l])
        pw_bf = pool_w[l].astype(BF16)
        pscale = pool_scale[l].reshape(1, D_POOL)
        cw = conv_w[l]
        pa, pb, pc, wo = (a[l].astype(BF16) for a in (proj_a, proj_b, proj_c, w_out))
        lg = ln_g[l].reshape(1, D_MODEL)
        lb = ln_b[l].reshape(1, D_MODEL)

        h = _inproj(xp_bf, w_bf, 1024)
        kcvn, *st6, qt, ksa, kwg, vst, vwt, gt = _prep_prompt(h, tabs_p, b, t, l, depth, st6)
        k16 = kcvn[:, 0:D_KV].reshape(b, nch_p, CMP_STRIDE * D_KV)
        v16 = kcvn[:, D_KV:2 * D_KV].reshape(b, nch_p, CMP_STRIDE * D_KV)
        kcg, vct = _compress_prompt(k16, v16, wb_w, pe, ctabs_p, b)
        yc = _attn_prompt(qt, kcg, vct, ksa, vst, kwg, vwt, gt, cov_p, cb_p, tri_p, b, t, n_blk_p)
        ya, yb, hct = _ab_prompt(h, pw_bf, pscale, cw, b, t)
        xp_new, xp_bf = _tail(ya, yb, yc, h, xp, pa, pb, pc, wo, lg, lb, 512)
        st_p.append((h[:, C_PU:C_PU + D_POOL].reshape(b, t, D_POOL)[:, t - POOL_HIST:],
                     hct[:, 8 - (CONV_WIDTH - 1):]))
        xp = xp_new

        hs = _inproj(xs_bf, w_bf, ns)
        kvs, qrot, gates = _prep_sample(hs, tabs_s)
        q3 = qrot.reshape(ns, N_HEADS, 1, HEAD_DIM)
        qz = (q3 * jnp.eye(N_KV, dtype=F32)[jnp.arange(N_HEADS) // GROUP][None, :, :, None]).reshape(ns, N_HEADS, D_KV)
        new_rows = jnp.concatenate(
            [kvs[:, 2 * D_KV:6 * D_KV].reshape(ns, 4, D_KV), jnp.zeros((ns, 4, D_KV), F32)], axis=1)
        g4 = gates[:, :64].reshape(ns, N_KV, 4, GROUP)[:, :, :3, :]
        g_h = jnp.transpose(g4, (0, 1, 3, 2)).reshape(ns, N_HEADS, 3)
        g_h = jnp.concatenate([g_h, jnp.zeros((ns, N_HEADS, LANE - 3), F32)], axis=2)
        ycs, kwo, vwo = _attn_sample(page_table, kc_t, vc_t, ks_t, vs_t, kwin_t, vwin_t, qz, new_rows, g_h, ws, pec,
                                     ctabs_s, cov_s, expand, headmask, l, past_len, n_cmp_s, n_blk_s, win_s)
        win_s = (kwo, vwo)
        ycs = ycs.reshape(ns, D_ATTN)
        pool_hist = state_pool[l]
        conv_hist = state_conv[l]
        yas, ybs, hcs = _ab_sample(hs, jnp.transpose(pool_hist, (1, 0, 2)), jnp.transpose(conv_hist, (1, 0, 2)),
                                   pw_bf, pscale, cw, past_len)
        xs_new, xs_bf = _tail(yas, ybs, ycs, hs, xs, pa, pb, pc, wo, lg, lb, ns)
        kvs5 = kvs.reshape(ns, 1, 6, N_KV, HEAD_DIM)
        st_s.append((kvs5[:, :, 0], kvs5[:, :, 1], kvs5[:, :, 2], kvs5[:, :, 3],
                     jnp.concatenate([pool_hist, hs[:, None, C_PU:C_PU + D_POOL]], axis=1)[:, 1:],
                     jnp.concatenate([conv_hist, hcs[:, None, :]], axis=1)[:, 1:]))
        xs = xs_new

    stack = lambda states, i: jnp.stack([s[i] for s in states], axis=0)
    rm = lambda a: jnp.transpose(a.reshape(a.shape[0], a.shape[1], N_KV, HEAD_DIM, a.shape[3]), (0, 1, 4, 2, 3))
    rm5 = lambda a: jnp.transpose(a, (0, 1, 4, 2, 3))
    return ((xp.reshape(b, t, D_MODEL), xs.reshape(ns, 1, D_MODEL))
            + tuple(rm(a) for a in st6) + (stack(st_p, 0), stack(st_p, 1))
            + tuple(stack(st_s, i) for i in range(4)) + (rm5(win_s[0]), rm5(win_s[1]))
            + (stack(st_s, 4), stack(st_s, 5)))
```

```python
import functools

import numpy as np
import jax
import jax.numpy as jnp
from jax import lax
from jax.experimental import pallas as pl
from jax.experimental.pallas import tpu as pltpu

D_MODEL = 1024
DEPTH = 2
D_POOL = 512
POOL_WINDOWS = (2, 4, 8, 16)
POOL_GROUP = D_POOL // len(POOL_WINDOWS)
POOL_HIST = max(POOL_WINDOWS) - 1
D_CONV = 512
CONV_WIDTH = 3
HEAD_DIM = 64
N_HEADS = 16
N_KV = 4
GROUP = N_HEADS // N_KV
D_ATTN = N_HEADS * HEAD_DIM
D_KV = N_KV * HEAD_DIM
ROT_DIM = HEAD_DIM // 4
ROPE_THETA = 500000.0
CMP_LEN = 32
CMP_STRIDE = 16
SEL_LEN = 64
SEL_SHIFT = 6
N_SEL = 8
WINDOW = 512
FORCE_SCORE = 1.0e4
LN_EPS = 1e-5
ALPHA = (2 * DEPTH) ** 0.25
SCALE = HEAD_DIM ** -0.5

C_PU, C_PZ, C_CB, C_CC, C_CX, C_CZ = 0, 512, 1024, 1536, 2048, 2560
C_Q = 3072
C_AZ = 4096
C_MG = 5120
C_KV = 8192
C_NG = 9728
D_H = 9984
NG_PAD = 128

NEG = -1.0e30
LANE = 128
TQ = 512
KCH = 512
NSPLIT = 1
AUG = 16
VMEM_LIMIT = 48 * 1024 * 1024

BF16 = jnp.bfloat16
F32 = jnp.float32


def _cparams(sem):
    return pltpu.CompilerParams(dimension_semantics=sem, vmem_limit_bytes=VMEM_LIMIT)


def _dot(a, b):
    return jnp.dot(a, b, preferred_element_type=F32)


def _dot_nt(a, b):
    return lax.dot_general(a, b, (((1,), (1,)), ((), ())), preferred_element_type=F32)


def _silu(x):
    return x * jax.nn.sigmoid(x)


def _perm_w_in(w):
    ab = w[:, 0:3072]
    q = w[:, 3072:4096]
    kv = w[:, 4096:5632]
    ng = w[:, 5632:5680]
    az = w[:, 5680:6704]
    mg = w[:, 6704:9776]
    idx = np.full((NG_PAD,), 48, np.int32)
    for g in range(N_KV):
        for br in range(3):
            for r in range(GROUP):
                idx[g * 16 + br * 4 + r] = (GROUP * g + r) * 3 + br
    ng_ext = jnp.concatenate([ng, jnp.zeros((w.shape[0], 1), w.dtype)], axis=1)
    ng_p = jnp.take(ng_ext, jnp.asarray(idx), axis=1)
    pad = jnp.zeros((w.shape[0], D_H - C_NG - NG_PAD), w.dtype)
    return jnp.concatenate([ab, q, az, mg, kv, ng_p, pad], axis=1).astype(BF16)


def _rope_tables(pos, width):
    pos = np.asarray(pos, np.float32)
    p = pos.shape[0]
    inv_freq = (1.0 / (ROPE_THETA ** (np.arange(0, ROT_DIM, 2, dtype=np.float32) / ROT_DIM))).astype(np.float32)
    ang = (pos[:, None] * inv_freq[None, :]).astype(np.float32).astype(np.float64)
    cos = np.cos(ang).astype(np.float32)
    sin = np.sin(ang).astype(np.float32)
    z8 = np.zeros((p, 8), np.float32)
    z48 = np.zeros((p, 48), np.float32)
    c64 = np.concatenate([cos, cos, np.ones((p, 48), np.float32)], axis=1)
    s1 = np.concatenate([-sin, z8, z48], axis=1)
    s2 = np.concatenate([z8, sin, z48], axis=1)
    rep = width // HEAD_DIM
    return tuple(jnp.asarray(np.tile(t, (1, rep))) for t in (c64, s1, s2))


def _rope(x, c, s1, s2):
    w = x.shape[-1]
    up = pltpu.roll(x, w - 8, 1)
    dn = pltpu.roll(x, 8, 1)
    return x * c + up * s1 + dn * s2


def _tile_lanes(t, rep):
    return t if rep == 1 else jnp.concatenate([t] * rep, axis=1)


def _cmp_weights_rows(cmp_w_l, cmp_pe_l):
    eye2 = jnp.eye(2, dtype=F32)
    w = cmp_w_l.reshape(2, 2, CMP_STRIDE, HEAD_DIM, HEAD_DIM)
    ws = jnp.einsum('ktshe,gf->ksghtfe', w, eye2).reshape(2, CMP_STRIDE, 2 * HEAD_DIM, 4 * HEAD_DIM)
    pec = jnp.einsum('kph,kphe->ke', cmp_pe_l, cmp_w_l.reshape(2, CMP_LEN, HEAD_DIM, HEAD_DIM),
                     precision=lax.Precision.HIGHEST)
    pec = jnp.tile(pec, (1, N_KV)).reshape(2, 1, D_KV)
    return ws.astype(BF16), pec


def _cover_t(n_cmp_pad, n_blk_pad, n_cmp, n_blk):
    c0 = np.arange(n_cmp_pad)[None, :] * CMP_STRIDE
    s0 = np.arange(n_blk_pad)[:, None] * SEL_LEN
    m = (c0 < s0 + SEL_LEN) & (c0 + CMP_LEN > s0)
    m &= (np.arange(n_cmp_pad)[None, :] < n_cmp) & (np.arange(n_blk_pad)[:, None] < n_blk)
    return m.astype(np.float32)


def _mm_kernel(x_ref, w_ref, o_ref):
    o_ref[...] = _dot(x_ref[...], w_ref[...])


def _inproj(x_bf, w_bf, tm):
    m = x_bf.shape[0]
    tn = 768
    return pl.pallas_call(
        _mm_kernel,
        grid=(m // tm, D_H // tn),
        in_specs=[pl.BlockSpec((tm, D_MODEL), lambda i, j: (i, 0)),
                  pl.BlockSpec((D_MODEL, tn), lambda i, j: (0, j))],
        out_specs=pl.BlockSpec((tm, tn), lambda i, j: (i, j)),
        out_shape=jax.ShapeDtypeStruct((m, D_H), F32),
        compiler_params=_cparams(("arbitrary", "arbitrary")),
        name="inproj",
    )(x_bf, w_bf)


def _prep_p_kernel(q_ref, kcv_ref, ksv_ref, kwv_ref, ng_ref, c_ref, s1_ref, s2_ref,
                   *rest):
    (kct_ref, vct_ref, kstt_ref, vstt_ref, kwtt_ref, vwtt_ref,
     qt_ref, ksa_ref, kwg_ref, vst_ref, vwt_ref, gt_ref) = rest[-12:]
    tt = q_ref.shape[0]
    c, s1, s2 = c_ref[...], s1_ref[...], s2_ref[...]
    q = _rope(q_ref[...], _tile_lanes(c, 8), _tile_lanes(s1, 8), _tile_lanes(s2, 8)) * SCALE
    qt_ref[0] = q.T.astype(BF16)
    c2, s12, s22 = _tile_lanes(c, 2), _tile_lanes(s1, 2), _tile_lanes(s2, 2)
    kcv = kcv_ref[...]
    ksv = ksv_ref[...]
    kwv = kwv_ref[...]
    ks = _rope(ksv[:, :D_KV], c2, s12, s22)
    kw = _rope(kwv[:, :D_KV], c2, s12, s22)
    kcvt = kcv.T
    kct_ref[0, 0] = kcvt[:D_KV]
    vct_ref[0, 0] = kcvt[D_KV:]
    kst = ks.T
    vst = ksv[:, D_KV:].T
    kwt = kw.T
    vwt = kwv[:, D_KV:].T
    kstt_ref[0, 0] = kst
    vstt_ref[0, 0] = vst
    kwtt_ref[0, 0] = kwt
    vwtt_ref[0, 0] = vwt
    for j in range(tt // KCH):
        vst_ref[0, j] = vst[:, j * KCH:(j + 1) * KCH].astype(BF16)
        vwt_ref[0, j] = vwt[:, j * KCH:(j + 1) * KCH].astype(BF16)
    row = lax.broadcasted_iota(jnp.int32, (tt, LANE), 0)
    lane = lax.broadcasted_iota(jnp.int32, (tt, LANE), 1)
    blk = jnp.right_shift(jnp.bitwise_and(row, KCH - 1), SEL_SHIFT)
    onehot = jnp.where((lane >= HEAD_DIM) & (lane - HEAD_DIM == blk), 1.0, 0.0)
    zpad = jnp.zeros((tt, LANE - HEAD_DIM), F32)
    for g in range(N_KV):
        kg = ks[:, g * HEAD_DIM:(g + 1) * HEAD_DIM]
        ksa_ref[0, g] = (jnp.concatenate([kg, zpad], axis=1) + onehot).astype(BF16)
        kwg_ref[0, g] = kw[:, g * HEAD_DIM:(g + 1) * HEAD_DIM].astype(BF16)
    gt_ref[0] = jax.nn.sigmoid(ng_ref[...]).T[:64, :]


def _prep_prompt(h, tabs, b, t, layer, depth, prev):
    tt = 512
    assert min(WINDOW, t) == tt
    nt = t // tt
    row = lambda bi, ti: bi * nt + ti
    c, s1, s2 = tabs
    tab_spec = pl.BlockSpec((tt, LANE), lambda bi, ti: (ti, 0))
    in_specs = [pl.BlockSpec((tt, 1024), lambda bi, ti: (row(bi, ti), C_Q // 1024)),
                pl.BlockSpec((tt, 512), lambda bi, ti: (row(bi, ti), C_KV // 512)),
                pl.BlockSpec((tt, 512), lambda bi, ti: (row(bi, ti), C_KV // 512 + 1)),
                pl.BlockSpec((tt, 512), lambda bi, ti: (row(bi, ti), C_KV // 512 + 2)),
                pl.BlockSpec((tt, LANE), lambda bi, ti: (row(bi, ti), C_NG // LANE)),
                tab_spec, tab_spec, tab_spec]
    args = [h, h, h, h, h, c, s1, s2]
    aliases = {}
    if prev is not None:
        for j, a in enumerate(prev):
            aliases[len(args)] = j
            args.append(a)
            in_specs.append(pl.BlockSpec(memory_space=pl.ANY))
    full_t = pl.BlockSpec((1, 1, D_KV, tt), lambda bi, ti: (layer, bi, 0, ti))
    win_t = pl.BlockSpec((1, 1, D_KV, tt), lambda bi, ti: (layer, bi, 0, 0))
    st_full = jax.ShapeDtypeStruct((depth, b, D_KV, t), F32)
    st_win = jax.ShapeDtypeStruct((depth, b, D_KV, tt), F32)
    return pl.pallas_call(
        _prep_p_kernel,
        grid=(b, nt),
        in_specs=in_specs,
        out_specs=[full_t, full_t, full_t, full_t, win_t, win_t,
                   pl.BlockSpec((1, 1024, tt), lambda bi, ti: (bi, 0, ti)),
                   pl.BlockSpec((1, N_KV, tt, LANE), lambda bi, ti: (bi, 0, ti, 0)),
                   pl.BlockSpec((1, N_KV, tt, HEAD_DIM), lambda bi, ti: (bi, 0, ti, 0)),
                   pl.BlockSpec((1, tt // KCH, D_KV, KCH), lambda bi, ti: (bi, ti, 0, 0)),
                   pl.BlockSpec((1, tt // KCH, D_KV, KCH), lambda bi, ti: (bi, ti, 0, 0)),
                   pl.BlockSpec((1, 64, tt), lambda bi, ti: (bi, 0, ti))],
        out_shape=[st_full, st_full, st_full, st_full, st_win, st_win,
                   jax.ShapeDtypeStruct((b, 1024, t), BF16),
                   jax.ShapeDtypeStruct((b, N_KV, t, LANE), BF16),
                   jax.ShapeDtypeStruct((b, N_KV, t, HEAD_DIM), BF16),
                   jax.ShapeDtypeStruct((b, t // KCH, D_KV, KCH), BF16),
                   jax.ShapeDtypeStruct((b, t // KCH, D_KV, KCH), BF16),
                   jax.ShapeDtypeStruct((b, 64, t), F32)],
        input_output_aliases=aliases,
        compiler_params=_cparams(("arbitrary", "arbitrary")),
        name="prep_prompt",
    )(*args)


def _gather_rows(perm, page_t, x_ref, p, cpp):
    x = _dot_nt(perm, page_t.astype(BF16))
    for s_ in range(CMP_STRIDE):
        x_ref[s_, p * cpp:(p + 1) * cpp, :] = x[s_ * cpp:(s_ + 1) * cpp, :]


def _compress_rows(x_ref, ws_ref, pec_ref, kv, nch):
    halves = []
    for hf in range(2):
        acc = jnp.zeros((nch, 2 * LANE), F32)
        for s_ in range(CMP_STRIDE):
            acc = acc + _dot(x_ref[s_, :, hf * LANE:(hf + 1) * LANE].astype(BF16), ws_ref[kv, s_])
        bot_up = jnp.concatenate([acc[1:, LANE:], jnp.zeros((1, LANE), F32)], axis=0)
        halves.append(acc[:, :LANE] + bot_up)
    return jnp.concatenate(halves, axis=1) + pec_ref[kv]


def _row_perm(rows):
    cpp = rows // CMP_STRIDE
    i_ = np.arange(rows)
    return jnp.asarray(i_[None, :] == ((i_ % cpp) * CMP_STRIDE + i_ // cpp)[:, None], BF16)


def _compress_p_kernel(kt_ref, vt_ref, ws_ref, pec_ref, perm_ref, c_ref, s1_ref, s2_ref, kcg_ref, vct_ref,
                       kx_ref, vx_ref, *, rows):
    t = kt_ref.shape[-1]
    nch = t // CMP_STRIDE
    cpp = rows // CMP_STRIDE
    perm = perm_ref[...]
    for p in range(t // rows):
        _gather_rows(perm, kt_ref[0, 0, :, p * rows:(p + 1) * rows], kx_ref, p, cpp)
        _gather_rows(perm, vt_ref[0, 0, :, p * rows:(p + 1) * rows], vx_ref, p, cpp)
    kc = _rope(_compress_rows(kx_ref, ws_ref, pec_ref, 0, nch), c_ref[...], s1_ref[...], s2_ref[...])
    vc = _compress_rows(vx_ref, ws_ref, pec_ref, 1, nch)
    for g in range(N_KV):
        kcg_ref[0, g] = kc[:, g * HEAD_DIM:(g + 1) * HEAD_DIM].astype(BF16)
    vct_ref[0] = vc.T.astype(BF16)


def _compress_prompt(kct, vct_, ws, pec, ctabs, layer, b):
    t = kct.shape[-1]
    nch = t // CMP_STRIDE
    rows = LANE
    perm = _row_perm(rows)
    c, s1, s2 = ctabs
    full = lambda shape: pl.BlockSpec(shape, lambda bi: (0,) * len(shape))
    seq = pl.BlockSpec((1, 1, D_KV, t), lambda bi: (layer, bi, 0, 0))
    return pl.pallas_call(
        functools.partial(_compress_p_kernel, rows=rows),
        grid=(b,),
        in_specs=[seq, seq, full(ws.shape), full(pec.shape), full(perm.shape),
                  full((nch, D_KV)), full((nch, D_KV)), full((nch, D_KV))],
        out_specs=[pl.BlockSpec((1, N_KV, nch, HEAD_DIM), lambda bi: (bi, 0, 0, 0)),
                   pl.BlockSpec((1, D_KV, nch), lambda bi: (bi, 0, 0))],
        out_shape=[jax.ShapeDtypeStruct((b, N_KV, nch, HEAD_DIM), BF16),
                   jax.ShapeDtypeStruct((b, D_KV, nch), BF16)],
        scratch_shapes=[pltpu.VMEM((CMP_STRIDE, nch, D_KV), F32), pltpu.VMEM((CMP_STRIDE, nch, D_KV), F32)],
        compiler_params=_cparams(("arbitrary",)),
        name="compress_prompt",
    )(kct, vct_, ws, pec, perm, c, s1, s2)


def _select_blocks(score, n_blk, axis):
    idx = lax.broadcasted_iota(jnp.int32, score.shape, axis)
    rank = jnp.zeros(score.shape, F32)
    for j in range(n_blk):
        if axis == 0:
            sj = score[j:j + 1, :]
        else:
            sj = score[:, j:j + 1]
        beats = (sj > score) | ((sj == score) & (idx > j))
        rank = rank + jnp.where(beats, 1.0, 0.0)
    keep = (rank < float(N_SEL)) & (score > -jnp.inf) & (idx < n_blk)
    return jnp.where(keep, 1.0, 0.0)


def _attn_p_kernel(qt_ref, kcg_ref, vct_ref, ksa_ref, vst_ref, kwg_ref, vwt_ref, gt_ref, cov_ref, cb_ref, tri_ref,
                   o_ref, selb_ref, m_ref, l_ref, acc_ref, comb_ref, sa_ref, sb_ref, *, n_blk):
    qi = pl.program_id(2)
    t0 = qi * TQ
    n = GROUP * TQ
    qt = qt_ref[0]
    qcat = jnp.concatenate([qt[r * HEAD_DIM:(r + 1) * HEAD_DIM, :] for r in range(GROUP)], axis=1)
    gt = gt_ref[0]

    def heads(x):
        return jnp.concatenate([x] * GROUP, axis=1)

    def gate(branch):
        return jnp.concatenate([gt[branch * GROUP + r:branch * GROUP + r + 1, :] for r in range(GROUP)], axis=1)

    def kchunk(k_ref, c):
        return k_ref[0, 0, pl.ds(pl.multiple_of(c * KCH, KCH), KCH), :]

    s = _dot(kcg_ref[0, 0], qcat) + heads(cb_ref[0])
    e = jnp.exp(s - jnp.max(s, axis=0, keepdims=True))
    qpos = t0 + jnp.bitwise_and(lax.broadcasted_iota(jnp.int32, (1, n), 1), TQ - 1)
    inv = jnp.where(qpos >= CMP_LEN - 1, 1.0, 0.0) / jnp.sum(e, axis=0, keepdims=True)
    p = (e * inv).astype(BF16)
    comb_ref[...] = gate(0) * _dot(vct_ref[0], p)
    imp4 = _dot(cov_ref[...], p)
    imp = imp4[:, 0:TQ]
    for r in range(1, GROUP):
        imp = imp + imp4[:, r * TQ:(r + 1) * TQ]

    nb = imp.shape[0]
    per = KCH // SEL_LEN
    bidx = lax.broadcasted_iota(jnp.int32, (nb, TQ), 0)
    qp = t0 + lax.broadcasted_iota(jnp.int32, (nb, TQ), 1)
    cur = jnp.right_shift(qp, SEL_SHIFT)
    eligible = bidx * SEL_LEN <= qp
    forced = (bidx == 0) | (bidx == cur) | (bidx == cur - 1)
    score = jnp.where(forced, FORCE_SCORE, jnp.where(eligible, imp, -jnp.inf))
    selb = jnp.where(_select_blocks(score, n_blk, 0) > 0.5, 0.0, NEG)
    zrows = jnp.zeros((AUG - per, TQ), F32)
    for c in range(nb // per):
        selb_ref[c] = jnp.concatenate([selb[c * per:(c + 1) * per, :], zrows], axis=0).astype(BF16)

    def reset():
        m_ref[...] = jnp.full((1, n), NEG, F32)
        l_ref[...] = jnp.zeros((1, n), F32)
        acc_ref[...] = jnp.zeros((HEAD_DIM, n), F32)

    def scores(k_ref, c, aug):
        rhs = qcat
        if aug:
            rhs = jnp.concatenate([qcat, heads(selb_ref[c]), jnp.zeros((LANE - HEAD_DIM - AUG, n), BF16)], axis=0)
        return _dot(kchunk(k_ref, c), rhs)

    def update(s, vchunk):
        m_old = m_ref[...]
        m_new = jnp.maximum(m_old, jnp.max(s, axis=0, keepdims=True))
        alpha = jnp.exp(m_old - m_new)
        e = jnp.exp(s - m_new)
        l_ref[...] = alpha * l_ref[...] + jnp.sum(e, axis=0, keepdims=True)
        acc_ref[...] = alpha * acc_ref[...] + _dot(vchunk, e.astype(BF16))
        m_ref[...] = m_new

    def flush(branch):
        comb_ref[...] = comb_ref[...] + gate(branch) * (acc_ref[...] / jnp.maximum(l_ref[...], 1e-30))

    causal = lambda: heads(tri_ref[0])
    lowcut = lambda: heads(tri_ref[1])

    reset()
    sa_ref[...] = scores(ksa_ref, 0, True)

    def pair(i, carry):
        c = 2 * i
        sb_ref[...] = scores(ksa_ref, c + 1, True)
        update(sa_ref[...], vst_ref[0, c])
        sa_ref[...] = scores(ksa_ref, c + 2, True)
        update(sb_ref[...], vst_ref[0, c + 1])
        return carry

    lax.fori_loop(0, qi // 2, pair, 0)

    @pl.when(qi % 2 == 1)
    def _():
        sb_ref[...] = scores(ksa_ref, qi, True)
        update(sa_ref[...], vst_ref[0, qi - 1])
        update(sb_ref[...] + causal(), vst_ref[0, qi])

    @pl.when(qi % 2 == 0)
    def _():
        update(sa_ref[...] + causal(), vst_ref[0, qi])

    flush(1)

    reset()

    @pl.when(qi >= 1)
    def _():
        sa_ref[...] = scores(kwg_ref, qi - 1, False)
        sb_ref[...] = scores(kwg_ref, qi, False)
        update(sa_ref[...] + lowcut(), vwt_ref[0, qi - 1])
        update(sb_ref[...] + causal(), vwt_ref[0, qi])

    @pl.when(qi == 0)
    def _():
        update(scores(kwg_ref, 0, False) + causal(), vwt_ref[0, 0])

    flush(2)

    comb = comb_ref[...]
    o_ref[...] = jnp.concatenate([comb[:, r * TQ:(r + 1) * TQ].T for r in range(GROUP)], axis=1)


def _attn_prompt(qt, kcg, vct, ksa, vst, kwg, vwt, gt, cov, cb, tri, b, t, n_blk):
    nq = t // TQ
    nch = t // KCH
    ncp = kcg.shape[2]
    assert WINDOW == KCH and KCH == TQ and KCH // SEL_LEN <= AUG
    kern = functools.partial(_attn_p_kernel, n_blk=n_blk)
    n = GROUP * TQ
    return pl.pallas_call(
        kern,
        grid=(b, N_KV, nq),
        in_specs=[pl.BlockSpec((1, D_KV, TQ), lambda bi, g, qi: (bi, g, qi)),
                  pl.BlockSpec((1, 1, ncp, HEAD_DIM), lambda bi, g, qi: (bi, g, 0, 0)),
                  pl.BlockSpec((1, HEAD_DIM, ncp), lambda bi, g, qi: (bi, g, 0)),
                  pl.BlockSpec((1, 1, t, LANE), lambda bi, g, qi: (bi, g, 0, 0)),
                  pl.BlockSpec((1, nch, HEAD_DIM, KCH), lambda bi, g, qi: (bi, 0, g, 0)),
                  pl.BlockSpec((1, 1, t, HEAD_DIM), lambda bi, g, qi: (bi, g, 0, 0)),
                  pl.BlockSpec((1, nch, HEAD_DIM, KCH), lambda bi, g, qi: (bi, 0, g, 0)),
                  pl.BlockSpec((1, 16, TQ), lambda bi, g, qi: (bi, g, qi)),
                  pl.BlockSpec(cov.shape, lambda bi, g, qi: (0, 0)),
                  pl.BlockSpec((1, ncp, TQ), lambda bi, g, qi: (qi, 0, 0)),
                  pl.BlockSpec(tri.shape, lambda bi, g, qi: (0, 0, 0))],
        out_specs=pl.BlockSpec((TQ, D_KV), lambda bi, g, qi: (bi * nq + qi, g)),
        out_shape=jax.ShapeDtypeStruct((b * t, D_ATTN), F32),
        scratch_shapes=[pltpu.VMEM((cov.shape[0] * SEL_LEN // KCH, AUG, TQ), BF16),
                        pltpu.VMEM((1, n), F32), pltpu.VMEM((1, n), F32),
                        pltpu.VMEM((HEAD_DIM, n), F32), pltpu.VMEM((HEAD_DIM, n), F32),
                        pltpu.VMEM((KCH, n), F32), pltpu.VMEM((KCH, n), F32)],
        compiler_params=_cparams(("arbitrary", "arbitrary", "arbitrary")),
        name="attn_prompt",
    )(qt, kcg, vct, ksa, vst, kwg, vwt, gt, cov, cb, tri)


def _attn_bias_tiles(t, ncp, n_cmp):
    k = np.arange(KCH)[:, None]
    q = np.arange(TQ)[None, :]
    tri = np.stack([np.where(k <= q, 0.0, NEG), np.where(k > q, 0.0, NEG), np.zeros((KCH, TQ))]).astype(np.float32)
    c = np.arange(ncp)[None, :, None]
    qpos = (np.arange(t // TQ)[:, None, None] * TQ) + np.arange(TQ)[None, None, :]
    cb = np.where((c * CMP_STRIDE + CMP_LEN - 1 <= qpos) & (c < n_cmp), 0.0, NEG).astype(np.float32)
    return jnp.asarray(cb), jnp.asarray(tri)


def _pool_group_mix(pooled, pw_ref, scale, pz):
    mixed = jnp.concatenate(
        [_dot(pooled[:, g * POOL_GROUP:(g + 1) * POOL_GROUP].astype(BF16), pw_ref[g])
         for g in range(len(POOL_WINDOWS))], axis=1)
    return (mixed * scale) * _silu(pz)


def _ab_p_kernel(pu_ref, pz_ref, cb_ref, cc_ref, cx_ref, cz_ref, pup_ref, ccp_ref, cxp_ref,
                 pw_ref, ps_ref, cw_ref, ya_ref, yb_ref, hct_ref, put_ref, *, tm, tiles_per_seq):
    i = pl.program_id(0)
    ti = i % tiles_per_seq
    valid = jnp.where(ti > 0, 1.0, 0.0)
    pu = pu_ref[...]
    ext = jnp.concatenate([pup_ref[...] * valid, pu], axis=0)
    pos = ti * tm + lax.broadcasted_iota(jnp.int32, (tm, POOL_GROUP), 0)
    acc = pu
    means = []
    k = 1
    for gi, w in enumerate(POOL_WINDOWS):
        ch = slice(gi * POOL_GROUP, (gi + 1) * POOL_GROUP)
        while k < w:
            acc = acc + ext[16 - k:16 - k + tm, :]
            k += 1
        count = jnp.minimum(w, pos + 1).astype(F32)
        means.append(acc[:, ch] / count)
    pooled = jnp.concatenate(means, axis=1) - pu
    ya_ref[...] = _pool_group_mix(pooled, pw_ref, ps_ref[...], pz_ref[...]).astype(BF16)

    hc = cc_ref[...] * cx_ref[...]
    hprev = (ccp_ref[...] * cxp_ref[...]) * valid
    hext = jnp.concatenate([hprev, hc], axis=0)
    cw = cw_ref[...]
    conv = hext[14:14 + tm, :] * cw[0:1, :]
    conv = conv + hext[15:15 + tm, :] * cw[1:2, :]
    conv = conv + hc * cw[2:3, :]
    yb_ref[...] = ((cb_ref[...] * conv) * _silu(cz_ref[...])).astype(BF16)
    hct_ref[0] = hc[tm - 8:tm, :]
    put_ref[0] = pu[tm - 16:tm, :]


def _ab_prompt(h, pw_bf, pscale, cw, b, t):
    tm = 512
    m = b * t
    tps = t // tm
    colb = lambda c: (lambda i: (i, c // 512))
    prev = lambda c: (lambda i: (jnp.maximum(i * (tm // 16) - 1, 0), c // 512))
    kern = functools.partial(_ab_p_kernel, tm=tm, tiles_per_seq=tps)
    full = lambda shape: pl.BlockSpec(shape, lambda i: (0,) * len(shape))
    return pl.pallas_call(
        kern,
        grid=(m // tm,),
        in_specs=[pl.BlockSpec((tm, 512), colb(C_PU)), pl.BlockSpec((tm, 512), colb(C_PZ)),
                  pl.BlockSpec((tm, 512), colb(C_CB)), pl.BlockSpec((tm, 512), colb(C_CC)),
                  pl.BlockSpec((tm, 512), colb(C_CX)), pl.BlockSpec((tm, 512), colb(C_CZ)),
                  pl.BlockSpec((16, 512), prev(C_PU)), pl.BlockSpec((16, 512), prev(C_CC)),
                  pl.BlockSpec((16, 512), prev(C_CX)),
                  full((len(POOL_WINDOWS), POOL_GROUP, POOL_GROUP)), full((1, D_POOL)), full((CONV_WIDTH, D_CONV))],
        out_specs=[pl.BlockSpec((tm, D_POOL), lambda i: (i, 0)),
                   pl.BlockSpec((tm, D_CONV), lambda i: (i, 0)),
                   pl.BlockSpec((1, 8, D_CONV), lambda i: (i // tps, 0, 0)),
                   pl.BlockSpec((1, 16, D_POOL), lambda i: (i // tps, 0, 0))],
        out_shape=[jax.ShapeDtypeStruct((m, D_POOL), BF16),
                   jax.ShapeDtypeStruct((m, D_CONV), BF16),
                   jax.ShapeDtypeStruct((b, 8, D_CONV), F32),
                   jax.ShapeDtypeStruct((b, 16, D_POOL), F32)],
        compiler_params=_cparams(("arbitrary",)),
        name="ab_prompt",
    )(h, h, h, h, h, h, h, h, h, pw_bf, pscale, cw)


def _tail_kernel(ya_ref, yb_ref, yc_ref, az_ref, ga_ref, gb_ref, gc_ref, x_ref,
                 pa_ref, pb_ref, pc_ref, wo_ref, lg_ref, lb_ref, o_ref, obf_ref):
    yc = (yc_ref[...] * _silu(az_ref[...])).astype(BF16)
    merged = (jax.nn.sigmoid(ga_ref[...]) * _dot(ya_ref[...], pa_ref[...])
              + jax.nn.sigmoid(gb_ref[...]) * _dot(yb_ref[...], pb_ref[...])
              + jax.nn.sigmoid(gc_ref[...]) * _dot(yc, pc_ref[...]))
    y = _dot(merged.astype(BF16), wo_ref[...])
    z = ALPHA * x_ref[...] + y
    mu = jnp.mean(z, axis=-1, keepdims=True)
    var = jnp.mean(jnp.square(z - mu), axis=-1, keepdims=True)
    out = (z - mu) * lax.rsqrt(var + LN_EPS) * lg_ref[...] + lb_ref[...]
    o_ref[...] = out
    obf_ref[...] = out.astype(BF16)


def _tail(ya, yb, yc, h, x, pa, pb, pc, wo, lg, lb, tm):
    m = x.shape[0]
    full = lambda shape: pl.BlockSpec(shape, lambda i: (0,) * len(shape))
    hcol = lambda c: pl.BlockSpec((tm, 1024), lambda i: (i, c // 1024))
    return pl.pallas_call(
        _tail_kernel,
        grid=(m // tm,),
        in_specs=[pl.BlockSpec((tm, D_POOL), lambda i: (i, 0)),
                  pl.BlockSpec((tm, D_CONV), lambda i: (i, 0)),
                  pl.BlockSpec((tm, D_ATTN), lambda i: (i, 0)),
                  hcol(C_AZ), hcol(C_MG), hcol(C_MG + 1024), hcol(C_MG + 2048),
                  pl.BlockSpec((tm, D_MODEL), lambda i: (i, 0)),
                  full((D_POOL, D_MODEL)), full((D_CONV, D_MODEL)), full((D_ATTN, D_MODEL)),
                  full((D_MODEL, D_MODEL)), full((1, D_MODEL)), full((1, D_MODEL))],
        out_specs=[pl.BlockSpec((tm, D_MODEL), lambda i: (i, 0)),
                   pl.BlockSpec((tm, D_MODEL), lambda i: (i, 0))],
        out_shape=[jax.ShapeDtypeStruct((m, D_MODEL), F32),
                   jax.ShapeDtypeStruct((m, D_MODEL), BF16)],
        compiler_params=_cparams(("arbitrary",)),
        name="tail",
    )(ya, yb, yc, h, h, h, h, x, pa, pb, pc, wo, lg, lb)


def _prep_s_kernel(q_ref, kcv_ref, ksv_ref, kwv_ref, ng_ref, c_ref, s1_ref, s2_ref,
                   kvst_ref, qr_ref, g_ref):
    c, s1, s2 = c_ref[...], s1_ref[...], s2_ref[...]
    qr_ref[...] = _rope(q_ref[...], _tile_lanes(c, 8), _tile_lanes(s1, 8), _tile_lanes(s2, 8)) * SCALE
    c2, s12, s22 = _tile_lanes(c, 2), _tile_lanes(s1, 2), _tile_lanes(s2, 2)
    ksv = ksv_ref[...]
    kwv = kwv_ref[...]
    kvst_ref[:, 0:512] = kcv_ref[...]
    kvst_ref[:, 512:768] = _rope(ksv[:, :D_KV], c2, s12, s22)
    kvst_ref[:, 768:1024] = ksv[:, D_KV:]
    kvst_ref[:, 1024:1280] = _rope(kwv[:, :D_KV], c2, s12, s22)
    kvst_ref[:, 1280:1536] = kwv[:, D_KV:]
    g_ref[...] = jax.nn.sigmoid(ng_ref[...])


def _prep_sample(h, tabs):
    n = h.shape[0]
    c, s1, s2 = tabs
    tab = pl.BlockSpec((1, LANE), lambda i: (0, 0))
    return pl.pallas_call(
        _prep_s_kernel,
        grid=(1,),
        in_specs=[pl.BlockSpec((n, 1024), lambda i: (0, C_Q // 1024)),
                  pl.BlockSpec((n, 512), lambda i: (0, C_KV // 512)),
                  pl.BlockSpec((n, 512), lambda i: (0, C_KV // 512 + 1)),
                  pl.BlockSpec((n, 512), lambda i: (0, C_KV // 512 + 2)),
                  pl.BlockSpec((n, LANE), lambda i: (0, C_NG // LANE)),
                  tab, tab, tab],
        out_specs=[pl.BlockSpec((n, 1536), lambda i: (0, 0)),
                   pl.BlockSpec((n, 1024), lambda i: (0, 0)),
                   pl.BlockSpec((n, LANE), lambda i: (0, 0))],
        out_shape=[jax.ShapeDtypeStruct((n, 1536), F32),
                   jax.ShapeDtypeStruct((n, 1024), F32),
                   jax.ShapeDtypeStruct((n, LANE), F32)],
        compiler_params=_cparams(("arbitrary",)),
        name="prep_sample",
    )(h, h, h, h, h, c, s1, s2)


def _ab_s_kernel(pu_ref, pz_ref, cb_ref, cc_ref, cx_ref, cz_ref, ph_ref, ch_ref,
                 pw_ref, ps_ref, cw_ref, ya_ref, yb_ref, hc_ref, *, pos0):
    pu = pu_ref[...]
    acc = pu
    means = []
    k = 1
    for gi, w in enumerate(POOL_WINDOWS):
        ch = slice(gi * POOL_GROUP, (gi + 1) * POOL_GROUP)
        while k < w:
            acc = acc + ph_ref[POOL_HIST - k]
            k += 1
        means.append(acc[:, ch] / float(min(w, pos0 + 1)))
    pooled = jnp.concatenate(means, axis=1) - pu
    ya_ref[...] = _pool_group_mix(pooled, pw_ref, ps_ref[...], pz_ref[...]).astype(BF16)
    hc = cc_ref[...] * cx_ref[...]
    cw = cw_ref[...]
    conv = ch_ref[0] * cw[0:1, :]
    conv = conv + ch_ref[1] * cw[1:2, :]
    conv = conv + hc * cw[2:3, :]
    yb_ref[...] = ((cb_ref[...] * conv) * _silu(cz_ref[...])).astype(BF16)
    hc_ref[...] = hc


def _ab_sample(h, pool_hist_t, conv_hist_t, pw_bf, pscale, cw, pos0):
    n = h.shape[0]
    colb = lambda c: pl.BlockSpec((n, 512), lambda i: (0, c // 512))
    full = lambda shape: pl.BlockSpec(shape, lambda i: (0,) * len(shape))
    kern = functools.partial(_ab_s_kernel, pos0=pos0)
    return pl.pallas_call(
        kern,
        grid=(1,),
        in_specs=[colb(C_PU), colb(C_PZ), colb(C_CB), colb(C_CC), colb(C_CX), colb(C_CZ),
                  full((POOL_HIST, n, D_POOL)), full((CONV_WIDTH - 1, n, D_CONV)),
                  full((len(POOL_WINDOWS), POOL_GROUP, POOL_GROUP)), full((1, D_POOL)), full((CONV_WIDTH, D_CONV))],
        out_specs=[full((n, D_POOL)), full((n, D_CONV)), full((n, D_CONV))],
        out_shape=[jax.ShapeDtypeStruct((n, D_POOL), BF16),
                   jax.ShapeDtypeStruct((n, D_CONV), BF16),
                   jax.ShapeDtypeStruct((n, D_CONV), F32)],
        compiler_params=_cparams(("arbitrary",)),
        name="ab_sample",
    )(h, h, h, h, h, h, pool_hist_t, conv_hist_t, pw_bf, pscale, cw)


def _attn_s_kernel(pt_ref, *refs, n_pages, past_len, n_cmp, n_blk, wb):
    del pt_ref
    np_ = n_pages
    kc_pages = refs[0:np_]
    vc_pages = refs[np_:2 * np_]
    ks_pages = refs[2 * np_:3 * np_]
    vs_pages = refs[3 * np_:4 * np_]
    (kwin_ref, vwin_ref, qz_ref, new_ref, newc_ref, g_ref, ws_ref, pec_ref, c_ref, s1_ref, s2_ref,
     cov_ref, exp_ref, hm_ref, perm_ref, *rest) = refs[4 * np_:]
    o_ref, kwo_ref, vwo_ref, kx_ref, vx_ref = rest[-5:]
    qpos = past_len
    rows = past_len // np_
    qz = qz_ref[0]
    qzb = qz.astype(BF16)
    new = new_ref[0]
    hm = hm_ref[...]
    nch = past_len // CMP_STRIDE

    def page_t(r):
        return r[0, 0].reshape(D_KV, rows)

    perm = perm_ref[...]
    cpp = rows // CMP_STRIDE
    for p in range(np_):
        _gather_rows(perm, page_t(kc_pages[p]), kx_ref, p, cpp)
        _gather_rows(perm, page_t(vc_pages[p]), vx_ref, p, cpp)
    kc = _rope(_compress_rows(kx_ref, ws_ref, pec_ref, 0, nch), c_ref[...], s1_ref[...], s2_ref[...])
    vc = _compress_rows(vx_ref, ws_ref, pec_ref, 1, nch)
    ncp = kc.shape[0]
    s_c = _dot_nt(qzb, kc.astype(BF16))
    cidx = lax.broadcasted_iota(jnp.int32, (N_HEADS, ncp), 1)
    vis = (cidx * CMP_STRIDE + (CMP_LEN - 1) <= qpos) & (cidx < n_cmp)
    m_c = jnp.max(jnp.where(vis, s_c, NEG), axis=1, keepdims=True)
    e_c = jnp.where(vis, jnp.exp(s_c - m_c), 0.0)
    p_c = e_c / jnp.maximum(jnp.sum(e_c, axis=1, keepdims=True), 1e-30)
    p_cb = p_c.astype(BF16)
    o_cmp = _dot(p_cb, vc.astype(BF16))
    imp_h = _dot(p_cb, cov_ref[...])
    imp = jnp.concatenate(
        [jnp.sum(imp_h[g * GROUP:(g + 1) * GROUP, :], axis=0, keepdims=True) for g in range(N_KV)], axis=0)

    nbp = imp.shape[1]
    bidx = lax.broadcasted_iota(jnp.int32, (N_KV, nbp), 1)
    cur = qpos // SEL_LEN
    eligible = bidx * SEL_LEN <= qpos
    forced = (bidx == 0) | (bidx == cur) | (bidx == cur - 1)
    score = jnp.where(forced, FORCE_SCORE, jnp.where(eligible, imp, -jnp.inf))
    sel = _select_blocks(score, n_blk, 1)
    sel_h = jnp.concatenate(
        [jnp.broadcast_to(sel[g:g + 1, :], (GROUP, nbp)) for g in range(N_KV)], axis=0)
    kmask = _dot(sel_h.astype(BF16), exp_ref[...])

    def attend(k_t, v_t, ok, k_new, v_new):
        s = _dot(qzb, k_t)
        s_new = jnp.sum(qzb.astype(F32) * k_new.astype(BF16).astype(F32), axis=1, keepdims=True)
        m = jnp.maximum(jnp.max(jnp.where(ok, s, NEG), axis=1, keepdims=True), s_new)
        e = jnp.where(ok, jnp.exp(s - m), 0.0)
        e_new = jnp.exp(s_new - m)
        den = jnp.maximum(jnp.sum(e, axis=1, keepdims=True) + e_new, 1e-30)
        p = (e / den).astype(BF16)
        p_new = (e_new / den).astype(BF16).astype(F32)
        return _dot_nt(p, v_t) + p_new * v_new.astype(BF16).astype(F32)

    ks_t = jnp.concatenate([page_t(r).astype(BF16) for r in ks_pages], axis=1)
    vs_t = jnp.concatenate([page_t(r).astype(BF16) for r in vs_pages], axis=1)
    kpos = lax.broadcasted_iota(jnp.int32, (N_HEADS, past_len), 1)
    ok_s = (kmask > 0.5) & (kpos <= qpos)
    o_sel = attend(ks_t, vs_t, ok_s, new[0:1, :], new[1:2, :])

    kw_t = kwin_ref[0, 0].reshape(D_KV, wb)
    vw_t = vwin_ref[0, 0].reshape(D_KV, wb)
    wpos = (past_len - wb) + lax.broadcasted_iota(jnp.int32, (N_HEADS, wb), 1)
    ok_w = (wpos <= qpos) & (qpos - wpos < WINDOW)
    o_win = attend(kw_t.astype(BF16), vw_t.astype(BF16), ok_w, new[2:3, :], new[3:4, :])

    g = g_ref[0]
    o = g[:, 0:1] * o_cmp + g[:, 1:2] * o_sel + g[:, 2:3] * o_win
    o = o * hm
    o_ref[0] = (o[:, 0:64] + o[:, 64:128]) + (o[:, 128:192] + o[:, 192:256])

    cols = newc_ref[0]
    lane = lax.broadcasted_iota(jnp.int32, (D_KV, wb), 1)
    kwo = jnp.where(lane == wb - 1, cols[:, 2:3], pltpu.roll(kw_t, wb - 1, 1))
    vwo = jnp.where(lane == wb - 1, cols[:, 3:4], pltpu.roll(vw_t, wb - 1, 1))
    kwo_ref[0, 0] = kwo.reshape(N_KV, HEAD_DIM, wb)
    vwo_ref[0, 0] = vwo.reshape(N_KV, HEAD_DIM, wb)


def _attn_sample(page_table, kc_t, vc_t, ks_t, vs_t, kwin_t, vwin_t, qz, new_rows, gates, ws, pec, ctabs,
                 cov, expand, headmask, layer, past_len, n_cmp, n_blk, win_prev):
    n, n_pages = page_table.shape
    depth = kwin_t.shape[0]
    wbuf = kwin_t.shape[-1]
    rows_pp = ks_t.shape[-1]
    c, s1, s2 = ctabs
    ncp = c.shape[0]
    perm = _row_perm(rows_pp)

    def page_spec(p):
        return pl.BlockSpec((1, 1, N_KV, HEAD_DIM, rows_pp), lambda i, pt: (layer, pt[i, p], 0, 0, 0))

    full = lambda shape: pl.BlockSpec(shape, lambda i, pt: (0,) * len(shape))
    win_spec = pl.BlockSpec((1, 1, N_KV, HEAD_DIM, wbuf), lambda i, pt: (layer, i, 0, 0, 0))
    in_specs = ([page_spec(p) for p in range(n_pages)] * 4
                + [win_spec, win_spec,
                   pl.BlockSpec((1, N_HEADS, D_KV), lambda i, pt: (i, 0, 0)),
                   pl.BlockSpec((1, 8, D_KV), lambda i, pt: (i, 0, 0)),
                   pl.BlockSpec((1, D_KV, 8), lambda i, pt: (i, 0, 0)),
                   pl.BlockSpec((1, N_HEADS, LANE), lambda i, pt: (i, 0, 0)),
                   full(ws.shape), full(pec.shape),
                   full((ncp, D_KV)), full((ncp, D_KV)), full((ncp, D_KV)),
                   full(cov.shape), full(expand.shape), full(headmask.shape), full(perm.shape)])
    args = ([kc_t] * n_pages + [vc_t] * n_pages + [ks_t] * n_pages + [vs_t] * n_pages
            + [kwin_t, vwin_t, qz, new_rows, jnp.transpose(new_rows, (0, 2, 1)), gates, ws, pec, c, s1, s2,
               cov, expand, headmask, perm])
    aliases = {}
    if win_prev is not None:
        for j, a in enumerate(win_prev):
            aliases[1 + len(args)] = 1 + j
            args.append(a)
            in_specs.append(pl.BlockSpec(memory_space=pl.ANY))
    kern = functools.partial(_attn_s_kernel, n_pages=n_pages, past_len=past_len, n_cmp=n_cmp,
                             n_blk=n_blk, wb=wbuf)
    wout = pl.BlockSpec((1, 1, N_KV, HEAD_DIM, wbuf), lambda i, pt: (layer, i, 0, 0, 0))
    grid_spec = pltpu.PrefetchScalarGridSpec(
        num_scalar_prefetch=1, grid=(n,), in_specs=in_specs,
        out_specs=[pl.BlockSpec((1, N_HEADS, HEAD_DIM), lambda i, pt: (i, 0, 0)), wout, wout],
        scratch_shapes=[pltpu.VMEM((CMP_STRIDE, past_len // CMP_STRIDE, D_KV), F32),
                        pltpu.VMEM((CMP_STRIDE, past_len // CMP_STRIDE, D_KV), F32)])
    wshape = jax.ShapeDtypeStruct((depth, n, N_KV, HEAD_DIM, wbuf), F32)
    return pl.pallas_call(
        kern,
        grid_spec=grid_spec,
        out_shape=[jax.ShapeDtypeStruct((n, N_HEADS, HEAD_DIM), F32), wshape, wshape],
        input_output_aliases=aliases,
        compiler_params=_cparams(("arbitrary",)),
        name="attn_sample",
    )(page_table, *args)


def kernel(x_prompt, x_sample, cache_k_cmp, cache_v_cmp, cache_k_sel, cache_v_sel, page_table, state_k_win, state_v_win, state_pool, state_conv, w_in, pool_w, pool_scale, conv_w, cmp_pe, cmp_w, proj_a, proj_b, proj_c, w_out, ln_g, ln_b):
    b, t, _ = x_prompt.shape
    ns = x_sample.shape[0]
    assert x_sample.shape[1] == 1
    depth = w_in.shape[0]
    page = cache_k_cmp.shape[2]
    n_pages = page_table.shape[1]
    past_len = n_pages * page
    wbuf = state_k_win.shape[2]
    assert t % 512 == 0 and page % CMP_STRIDE == 0 and wbuf == WINDOW and past_len >= WINDOW

    n_cmp_p = (t - CMP_LEN) // CMP_STRIDE + 1
    n_blk_p = -(-t // SEL_LEN)
    nch_p = t // CMP_STRIDE
    tabs_p = _rope_tables(np.arange(t), LANE)
    ctabs_p = _rope_tables(np.arange(nch_p) * CMP_STRIDE + CMP_LEN - 1, D_KV)
    cov_p = jnp.asarray(_cover_t(nch_p, n_blk_p, n_cmp_p, n_blk_p), BF16)
    cb_p, tri_p = _attn_bias_tiles(t, nch_p, n_cmp_p)

    total_s = past_len + 1
    n_cmp_s = (total_s - CMP_LEN) // CMP_STRIDE + 1
    n_blk_s = -(-total_s // SEL_LEN)
    nch_s = past_len // CMP_STRIDE
    tabs_s = _rope_tables(np.asarray([past_len]), LANE)
    ctabs_s = _rope_tables(np.arange(nch_s) * CMP_STRIDE + CMP_LEN - 1, D_KV)
    cov_s = jnp.asarray(_cover_t(nch_s, LANE, n_cmp_s, n_blk_s).T, BF16)
    expand = jnp.asarray((np.arange(LANE)[:, None] == (np.arange(past_len)[None, :] // SEL_LEN)), BF16)
    headmask = jnp.asarray((np.arange(N_HEADS)[:, None] // GROUP) == (np.arange(D_KV)[None, :] // HEAD_DIM), F32)

    fm = lambda a: jnp.transpose(a, (0, 1, 3, 4, 2))
    kc_t, vc_t, ks_t, vs_t = fm(cache_k_cmp), fm(cache_v_cmp), fm(cache_k_sel), fm(cache_v_sel)
    kwin_t, vwin_t = fm(state_k_win), fm(state_v_win)

    xp = x_prompt.reshape(b * t, D_MODEL)
    xs = x_sample.reshape(ns, D_MODEL)
    xp_bf = xp.astype(BF16)
    xs_bf = xs.astype(BF16)
    st_p, st_s = [], []
    win_s = None
    st6 = None
    for l in range(depth):
        w_bf = _perm_w_in(w_in[l])
        ws, pec = _cmp_weights_rows(cmp_w[l], cmp_pe[l])
        pw_bf = pool_w[l].astype(BF16)
        pscale = pool_scale[l].reshape(1, D_POOL)
        cw = conv_w[l]
        pa, pb, pc, wo = (a[l].astype(BF16) for a in (proj_a, proj_b, proj_c, w_out))
        lg = ln_g[l].reshape(1, D_MODEL)
        lb = ln_b[l].reshape(1, D_MODEL)

        h = _inproj(xp_bf, w_bf, 2048)
        *st6, qt, ksa, kwg, vst, vwt, gt = _prep_prompt(h, tabs_p, b, t, l, depth, st6)
        kcg, vct = _compress_prompt(st6[0], st6[1], ws, pec, ctabs_p, l, b)
        yc = _attn_prompt(qt, kcg, vct, ksa, vst, kwg, vwt, gt, cov_p, cb_p, tri_p, b, t, n_blk_p)
        ya, yb, hct, put = _ab_prompt(h, pw_bf, pscale, cw, b, t)
        xp_new, xp_bf = _tail(ya, yb, yc, h, xp, pa, pb, pc, wo, lg, lb, 512)
        st_p.append((put[:, 16 - POOL_HIST:],
                     hct[:, 8 - (CONV_WIDTH - 1):]))
        xp = xp_new

        hs = _inproj(xs_bf, w_bf, ns)
        kvs, qrot, gates = _prep_sample(hs, tabs_s)
        q3 = qrot.reshape(ns, N_HEADS, 1, HEAD_DIM)
        qz = (q3 * jnp.eye(N_KV, dtype=F32)[jnp.arange(N_HEADS) // GROUP][None, :, :, None]).reshape(ns, N_HEADS, D_KV)
        new_rows = jnp.concatenate(
            [kvs[:, 2 * D_KV:6 * D_KV].reshape(ns, 4, D_KV), jnp.zeros((ns, 4, D_KV), F32)], axis=1)
        g4 = gates[:, :64].reshape(ns, N_KV, 4, GROUP)[:, :, :3, :]
        g_h = jnp.transpose(g4, (0, 1, 3, 2)).reshape(ns, N_HEADS, 3)
        g_h = jnp.concatenate([g_h, jnp.zeros((ns, N_HEADS, LANE - 3), F32)], axis=2)
        ycs, kwo, vwo = _attn_sample(page_table, kc_t, vc_t, ks_t, vs_t, kwin_t, vwin_t, qz, new_rows, g_h, ws, pec,
                                     ctabs_s, cov_s, expand, headmask, l, past_len, n_cmp_s, n_blk_s, win_s)
        win_s = (kwo, vwo)
        ycs = ycs.reshape(ns, D_ATTN)
        pool_hist = state_pool[l]
        conv_hist = state_conv[l]
        yas, ybs, hcs = _ab_sample(hs, jnp.transpose(pool_hist, (1, 0, 2)), jnp.transpose(conv_hist, (1, 0, 2)),
                                   pw_bf, pscale, cw, past_len)
        xs_new, xs_bf = _tail(yas, ybs, ycs, hs, xs, pa, pb, pc, wo, lg, lb, ns)
        kvs5 = kvs.reshape(ns, 1, 6, N_KV, HEAD_DIM)
        st_s.append((kvs5[:, :, 0], kvs5[:, :, 1], kvs5[:, :, 2], kvs5[:, :, 3],
                     jnp.concatenate([pool_hist, hs[:, None, C_PU:C_PU + D_POOL]], axis=1)[:, 1:],
                     jnp.concatenate([conv_hist, hcs[:, None, :]], axis=1)[:, 1:]))
        xs = xs_new

    stack = lambda states, i: jnp.stack([s[i] for s in states], axis=0)
    rm = lambda a: jnp.transpose(a.reshape(a.shape[0], a.shape[1], N_KV, HEAD_DIM, a.shape[3]), (0, 1, 4, 2, 3))
    rm5 = lambda a: jnp.transpose(a, (0, 1, 4, 2, 3))
    return ((xp.reshape(b, t, D_MODEL), xs.reshape(ns, 1, D_MODEL))
            + tuple(rm(a) for a in st6) + (stack(st_p, 0), stack(st_p, 1))
            + tuple(stack(st_s, i) for i in range(4)) + (rm5(win_s[0]), rm5(win_s[1]))
            + (stack(st_s, 4), stack(st_s, 5)))
```

```python
import functools

import numpy as np
import jax
import jax.numpy as jnp
from jax import lax
from jax.experimental import pallas as pl
from jax.experimental.pallas import tpu as pltpu

D_MODEL = 1024
DEPTH = 2
D_POOL = 512
POOL_WINDOWS = (2, 4, 8, 16)
POOL_GROUP = D_POOL // len(POOL_WINDOWS)
POOL_HIST = max(POOL_WINDOWS) - 1
D_CONV = 512
CONV_WIDTH = 3
HEAD_DIM = 64
N_HEADS = 16
N_KV = 4
GROUP = N_HEADS // N_KV
D_ATTN = N_HEADS * HEAD_DIM
D_KV = N_KV * HEAD_DIM
ROT_DIM = HEAD_DIM // 4
ROPE_THETA = 500000.0
CMP_LEN = 32
CMP_STRIDE = 16
SEL_LEN = 64
SEL_SHIFT = 6
N_SEL = 8
WINDOW = 512
FORCE_SCORE = 1.0e4
LN_EPS = 1e-5
ALPHA = (2 * DEPTH) ** 0.25
SCALE = HEAD_DIM ** -0.5
LOG2E = 1.4426950408889634

C_PU, C_PZ, C_CB, C_CC, C_CX, C_CZ = 0, 512, 1024, 1536, 2048, 2560
C_Q = 3072
C_AZ = 4096
C_MG = 5120
C_KV = 8192
C_NG = 9728
D_H = 9984
NG_PAD = 128

NEG = -1.0e30
LANE = 128
TQ = 512
KCH = 512
NSPLIT = 1
AUG = 16
VMEM_LIMIT = 48 * 1024 * 1024

BF16 = jnp.bfloat16
F32 = jnp.float32


def _cparams(sem):
    return pltpu.CompilerParams(dimension_semantics=sem, vmem_limit_bytes=VMEM_LIMIT)


def _dot(a, b):
    return jnp.dot(a, b, preferred_element_type=F32)


def _dot_nt(a, b):
    return lax.dot_general(a, b, (((1,), (1,)), ((), ())), preferred_element_type=F32)


def _silu(x):
    return x * jax.nn.sigmoid(x)


def _perm_w_in(w):
    ab = w[:, 0:3072]
    q = w[:, 3072:4096]
    kv = w[:, 4096:5632]
    ng = w[:, 5632:5680]
    az = w[:, 5680:6704]
    mg = w[:, 6704:9776]
    idx = np.full((NG_PAD,), 48, np.int32)
    for g in range(N_KV):
        for br in range(3):
            for r in range(GROUP):
                idx[g * 16 + br * 4 + r] = (GROUP * g + r) * 3 + br
    ng_ext = jnp.concatenate([ng, jnp.zeros((w.shape[0], 1), w.dtype)], axis=1)
    ng_p = jnp.take(ng_ext, jnp.asarray(idx), axis=1)
    pad = jnp.zeros((w.shape[0], D_H - C_NG - NG_PAD), w.dtype)
    return jnp.concatenate([ab, q, az, mg, kv, ng_p, pad], axis=1).astype(BF16)


def _rope_tables(pos, width):
    pos = np.asarray(pos, np.float32)
    p = pos.shape[0]
    inv_freq = (1.0 / (ROPE_THETA ** (np.arange(0, ROT_DIM, 2, dtype=np.float32) / ROT_DIM))).astype(np.float32)
    ang = (pos[:, None] * inv_freq[None, :]).astype(np.float32).astype(np.float64)
    cos = np.cos(ang).astype(np.float32)
    sin = np.sin(ang).astype(np.float32)
    z8 = np.zeros((p, 8), np.float32)
    z48 = np.zeros((p, 48), np.float32)
    c64 = np.concatenate([cos, cos, np.ones((p, 48), np.float32)], axis=1)
    s1 = np.concatenate([-sin, z8, z48], axis=1)
    s2 = np.concatenate([z8, sin, z48], axis=1)
    rep = width // HEAD_DIM
    return tuple(jnp.asarray(np.tile(t, (1, rep))) for t in (c64, s1, s2))


def _rope(x, c, s1, s2):
    w = x.shape[-1]
    up = pltpu.roll(x, w - 8, 1)
    dn = pltpu.roll(x, 8, 1)
    return x * c + up * s1 + dn * s2


def _tile_lanes(t, rep):
    return t if rep == 1 else jnp.concatenate([t] * rep, axis=1)


def _cmp_weights_rows(cmp_w_l, cmp_pe_l):
    eye2 = jnp.eye(2, dtype=F32)
    w = cmp_w_l.reshape(2, 2, CMP_STRIDE, HEAD_DIM, HEAD_DIM)
    ws = jnp.einsum('ktshe,gf->ksghtfe', w, eye2).reshape(2, CMP_STRIDE, 2 * HEAD_DIM, 4 * HEAD_DIM)
    pec = jnp.einsum('kph,kphe->ke', cmp_pe_l, cmp_w_l.reshape(2, CMP_LEN, HEAD_DIM, HEAD_DIM),
                     precision=lax.Precision.HIGHEST)
    pec = jnp.tile(pec, (1, N_KV)).reshape(2, 1, D_KV)
    return ws.astype(BF16), pec


def _cover_t(n_cmp_pad, n_blk_pad, n_cmp, n_blk):
    c0 = np.arange(n_cmp_pad)[None, :] * CMP_STRIDE
    s0 = np.arange(n_blk_pad)[:, None] * SEL_LEN
    m = (c0 < s0 + SEL_LEN) & (c0 + CMP_LEN > s0)
    m &= (np.arange(n_cmp_pad)[None, :] < n_cmp) & (np.arange(n_blk_pad)[:, None] < n_blk)
    return m.astype(np.float32)


def _mm_kernel(x_ref, w_ref, o_ref):
    o_ref[...] = _dot(x_ref[...], w_ref[...])


def _inproj(x_bf, w_bf, tm):
    m = x_bf.shape[0]
    tn = 768
    return pl.pallas_call(
        _mm_kernel,
        grid=(m // tm, D_H // tn),
        in_specs=[pl.BlockSpec((tm, D_MODEL), lambda i, j: (i, 0)),
                  pl.BlockSpec((D_MODEL, tn), lambda i, j: (0, j))],
        out_specs=pl.BlockSpec((tm, tn), lambda i, j: (i, j)),
        out_shape=jax.ShapeDtypeStruct((m, D_H), F32),
        compiler_params=_cparams(("arbitrary", "arbitrary")),
        name="inproj",
    )(x_bf, w_bf)


def _prep_p_kernel(q_ref, kcv_ref, ksv_ref, kwv_ref, ng_ref, c_ref, s1_ref, s2_ref,
                   *rest):
    (kct_ref, vct_ref, kstt_ref, vstt_ref, kwtt_ref, vwtt_ref,
     qt_ref, ksa_ref, kwg_ref, vst_ref, vwt_ref, gt_ref) = rest[-12:]
    tt = q_ref.shape[0]
    c, s1, s2 = c_ref[...], s1_ref[...], s2_ref[...]
    q = _rope(q_ref[...], _tile_lanes(c, 8), _tile_lanes(s1, 8), _tile_lanes(s2, 8)) * (SCALE * LOG2E)
    qt_ref[0] = q.T.astype(BF16)
    c2, s12, s22 = _tile_lanes(c, 2), _tile_lanes(s1, 2), _tile_lanes(s2, 2)
    kcv = kcv_ref[...]
    ksv = ksv_ref[...]
    kwv = kwv_ref[...]
    ks = _rope(ksv[:, :D_KV], c2, s12, s22)
    kw = _rope(kwv[:, :D_KV], c2, s12, s22)
    kcvt = kcv.T
    kct_ref[0, 0] = kcvt[:D_KV]
    vct_ref[0, 0] = kcvt[D_KV:]
    kst = ks.T
    vst = ksv[:, D_KV:].T
    kwt = kw.T
    vwt = kwv[:, D_KV:].T
    kstt_ref[0, 0] = kst
    vstt_ref[0, 0] = vst
    kwtt_ref[0, 0] = kwt
    vwtt_ref[0, 0] = vwt
    for j in range(tt // KCH):
        vst_ref[0, j] = vst[:, j * KCH:(j + 1) * KCH].astype(BF16)
        vwt_ref[0, j] = vwt[:, j * KCH:(j + 1) * KCH].astype(BF16)
    row = lax.broadcasted_iota(jnp.int32, (tt, LANE), 0)
    lane = lax.broadcasted_iota(jnp.int32, (tt, LANE), 1)
    blk = jnp.right_shift(jnp.bitwise_and(row, KCH - 1), SEL_SHIFT)
    onehot = jnp.where((lane >= HEAD_DIM) & (lane - HEAD_DIM == blk), 1.0, 0.0)
    zpad = jnp.zeros((tt, LANE - HEAD_DIM), F32)
    for g in range(N_KV):
        kg = ks[:, g * HEAD_DIM:(g + 1) * HEAD_DIM]
        ksa_ref[0, g] = (jnp.concatenate([kg, zpad], axis=1) + onehot).astype(BF16)
        kwg_ref[0, g] = kw[:, g * HEAD_DIM:(g + 1) * HEAD_DIM].astype(BF16)
    gt_ref[0] = jax.nn.sigmoid(ng_ref[...]).T[:64, :]


def _prep_prompt(h, tabs, b, t, layer, depth, prev):
    tt = 512
    assert min(WINDOW, t) == tt
    nt = t // tt
    row = lambda bi, ti: bi * nt + ti
    c, s1, s2 = tabs
    tab_spec = pl.BlockSpec((tt, LANE), lambda bi, ti: (ti, 0))
    in_specs = [pl.BlockSpec((tt, 1024), lambda bi, ti: (row(bi, ti), C_Q // 1024)),
                pl.BlockSpec((tt, 512), lambda bi, ti: (row(bi, ti), C_KV // 512)),
                pl.BlockSpec((tt, 512), lambda bi, ti: (row(bi, ti), C_KV // 512 + 1)),
                pl.BlockSpec((tt, 512), lambda bi, ti: (row(bi, ti), C_KV // 512 + 2)),
                pl.BlockSpec((tt, LANE), lambda bi, ti: (row(bi, ti), C_NG // LANE)),
                tab_spec, tab_spec, tab_spec]
    args = [h, h, h, h, h, c, s1, s2]
    aliases = {}
    if prev is not None:
        for j, a in enumerate(prev):
            aliases[len(args)] = j
            args.append(a)
            in_specs.append(pl.BlockSpec(memory_space=pl.ANY))
    full_t = pl.BlockSpec((1, 1, D_KV, tt), lambda bi, ti: (layer, bi, 0, ti))
    win_t = pl.BlockSpec((1, 1, D_KV, tt), lambda bi, ti: (layer, bi, 0, 0))
    st_full = jax.ShapeDtypeStruct((depth, b, D_KV, t), F32)
    st_win = jax.ShapeDtypeStruct((depth, b, D_KV, tt), F32)
    return pl.pallas_call(
        _prep_p_kernel,
        grid=(b, nt),
        in_specs=in_specs,
        out_specs=[full_t, full_t, full_t, full_t, win_t, win_t,
                   pl.BlockSpec((1, 1024, tt), lambda bi, ti: (bi, 0, ti)),
                   pl.BlockSpec((1, N_KV, tt, LANE), lambda bi, ti: (bi, 0, ti, 0)),
                   pl.BlockSpec((1, N_KV, tt, HEAD_DIM), lambda bi, ti: (bi, 0, ti, 0)),
                   pl.BlockSpec((1, tt // KCH, D_KV, KCH), lambda bi, ti: (bi, ti, 0, 0)),
                   pl.BlockSpec((1, tt // KCH, D_KV, KCH), lambda bi, ti: (bi, ti, 0, 0)),
                   pl.BlockSpec((1, 64, tt), lambda bi, ti: (bi, 0, ti))],
        out_shape=[st_full, st_full, st_full, st_full, st_win, st_win,
                   jax.ShapeDtypeStruct((b, 1024, t), BF16),
                   jax.ShapeDtypeStruct((b, N_KV, t, LANE), BF16),
                   jax.ShapeDtypeStruct((b, N_KV, t, HEAD_DIM), BF16),
                   jax.ShapeDtypeStruct((b, t // KCH, D_KV, KCH), BF16),
                   jax.ShapeDtypeStruct((b, t // KCH, D_KV, KCH), BF16),
                   jax.ShapeDtypeStruct((b, 64, t), F32)],
        input_output_aliases=aliases,
        compiler_params=_cparams(("arbitrary", "arbitrary")),
        name="prep_prompt",
    )(*args)


def _gather_rows(perm, page_t, x_ref, p, cpp):
    x = _dot_nt(perm, page_t.astype(BF16))
    for s_ in range(CMP_STRIDE):
        x_ref[s_, p * cpp:(p + 1) * cpp, :] = x[s_ * cpp:(s_ + 1) * cpp, :]


def _compress_rows(x_ref, ws_ref, pec_ref, kv, nch):
    halves = []
    for hf in range(2):
        acc = jnp.zeros((nch, 2 * LANE), F32)
        for s_ in range(CMP_STRIDE):
            acc = acc + _dot(x_ref[s_, :, hf * LANE:(hf + 1) * LANE].astype(BF16), ws_ref[kv, s_])
        bot_up = jnp.concatenate([acc[1:, LANE:], jnp.zeros((1, LANE), F32)], axis=0)
        halves.append(acc[:, :LANE] + bot_up)
    return jnp.concatenate(halves, axis=1) + pec_ref[kv]


def _row_perm(rows):
    cpp = rows // CMP_STRIDE
    i_ = np.arange(rows)
    return jnp.asarray(i_[None, :] == ((i_ % cpp) * CMP_STRIDE + i_ // cpp)[:, None], BF16)


def _compress_p_kernel(kt_ref, vt_ref, ws_ref, pec_ref, perm_ref, c_ref, s1_ref, s2_ref, kcg_ref, vct_ref,
                       kx_ref, vx_ref, *, rows):
    t = kt_ref.shape[-1]
    nch = t // CMP_STRIDE
    cpp = rows // CMP_STRIDE
    perm = perm_ref[...]
    for p in range(t // rows):
        _gather_rows(perm, kt_ref[0, 0, :, p * rows:(p + 1) * rows], kx_ref, p, cpp)
        _gather_rows(perm, vt_ref[0, 0, :, p * rows:(p + 1) * rows], vx_ref, p, cpp)
    kc = _rope(_compress_rows(kx_ref, ws_ref, pec_ref, 0, nch), c_ref[...], s1_ref[...], s2_ref[...])
    vc = _compress_rows(vx_ref, ws_ref, pec_ref, 1, nch)
    for g in range(N_KV):
        kcg_ref[0, g] = kc[:, g * HEAD_DIM:(g + 1) * HEAD_DIM].astype(BF16)
    vct_ref[0] = vc.T.astype(BF16)


def _compress_prompt(kct, vct_, ws, pec, ctabs, layer, b):
    t = kct.shape[-1]
    nch = t // CMP_STRIDE
    rows = LANE
    perm = _row_perm(rows)
    c, s1, s2 = ctabs
    full = lambda shape: pl.BlockSpec(shape, lambda bi: (0,) * len(shape))
    seq = pl.BlockSpec((1, 1, D_KV, t), lambda bi: (layer, bi, 0, 0))
    return pl.pallas_call(
        functools.partial(_compress_p_kernel, rows=rows),
        grid=(b,),
        in_specs=[seq, seq, full(ws.shape), full(pec.shape), full(perm.shape),
                  full((nch, D_KV)), full((nch, D_KV)), full((nch, D_KV))],
        out_specs=[pl.BlockSpec((1, N_KV, nch, HEAD_DIM), lambda bi: (bi, 0, 0, 0)),
                   pl.BlockSpec((1, D_KV, nch), lambda bi: (bi, 0, 0))],
        out_shape=[jax.ShapeDtypeStruct((b, N_KV, nch, HEAD_DIM), BF16),
                   jax.ShapeDtypeStruct((b, D_KV, nch), BF16)],
        scratch_shapes=[pltpu.VMEM((CMP_STRIDE, nch, D_KV), F32), pltpu.VMEM((CMP_STRIDE, nch, D_KV), F32)],
        compiler_params=_cparams(("arbitrary",)),
        name="compress_prompt",
    )(kct, vct_, ws, pec, perm, c, s1, s2)


def _select_blocks(score, n_blk, axis):
    idx = lax.broadcasted_iota(jnp.int32, score.shape, axis)
    rank = jnp.zeros(score.shape, F32)
    for j in range(n_blk):
        if axis == 0:
            sj = score[j:j + 1, :]
        else:
            sj = score[:, j:j + 1]
        beats = (sj > score) | ((sj == score) & (idx > j))
        rank = rank + jnp.where(beats, 1.0, 0.0)
    keep = (rank < float(N_SEL)) & (score > -jnp.inf) & (idx < n_blk)
    return jnp.where(keep, 1.0, 0.0)


def _select_rows(score, n_blk):
    nb = score.shape[0]
    idx = lax.broadcasted_iota(jnp.int32, score.shape, 0).astype(F32)
    cur = jnp.where(idx < float(n_blk), score, -jnp.inf)
    keep = jnp.zeros(score.shape, F32)
    for _ in range(N_SEL):
        mx = jnp.max(cur, axis=0, keepdims=True)
        first = jnp.min(jnp.where(cur == mx, idx, float(nb)), axis=0, keepdims=True)
        took = jnp.where(mx > -jnp.inf, 1.0, 0.0)
        keep = jnp.maximum(keep, jnp.where(idx == first, took, 0.0))
        cur = jnp.where(idx == first, -jnp.inf, cur)
    return keep


def _attn_p_kernel(qt_ref, kcg_ref, vct_ref, ksa_ref, vst_ref, kwg_ref, vwt_ref, gt_ref, cov_ref, cb_ref, tri_ref,
                   o_ref, selb_ref, m_ref, l_ref, acc_ref, comb_ref, sa_ref, sb_ref, wa_ref, *, n_blk):
    qi = pl.program_id(2)
    t0 = qi * TQ
    n = GROUP * TQ
    qt = qt_ref[0]
    qcat = jnp.concatenate([qt[r * HEAD_DIM:(r + 1) * HEAD_DIM, :] for r in range(GROUP)], axis=1)
    gt = gt_ref[0]

    def heads(x):
        return jnp.concatenate([x] * GROUP, axis=1)

    def gate(branch):
        return jnp.concatenate([gt[branch * GROUP + r:branch * GROUP + r + 1, :] for r in range(GROUP)], axis=1)

    def kchunk(k_ref, c):
        return k_ref[0, 0, pl.ds(pl.multiple_of(c * KCH, KCH), KCH), :]

    s = _dot(kcg_ref[0, 0], qcat) + heads(cb_ref[0])
    e = jnp.exp2(s - jnp.max(s, axis=0, keepdims=True))
    qpos = t0 + jnp.bitwise_and(lax.broadcasted_iota(jnp.int32, (1, n), 1), TQ - 1)
    inv = jnp.where(qpos >= CMP_LEN - 1, 1.0, 0.0) / jnp.sum(e, axis=0, keepdims=True)
    p = (e * inv).astype(BF16)
    comb_ref[...] = gate(0) * _dot(vct_ref[0], p)
    imp4 = _dot(cov_ref[...], p)
    imp = imp4[:, 0:TQ]
    for r in range(1, GROUP):
        imp = imp + imp4[:, r * TQ:(r + 1) * TQ]

    nb = imp.shape[0]
    per = KCH // SEL_LEN
    bidx = lax.broadcasted_iota(jnp.int32, (nb, TQ), 0)
    qp = t0 + lax.broadcasted_iota(jnp.int32, (nb, TQ), 1)
    cur = jnp.right_shift(qp, SEL_SHIFT)
    eligible = bidx * SEL_LEN <= qp
    forced = (bidx == 0) | (bidx == cur) | (bidx == cur - 1)
    score = jnp.where(forced, FORCE_SCORE, jnp.where(eligible, imp, -jnp.inf))
    selb = jnp.where(_select_rows(score, n_blk) > 0.5, 0.0, NEG)
    zrows = jnp.zeros((AUG - per, TQ), F32)
    for c in range(nb // per):
        selb_ref[c] = jnp.concatenate([selb[c * per:(c + 1) * per, :], zrows], axis=0).astype(BF16)

    def reset():
        m_ref[...] = jnp.full((1, n), NEG, F32)
        l_ref[...] = jnp.zeros((1, n), F32)
        acc_ref[...] = jnp.zeros((HEAD_DIM, n), F32)

    def scores(k_ref, c, aug):
        rhs = qcat
        if aug:
            rhs = jnp.concatenate([qcat, heads(selb_ref[c]), jnp.zeros((LANE - HEAD_DIM - AUG, n), BF16)], axis=0)
        return _dot(kchunk(k_ref, c), rhs)

    def update(s, vchunk):
        m_old = m_ref[...]
        m_new = jnp.maximum(m_old, jnp.max(s, axis=0, keepdims=True))
        alpha = jnp.exp2(m_old - m_new)
        e = jnp.exp2(s - m_new)
        l_ref[...] = alpha * l_ref[...] + jnp.sum(e, axis=0, keepdims=True)
        acc_ref[...] = alpha * acc_ref[...] + _dot(vchunk, e.astype(BF16))
        m_ref[...] = m_new

    def flush(branch):
        comb_ref[...] = comb_ref[...] + gate(branch) * (acc_ref[...] / jnp.maximum(l_ref[...], 1e-30))

    causal = lambda: heads(tri_ref[0])
    lowcut = lambda: heads(tri_ref[1])

    def wscores(c):
        return _dot(kchunk(kwg_ref, c), qcat)

    reset()
    sa_ref[...] = scores(ksa_ref, 0, True)

    def pair(i, carry):
        c = 2 * i
        sb_ref[...] = scores(ksa_ref, c + 1, True)
        update(sa_ref[...], vst_ref[0, c])
        sa_ref[...] = scores(ksa_ref, c + 2, True)
        update(sb_ref[...], vst_ref[0, c + 1])
        return carry

    lax.fori_loop(0, qi // 2, pair, 0)
    wfirst = jnp.maximum(qi - 1, 0)

    @pl.when(qi % 2 == 1)
    def _():
        sb_ref[...] = scores(ksa_ref, qi, True)
        update(sa_ref[...], vst_ref[0, qi - 1])
        wa_ref[...] = wscores(wfirst)
        update(sb_ref[...] + causal(), vst_ref[0, qi])

    @pl.when(qi % 2 == 0)
    def _():
        wa_ref[...] = wscores(wfirst)
        update(sa_ref[...] + causal(), vst_ref[0, qi])

    flush(1)

    reset()

    @pl.when(qi >= 1)
    def _():
        sb_ref[...] = wscores(qi)
        update(wa_ref[...] + lowcut(), vwt_ref[0, qi - 1])
        update(sb_ref[...] + causal(), vwt_ref[0, qi])

    @pl.when(qi == 0)
    def _():
        update(wa_ref[...] + causal(), vwt_ref[0, 0])

    flush(2)

    comb = comb_ref[...]
    o_ref[...] = jnp.concatenate([comb[:, r * TQ:(r + 1) * TQ].T for r in range(GROUP)], axis=1)


def _attn_prompt(qt, kcg, vct, ksa, vst, kwg, vwt, gt, cov, cb, tri, b, t, n_blk):
    nq = t // TQ
    nch = t // KCH
    ncp = kcg.shape[2]
    assert WINDOW == KCH and KCH == TQ and KCH // SEL_LEN <= AUG
    kern = functools.partial(_attn_p_kernel, n_blk=n_blk)
    n = GROUP * TQ
    return pl.pallas_call(
        kern,
        grid=(b, N_KV, nq),
        in_specs=[pl.BlockSpec((1, D_KV, TQ), lambda bi, g, qi: (bi, g, qi)),
                  pl.BlockSpec((1, 1, ncp, HEAD_DIM), lambda bi, g, qi: (bi, g, 0, 0)),
                  pl.BlockSpec((1, HEAD_DIM, ncp), lambda bi, g, qi: (bi, g, 0)),
                  pl.BlockSpec((1, 1, t, LANE), lambda bi, g, qi: (bi, g, 0, 0)),
                  pl.BlockSpec((1, nch, HEAD_DIM, KCH), lambda bi, g, qi: (bi, 0, g, 0)),
                  pl.BlockSpec((1, 1, t, HEAD_DIM), lambda bi, g, qi: (bi, g, 0, 0)),
                  pl.BlockSpec((1, nch, HEAD_DIM, KCH), lambda bi, g, qi: (bi, 0, g, 0)),
                  pl.BlockSpec((1, 16, TQ), lambda bi, g, qi: (bi, g, qi)),
                  pl.BlockSpec(cov.shape, lambda bi, g, qi: (0, 0)),
                  pl.BlockSpec((1, ncp, TQ), lambda bi, g, qi: (qi, 0, 0)),
                  pl.BlockSpec(tri.shape, lambda bi, g, qi: (0, 0, 0))],
        out_specs=pl.BlockSpec((TQ, D_KV), lambda bi, g, qi: (bi * nq + qi, g)),
        out_shape=jax.ShapeDtypeStruct((b * t, D_ATTN), F32),
        scratch_shapes=[pltpu.VMEM((cov.shape[0] * SEL_LEN // KCH, AUG, TQ), BF16),
                        pltpu.VMEM((1, n), F32), pltpu.VMEM((1, n), F32),
                        pltpu.VMEM((HEAD_DIM, n), F32), pltpu.VMEM((HEAD_DIM, n), F32),
                        pltpu.VMEM((KCH, n), F32), pltpu.VMEM((KCH, n), F32), pltpu.VMEM((KCH, n), F32)],
        compiler_params=_cparams(("arbitrary", "arbitrary", "arbitrary")),
        name="attn_prompt",
    )(qt, kcg, vct, ksa, vst, kwg, vwt, gt, cov, cb, tri)


def _attn_bias_tiles(t, ncp, n_cmp):
    k = np.arange(KCH)[:, None]
    q = np.arange(TQ)[None, :]
    tri = np.stack([np.where(k <= q, 0.0, NEG), np.where(k > q, 0.0, NEG), np.zeros((KCH, TQ))]).astype(np.float32)
    c = np.arange(ncp)[None, :, None]
    qpos = (np.arange(t // TQ)[:, None, None] * TQ) + np.arange(TQ)[None, None, :]
    cb = np.where((c * CMP_STRIDE + CMP_LEN - 1 <= qpos) & (c < n_cmp), 0.0, NEG).astype(np.float32)
    return jnp.asarray(cb), jnp.asarray(tri)


def _pool_group_mix(pooled, pw_ref, scale, pz):
    mixed = jnp.concatenate(
        [_dot(pooled[:, g * POOL_GROUP:(g + 1) * POOL_GROUP].astype(BF16), pw_ref[g])
         for g in range(len(POOL_WINDOWS))], axis=1)
    return (mixed * scale) * _silu(pz)


def _ab_p_kernel(pu_ref, pz_ref, cb_ref, cc_ref, cx_ref, cz_ref, pup_ref, ccp_ref, cxp_ref,
                 pw_ref, ps_ref, cw_ref, ya_ref, yb_ref, hct_ref, put_ref, *, tm, tiles_per_seq):
    i = pl.program_id(0)
    ti = i % tiles_per_seq
    valid = jnp.where(ti > 0, 1.0, 0.0)
    pu = pu_ref[...]
    ext = jnp.concatenate([pup_ref[...] * valid, pu], axis=0)
    pos = ti * tm + lax.broadcasted_iota(jnp.int32, (tm, POOL_GROUP), 0)
    acc = pu
    means = []
    k = 1
    for gi, w in enumerate(POOL_WINDOWS):
        ch = slice(gi * POOL_GROUP, (gi + 1) * POOL_GROUP)
        while k < w:
            acc = acc + ext[16 - k:16 - k + tm, :]
            k += 1
        count = jnp.minimum(w, pos + 1).astype(F32)
        means.append(acc[:, ch] / count)
    pooled = jnp.concatenate(means, axis=1) - pu
    ya_ref[...] = _pool_group_mix(pooled, pw_ref, ps_ref[...], pz_ref[...]).astype(BF16)

    hc = cc_ref[...] * cx_ref[...]
    hprev = (ccp_ref[...] * cxp_ref[...]) * valid
    hext = jnp.concatenate([hprev, hc], axis=0)
    cw = cw_ref[...]
    conv = hext[14:14 + tm, :] * cw[0:1, :]
    conv = conv + hext[15:15 + tm, :] * cw[1:2, :]
    conv = conv + hc * cw[2:3, :]
    yb_ref[...] = ((cb_ref[...] * conv) * _silu(cz_ref[...])).astype(BF16)
    hct_ref[0] = hc[tm - 8:tm, :]
    put_ref[0] = pu[tm - 16:tm, :]


def _ab_prompt(h, pw_bf, pscale, cw, b, t):
    tm = 512
    m = b * t
    tps = t // tm
    colb = lambda c: (lambda i: (i, c // 512))
    prev = lambda c: (lambda i: (jnp.maximum(i * (tm // 16) - 1, 0), c // 512))
    kern = functools.partial(_ab_p_kernel, tm=tm, tiles_per_seq=tps)
    full = lambda shape: pl.BlockSpec(shape, lambda i: (0,) * len(shape))
    return pl.pallas_call(
        kern,
        grid=(m // tm,),
        in_specs=[pl.BlockSpec((tm, 512), colb(C_PU)), pl.BlockSpec((tm, 512), colb(C_PZ)),
                  pl.BlockSpec((tm, 512), colb(C_CB)), pl.BlockSpec((tm, 512), colb(C_CC)),
                  pl.BlockSpec((tm, 512), colb(C_CX)), pl.BlockSpec((tm, 512), colb(C_CZ)),
                  pl.BlockSpec((16, 512), prev(C_PU)), pl.BlockSpec((16, 512), prev(C_CC)),
                  pl.BlockSpec((16, 512), prev(C_CX)),
                  full((len(POOL_WINDOWS), POOL_GROUP, POOL_GROUP)), full((1, D_POOL)), full((CONV_WIDTH, D_CONV))],
        out_specs=[pl.BlockSpec((tm, D_POOL), lambda i: (i, 0)),
                   pl.BlockSpec((tm, D_CONV), lambda i: (i, 0)),
                   pl.BlockSpec((1, 8, D_CONV), lambda i: (i // tps, 0, 0)),
                   pl.BlockSpec((1, 16, D_POOL), lambda i: (i // tps, 0, 0))],
        out_shape=[jax.ShapeDtypeStruct((m, D_POOL), BF16),
                   jax.ShapeDtypeStruct((m, D_CONV), BF16),
                   jax.ShapeDtypeStruct((b, 8, D_CONV), F32),
                   jax.ShapeDtypeStruct((b, 16, D_POOL), F32)],
        compiler_params=_cparams(("arbitrary",)),
        name="ab_prompt",
    )(h, h, h, h, h, h, h, h, h, pw_bf, pscale, cw)


def _tail_kernel(ya_ref, yb_ref, yc_ref, az_ref, ga_ref, gb_ref, gc_ref, x_ref,
                 pa_ref, pb_ref, pc_ref, wo_ref, lg_ref, lb_ref, o_ref, obf_ref):
    yc = (yc_ref[...] * _silu(az_ref[...])).astype(BF16)
    merged = (jax.nn.sigmoid(ga_ref[...]) * _dot(ya_ref[...], pa_ref[...])
              + jax.nn.sigmoid(gb_ref[...]) * _dot(yb_ref[...], pb_ref[...])
              + jax.nn.sigmoid(gc_ref[...]) * _dot(yc, pc_ref[...]))
    y = _dot(merged.astype(BF16), wo_ref[...])
    z = ALPHA * x_ref[...] + y
    mu = jnp.mean(z, axis=-1, keepdims=True)
    var = jnp.mean(jnp.square(z - mu), axis=-1, keepdims=True)
    out = (z - mu) * lax.rsqrt(var + LN_EPS) * lg_ref[...] + lb_ref[...]
    o_ref[...] = out
    obf_ref[...] = out.astype(BF16)


def _tail(ya, yb, yc, h, x, pa, pb, pc, wo, lg, lb, tm):
    m = x.shape[0]
    full = lambda shape: pl.BlockSpec(shape, lambda i: (0,) * len(shape))
    hcol = lambda c: pl.BlockSpec((tm, 1024), lambda i: (i, c // 1024))
    return pl.pallas_call(
        _tail_kernel,
        grid=(m // tm,),
        in_specs=[pl.BlockSpec((tm, D_POOL), lambda i: (i, 0)),
                  pl.BlockSpec((tm, D_CONV), lambda i: (i, 0)),
                  pl.BlockSpec((tm, D_ATTN), lambda i: (i, 0)),
                  hcol(C_AZ), hcol(C_MG), hcol(C_MG + 1024), hcol(C_MG + 2048),
                  pl.BlockSpec((tm, D_MODEL), lambda i: (i, 0)),
                  full((D_POOL, D_MODEL)), full((D_CONV, D_MODEL)), full((D_ATTN, D_MODEL)),
                  full((D_MODEL, D_MODEL)), full((1, D_MODEL)), full((1, D_MODEL))],
        out_specs=[pl.BlockSpec((tm, D_MODEL), lambda i: (i, 0)),
                   pl.BlockSpec((tm, D_MODEL), lambda i: (i, 0))],
        out_shape=[jax.ShapeDtypeStruct((m, D_MODEL), F32),
                   jax.ShapeDtypeStruct((m, D_MODEL), BF16)],
        compiler_params=_cparams(("arbitrary",)),
        name="tail",
    )(ya, yb, yc, h, h, h, h, x, pa, pb, pc, wo, lg, lb)


def _prep_s_kernel(q_ref, kcv_ref, ksv_ref, kwv_ref, ng_ref, c_ref, s1_ref, s2_ref,
                   kvst_ref, qr_ref, g_ref):
    c, s1, s2 = c_ref[...], s1_ref[...], s2_ref[...]
    qr_ref[...] = _rope(q_ref[...], _tile_lanes(c, 8), _tile_lanes(s1, 8), _tile_lanes(s2, 8)) * SCALE
    c2, s12, s22 = _tile_lanes(c, 2), _tile_lanes(s1, 2), _tile_lanes(s2, 2)
    ksv = ksv_ref[...]
    kwv = kwv_ref[...]
    kvst_ref[:, 0:512] = kcv_ref[...]
    kvst_ref[:, 512:768] = _rope(ksv[:, :D_KV], c2, s12, s22)
    kvst_ref[:, 768:1024] = ksv[:, D_KV:]
    kvst_ref[:, 1024:1280] = _rope(kwv[:, :D_KV], c2, s12, s22)
    kvst_ref[:, 1280:1536] = kwv[:, D_KV:]
    g_ref[...] = jax.nn.sigmoid(ng_ref[...])


def _prep_sample(h, tabs):
    n = h.shape[0]
    c, s1, s2 = tabs
    tab = pl.BlockSpec((1, LANE), lambda i: (0, 0))
    return pl.pallas_call(
        _prep_s_kernel,
        grid=(1,),
        in_specs=[pl.BlockSpec((n, 1024), lambda i: (0, C_Q // 1024)),
                  pl.BlockSpec((n, 512), lambda i: (0, C_KV // 512)),
                  pl.BlockSpec((n, 512), lambda i: (0, C_KV // 512 + 1)),
                  pl.BlockSpec((n, 512), lambda i: (0, C_KV // 512 + 2)),
                  pl.BlockSpec((n, LANE), lambda i: (0, C_NG // LANE)),
                  tab, tab, tab],
        out_specs=[pl.BlockSpec((n, 1536), lambda i: (0, 0)),
                   pl.BlockSpec((n, 1024), lambda i: (0, 0)),
                   pl.BlockSpec((n, LANE), lambda i: (0, 0))],
        out_shape=[jax.ShapeDtypeStruct((n, 1536), F32),
                   jax.ShapeDtypeStruct((n, 1024), F32),
                   jax.ShapeDtypeStruct((n, LANE), F32)],
        compiler_params=_cparams(("arbitrary",)),
        name="prep_sample",
    )(h, h, h, h, h, c, s1, s2)


def _ab_s_kernel(pu_ref, pz_ref, cb_ref, cc_ref, cx_ref, cz_ref, ph_ref, ch_ref,
                 pw_ref, ps_ref, cw_ref, ya_ref, yb_ref, hc_ref, *, pos0):
    pu = pu_ref[...]
    acc = pu
    means = []
    k = 1
    for gi, w in enumerate(POOL_WINDOWS):
        ch = slice(gi * POOL_GROUP, (gi + 1) * POOL_GROUP)
        while k < w:
            acc = acc + ph_ref[POOL_HIST - k]
            k += 1
        means.append(acc[:, ch] / float(min(w, pos0 + 1)))
    pooled = jnp.concatenate(means, axis=1) - pu
    ya_ref[...] = _pool_group_mix(pooled, pw_ref, ps_ref[...], pz_ref[...]).astype(BF16)
    hc = cc_ref[...] * cx_ref[...]
    cw = cw_ref[...]
    conv = ch_ref[0] * cw[0:1, :]
    conv = conv + ch_ref[1] * cw[1:2, :]
    conv = conv + hc * cw[2:3, :]
    yb_ref[...] = ((cb_ref[...] * conv) * _silu(cz_ref[...])).astype(BF16)
    hc_ref[...] = hc


def _ab_sample(h, pool_hist_t, conv_hist_t, pw_bf, pscale, cw, pos0):
    n = h.shape[0]
    colb = lambda c: pl.BlockSpec((n, 512), lambda i: (0, c // 512))
    full = lambda shape: pl.BlockSpec(shape, lambda i: (0,) * len(shape))
    kern = functools.partial(_ab_s_kernel, pos0=pos0)
    return pl.pallas_call(
        kern,
        grid=(1,),
        in_specs=[colb(C_PU), colb(C_PZ), colb(C_CB), colb(C_CC), colb(C_CX), colb(C_CZ),
                  full((POOL_HIST, n, D_POOL)), full((CONV_WIDTH - 1, n, D_CONV)),
                  full((len(POOL_WINDOWS), POOL_GROUP, POOL_GROUP)), full((1, D_POOL)), full((CONV_WIDTH, D_CONV))],
        out_specs=[full((n, D_POOL)), full((n, D_CONV)), full((n, D_CONV))],
        out_shape=[jax.ShapeDtypeStruct((n, D_POOL), BF16),
                   jax.ShapeDtypeStruct((n, D_CONV), BF16),
                   jax.ShapeDtypeStruct((n, D_CONV), F32)],
        compiler_params=_cparams(("arbitrary",)),
        name="ab_sample",
    )(h, h, h, h, h, h, pool_hist_t, conv_hist_t, pw_bf, pscale, cw)


def _attn_s_kernel(pt_ref, *refs, n_pages, past_len, n_cmp, n_blk, wb):
    del pt_ref
    np_ = n_pages
    kc_pages = refs[0:np_]
    vc_pages = refs[np_:2 * np_]
    ks_pages = refs[2 * np_:3 * np_]
    vs_pages = refs[3 * np_:4 * np_]
    (kwin_ref, vwin_ref, qz_ref, new_ref, newc_ref, g_ref, ws_ref, pec_ref, c_ref, s1_ref, s2_ref,
     cov_ref, exp_ref, hm_ref, perm_ref, *rest) = refs[4 * np_:]
    o_ref, kwo_ref, vwo_ref, kx_ref, vx_ref = rest[-5:]
    qpos = past_len
    rows = past_len // np_
    qz = qz_ref[0]
    qzb = qz.astype(BF16)
    new = new_ref[0]
    hm = hm_ref[...]
    nch = past_len // CMP_STRIDE

    def page_t(r):
        return r[0, 0].reshape(D_KV, rows)

    perm = perm_ref[...]
    cpp = rows // CMP_STRIDE
    for p in range(np_):
        _gather_rows(perm, page_t(kc_pages[p]), kx_ref, p, cpp)
        _gather_rows(perm, page_t(vc_pages[p]), vx_ref, p, cpp)
    kc = _rope(_compress_rows(kx_ref, ws_ref, pec_ref, 0, nch), c_ref[...], s1_ref[...], s2_ref[...])
    vc = _compress_rows(vx_ref, ws_ref, pec_ref, 1, nch)
    ncp = kc.shape[0]
    s_c = _dot_nt(qzb, kc.astype(BF16))
    cidx = lax.broadcasted_iota(jnp.int32, (N_HEADS, ncp), 1)
    vis = (cidx * CMP_STRIDE + (CMP_LEN - 1) <= qpos) & (cidx < n_cmp)
    m_c = jnp.max(jnp.where(vis, s_c, NEG), axis=1, keepdims=True)
    e_c = jnp.where(vis, jnp.exp(s_c - m_c), 0.0)
    p_c = e_c / jnp.maximum(jnp.sum(e_c, axis=1, keepdims=True), 1e-30)
    p_cb = p_c.astype(BF16)
    o_cmp = _dot(p_cb, vc.astype(BF16))
    imp_h = _dot(p_cb, cov_ref[...])
    imp = jnp.concatenate(
        [jnp.sum(imp_h[g * GROUP:(g + 1) * GROUP, :], axis=0, keepdims=True) for g in range(N_KV)], axis=0)

    nbp = imp.shape[1]
    bidx = lax.broadcasted_iota(jnp.int32, (N_KV, nbp), 1)
    cur = qpos // SEL_LEN
    eligible = bidx * SEL_LEN <= qpos
    forced = (bidx == 0) | (bidx == cur) | (bidx == cur - 1)
    score = jnp.where(forced, FORCE_SCORE, jnp.where(eligible, imp, -jnp.inf))
    sel = _select_blocks(score, n_blk, 1)
    sel_h = jnp.concatenate(
        [jnp.broadcast_to(sel[g:g + 1, :], (GROUP, nbp)) for g in range(N_KV)], axis=0)
    kmask = _dot(sel_h.astype(BF16), exp_ref[...])

    def attend(k_t, v_t, ok, k_new, v_new):
        s = _dot(qzb, k_t)
        s_new = jnp.sum(qzb.astype(F32) * k_new.astype(BF16).astype(F32), axis=1, keepdims=True)
        m = jnp.maximum(jnp.max(jnp.where(ok, s, NEG), axis=1, keepdims=True), s_new)
        e = jnp.where(ok, jnp.exp(s - m), 0.0)
        e_new = jnp.exp(s_new - m)
        den = jnp.maximum(jnp.sum(e, axis=1, keepdims=True) + e_new, 1e-30)
        p = (e / den).astype(BF16)
        p_new = (e_new / den).astype(BF16).astype(F32)
        return _dot_nt(p, v_t) + p_new * v_new.astype(BF16).astype(F32)

    ks_t = jnp.concatenate([page_t(r).astype(BF16) for r in ks_pages], axis=1)
    vs_t = jnp.concatenate([page_t(r).astype(BF16) for r in vs_pages], axis=1)
    kpos = lax.broadcasted_iota(jnp.int32, (N_HEADS, past_len), 1)
    ok_s = (kmask > 0.5) & (kpos <= qpos)
    o_sel = attend(ks_t, vs_t, ok_s, new[0:1, :], new[1:2, :])

    kw_t = kwin_ref[0, 0].reshape(D_KV, wb)
    vw_t = vwin_ref[0, 0].reshape(D_KV, wb)
    wpos = (past_len - wb) + lax.broadcasted_iota(jnp.int32, (N_HEADS, wb), 1)
    ok_w = (wpos <= qpos) & (qpos - wpos < WINDOW)
    o_win = attend(kw_t.astype(BF16), vw_t.astype(BF16), ok_w, new[2:3, :], new[3:4, :])

    g = g_ref[0]
    o = g[:, 0:1] * o_cmp + g[:, 1:2] * o_sel + g[:, 2:3] * o_win
    o = o * hm
    o_ref[0] = (o[:, 0:64] + o[:, 64:128]) + (o[:, 128:192] + o[:, 192:256])

    cols = newc_ref[0]
    lane = lax.broadcasted_iota(jnp.int32, (D_KV, wb), 1)
    kwo = jnp.where(lane == wb - 1, cols[:, 2:3], pltpu.roll(kw_t, wb - 1, 1))
    vwo = jnp.where(lane == wb - 1, cols[:, 3:4], pltpu.roll(vw_t, wb - 1, 1))
    kwo_ref[0, 0] = kwo.reshape(N_KV, HEAD_DIM, wb)
    vwo_ref[0, 0] = vwo.reshape(N_KV, HEAD_DIM, wb)


def _attn_sample(page_table, kc_t, vc_t, ks_t, vs_t, kwin_t, vwin_t, qz, new_rows, gates, ws, pec, ctabs,
                 cov, expand, headmask, layer, past_len, n_cmp, n_blk, win_prev):
    n, n_pages = page_table.shape
    depth = kwin_t.shape[0]
    wbuf = kwin_t.shape[-1]
    rows_pp = ks_t.shape[-1]
    c, s1, s2 = ctabs
    ncp = c.shape[0]
    perm = _row_perm(rows_pp)

    def page_spec(p):
        return pl.BlockSpec((1, 1, N_KV, HEAD_DIM, rows_pp), lambda i, pt: (layer, pt[i, p], 0, 0, 0))

    full = lambda shape: pl.BlockSpec(shape, lambda i, pt: (0,) * len(shape))
    win_spec = pl.BlockSpec((1, 1, N_KV, HEAD_DIM, wbuf), lambda i, pt: (layer, i, 0, 0, 0))
    in_specs = ([page_spec(p) for p in range(n_pages)] * 4
                + [win_spec, win_spec,
                   pl.BlockSpec((1, N_HEADS, D_KV), lambda i, pt: (i, 0, 0)),
                   pl.BlockSpec((1, 8, D_KV), lambda i, pt: (i, 0, 0)),
                   pl.BlockSpec((1, D_KV, 8), lambda i, pt: (i, 0, 0)),
                   pl.BlockSpec((1, N_HEADS, LANE), lambda i, pt: (i, 0, 0)),
                   full(ws.shape), full(pec.shape),
                   full((ncp, D_KV)), full((ncp, D_KV)), full((ncp, D_KV)),
                   full(cov.shape), full(expand.shape), full(headmask.shape), full(perm.shape)])
    args = ([kc_t] * n_pages + [vc_t] * n_pages + [ks_t] * n_pages + [vs_t] * n_pages
            + [kwin_t, vwin_t, qz, new_rows, jnp.transpose(new_rows, (0, 2, 1)), gates, ws, pec, c, s1, s2,
               cov, expand, headmask, perm])
    aliases = {}
    if win_prev is not None:
        for j, a in enumerate(win_prev):
            aliases[1 + len(args)] = 1 + j
            args.append(a)
            in_specs.append(pl.BlockSpec(memory_space=pl.ANY))
    kern = functools.partial(_attn_s_kernel, n_pages=n_pages, past_len=past_len, n_cmp=n_cmp,
                             n_blk=n_blk, wb=wbuf)
    wout = pl.BlockSpec((1, 1, N_KV, HEAD_DIM, wbuf), lambda i, pt: (layer, i, 0, 0, 0))
    grid_spec = pltpu.PrefetchScalarGridSpec(
        num_scalar_prefetch=1, grid=(n,), in_specs=in_specs,
        out_specs=[pl.BlockSpec((1, N_HEADS, HEAD_DIM), lambda i, pt: (i, 0, 0)), wout, wout],
        scratch_shapes=[pltpu.VMEM((CMP_STRIDE, past_len // CMP_STRIDE, D_KV), F32),
                        pltpu.VMEM((CMP_STRIDE, past_len // CMP_STRIDE, D_KV), F32)])
    wshape = jax.ShapeDtypeStruct((depth, n, N_KV, HEAD_DIM, wbuf), F32)
    return pl.pallas_call(
        kern,
        grid_spec=grid_spec,
        out_shape=[jax.ShapeDtypeStruct((n, N_HEADS, HEAD_DIM), F32), wshape, wshape],
        input_output_aliases=aliases,
        compiler_params=_cparams(("arbitrary",)),
        name="attn_sample",
    )(page_table, *args)


def kernel(x_prompt, x_sample, cache_k_cmp, cache_v_cmp, cache_k_sel, cache_v_sel, page_table, state_k_win, state_v_win, state_pool, state_conv, w_in, pool_w, pool_scale, conv_w, cmp_pe, cmp_w, proj_a, proj_b, proj_c, w_out, ln_g, ln_b):
    b, t, _ = x_prompt.shape
    ns = x_sample.shape[0]
    assert x_sample.shape[1] == 1
    depth = w_in.shape[0]
    page = cache_k_cmp.shape[2]
    n_pages = page_table.shape[1]
    past_len = n_pages * page
    wbuf = state_k_win.shape[2]
    assert t % 512 == 0 and page % CMP_STRIDE == 0 and wbuf == WINDOW and past_len >= WINDOW

    n_cmp_p = (t - CMP_LEN) // CMP_STRIDE + 1
    n_blk_p = -(-t // SEL_LEN)
    nch_p = t // CMP_STRIDE
    tabs_p = _rope_tables(np.arange(t), LANE)
    ctabs_p = _rope_tables(np.arange(nch_p) * CMP_STRIDE + CMP_LEN - 1, D_KV)
    cov_p = jnp.asarray(_cover_t(nch_p, n_blk_p, n_cmp_p, n_blk_p), BF16)
    cb_p, tri_p = _attn_bias_tiles(t, nch_p, n_cmp_p)

    total_s = past_len + 1
    n_cmp_s = (total_s - CMP_LEN) // CMP_STRIDE + 1
    n_blk_s = -(-total_s // SEL_LEN)
    nch_s = past_len // CMP_STRIDE
    tabs_s = _rope_tables(np.asarray([past_len]), LANE)
    ctabs_s = _rope_tables(np.arange(nch_s) * CMP_STRIDE + CMP_LEN - 1, D_KV)
    cov_s = jnp.asarray(_cover_t(nch_s, LANE, n_cmp_s, n_blk_s).T, BF16)
    expand = jnp.asarray((np.arange(LANE)[:, None] == (np.arange(past_len)[None, :] // SEL_LEN)), BF16)
    headmask = jnp.asarray((np.arange(N_HEADS)[:, None] // GROUP) == (np.arange(D_KV)[None, :] // HEAD_DIM), F32)

    fm = lambda a: jnp.transpose(a, (0, 1, 3, 4, 2))
    kc_t, vc_t, ks_t, vs_t = fm(cache_k_cmp), fm(cache_v_cmp), fm(cache_k_sel), fm(cache_v_sel)
    kwin_t, vwin_t = fm(state_k_win), fm(state_v_win)

    xp = x_prompt.reshape(b * t, D_MODEL)
    xs = x_sample.reshape(ns, D_MODEL)
    xp_bf = xp.astype(BF16)
    xs_bf = xs.astype(BF16)
    st_p, st_s = [], []
    win_s = None
    st6 = None
    for l in range(depth):
        w_bf = _perm_w_in(w_in[l])
        ws, pec = _cmp_weights_rows(cmp_w[l], cmp_pe[l])
        pw_bf = pool_w[l].astype(BF16)
        pscale = pool_scale[l].reshape(1, D_POOL)
        cw = conv_w[l]
        pa, pb, pc, wo = (a[l].astype(BF16) for a in (proj_a, proj_b, proj_c, w_out))
        lg = ln_g[l].reshape(1, D_MODEL)
        lb = ln_b[l].reshape(1, D_MODEL)

        h = _inproj(xp_bf, w_bf, 2048)
        *st6, qt, ksa, kwg, vst, vwt, gt = _prep_prompt(h, tabs_p, b, t, l, depth, st6)
        kcg, vct = _compress_prompt(st6[0], st6[1], ws, pec, ctabs_p, l, b)
        yc = _attn_prompt(qt, kcg, vct, ksa, vst, kwg, vwt, gt, cov_p, cb_p, tri_p, b, t, n_blk_p)
        ya, yb, hct, put = _ab_prompt(h, pw_bf, pscale, cw, b, t)
        xp_new, xp_bf = _tail(ya, yb, yc, h, xp, pa, pb, pc, wo, lg, lb, 512)
        st_p.append((put[:, 16 - POOL_HIST:],
                     hct[:, 8 - (CONV_WIDTH - 1):]))
        xp = xp_new

        hs = _inproj(xs_bf, w_bf, ns)
        kvs, qrot, gates = _prep_sample(hs, tabs_s)
        q3 = qrot.reshape(ns, N_HEADS, 1, HEAD_DIM)
        qz = (q3 * jnp.eye(N_KV, dtype=F32)[jnp.arange(N_HEADS) // GROUP][None, :, :, None]).reshape(ns, N_HEADS, D_KV)
        new_rows = jnp.concatenate(
            [kvs[:, 2 * D_KV:6 * D_KV].reshape(ns, 4, D_KV), jnp.zeros((ns, 4, D_KV), F32)], axis=1)
        g4 = gates[:, :64].reshape(ns, N_KV, 4, GROUP)[:, :, :3, :]
        g_h = jnp.transpose(g4, (0, 1, 3, 2)).reshape(ns, N_HEADS, 3)
        g_h = jnp.concatenate([g_h, jnp.zeros((ns, N_HEADS, LANE - 3), F32)], axis=2)
        ycs, kwo, vwo = _attn_sample(page_table, kc_t, vc_t, ks_t, vs_t, kwin_t, vwin_t, qz, new_rows, g_h, ws, pec,
                                     ctabs_s, cov_s, expand, headmask, l, past_len, n_cmp_s, n_blk_s, win_s)
        win_s = (kwo, vwo)
        ycs = ycs.reshape(ns, D_ATTN)
        pool_hist = state_pool[l]
        conv_hist = state_conv[l]
        yas, ybs, hcs = _ab_sample(hs, jnp.transpose(pool_hist, (1, 0, 2)), jnp.transpose(conv_hist, (1, 0, 2)),
                                   pw_bf, pscale, cw, past_len)
        xs_new, xs_bf = _tail(yas, ybs, ycs, hs, xs, pa, pb, pc, wo, lg, lb, ns)
        kvs5 = kvs.reshape(ns, 1, 6, N_KV, HEAD_DIM)
        st_s.append((kvs5[:, :, 0], kvs5[:, :, 1], kvs5[:, :, 2], kvs5[:, :, 3],
                     jnp.concatenate([pool_hist, hs[:, None, C_PU:C_PU + D_POOL]], axis=1)[:, 1:],
                     jnp.concatenate([conv_hist, hcs[:, None, :]], axis=1)[:, 1:]))
        xs = xs_new

    stack = lambda states, i: jnp.stack([s[i] for s in states], axis=0)
    rm = lambda a: jnp.transpose(a.reshape(a.shape[0], a.shape[1], N_KV, HEAD_DIM, a.shape[3]), (0, 1, 4, 2, 3))
    rm5 = lambda a: jnp.transpose(a, (0, 1, 4, 2, 3))
    return ((xp.reshape(b, t, D_MODEL), xs.reshape(ns, 1, D_MODEL))
            + tuple(rm(a) for a in st6) + (stack(st_p, 0), stack(st_p, 1))
            + tuple(stack(st_s, i) for i in range(4)) + (rm5(win_s[0]), rm5(win_s[1]))
            + (stack(st_s, 4), stack(st_s, 5)))
```

```python
import functools

import numpy as np
import jax
import jax.numpy as jnp
from jax import lax
from jax.experimental import pallas as pl
from jax.experimental.pallas import tpu as pltpu

D_MODEL = 1024
DEPTH = 2
D_POOL = 512
POOL_WINDOWS = (2, 4, 8, 16)
POOL_GROUP = D_POOL // len(POOL_WINDOWS)
POOL_HIST = max(POOL_WINDOWS) - 1
D_CONV = 512
CONV_WIDTH = 3
HEAD_DIM = 64
N_HEADS = 16
N_KV = 4
GROUP = N_HEADS // N_KV
D_ATTN = N_HEADS * HEAD_DIM
D_KV = N_KV * HEAD_DIM
ROT_DIM = HEAD_DIM // 4
ROPE_THETA = 500000.0
CMP_LEN = 32
CMP_STRIDE = 16
SEL_LEN = 64
SEL_SHIFT = 6
N_SEL = 8
WINDOW = 512
FORCE_SCORE = 1.0e4
LN_EPS = 1e-5
ALPHA = (2 * DEPTH) ** 0.25
SCALE = HEAD_DIM ** -0.5
LOG2E = 1.4426950408889634

C_PU, C_PZ, C_CB, C_CC, C_CX, C_CZ = 0, 512, 1024, 1536, 2048, 2560
C_Q = 3072
C_AZ = 4096
C_MG = 5120
C_KV = 8192
C_NG = 9728
D_H = 9984
NG_PAD = 128

NEG = -1.0e30
LANE = 128
TQ = 512
KCH = 512
NSPLIT = 1
AUG = 16
VMEM_LIMIT = 48 * 1024 * 1024

BF16 = jnp.bfloat16
F32 = jnp.float32


def _cparams(sem):
    return pltpu.CompilerParams(dimension_semantics=sem, vmem_limit_bytes=VMEM_LIMIT)


def _dot(a, b):
    return jnp.dot(a, b, preferred_element_type=F32)


def _dot_nt(a, b):
    return lax.dot_general(a, b, (((1,), (1,)), ((), ())), preferred_element_type=F32)


def _silu(x):
    return x * jax.nn.sigmoid(x)


def _perm_w_in(w):
    ab = w[:, 0:3072]
    q = w[:, 3072:4096]
    kv = w[:, 4096:5632]
    ng = w[:, 5632:5680]
    az = w[:, 5680:6704]
    mg = w[:, 6704:9776]
    idx = np.full((NG_PAD,), 48, np.int32)
    for g in range(N_KV):
        for br in range(3):
            for r in range(GROUP):
                idx[g * 16 + br * 4 + r] = (GROUP * g + r) * 3 + br
    ng_ext = jnp.concatenate([ng, jnp.zeros((w.shape[0], 1), w.dtype)], axis=1)
    ng_p = jnp.take(ng_ext, jnp.asarray(idx), axis=1)
    pad = jnp.zeros((w.shape[0], D_H - C_NG - NG_PAD), w.dtype)
    return jnp.concatenate([ab, q, az, mg, kv, ng_p, pad], axis=1).astype(BF16)


def _rope_tables(pos, width):
    pos = np.asarray(pos, np.float32)
    p = pos.shape[0]
    inv_freq = (1.0 / (ROPE_THETA ** (np.arange(0, ROT_DIM, 2, dtype=np.float32) / ROT_DIM))).astype(np.float32)
    ang = (pos[:, None] * inv_freq[None, :]).astype(np.float32).astype(np.float64)
    cos = np.cos(ang).astype(np.float32)
    sin = np.sin(ang).astype(np.float32)
    z8 = np.zeros((p, 8), np.float32)
    z48 = np.zeros((p, 48), np.float32)
    c64 = np.concatenate([cos, cos, np.ones((p, 48), np.float32)], axis=1)
    s1 = np.concatenate([-sin, z8, z48], axis=1)
    s2 = np.concatenate([z8, sin, z48], axis=1)
    rep = width // HEAD_DIM
    return tuple(jnp.asarray(np.tile(t, (1, rep))) for t in (c64, s1, s2))


def _rope(x, c, s1, s2):
    w = x.shape[-1]
    up = pltpu.roll(x, w - 8, 1)
    dn = pltpu.roll(x, 8, 1)
    return x * c + up * s1 + dn * s2


def _tile_lanes(t, rep):
    return t if rep == 1 else jnp.concatenate([t] * rep, axis=1)


def _cmp_weights_rows(cmp_w_l, cmp_pe_l):
    eye2 = jnp.eye(2, dtype=F32)
    w = cmp_w_l.reshape(2, 2, CMP_STRIDE, HEAD_DIM, HEAD_DIM)
    ws = jnp.einsum('ktshe,gf->ksghtfe', w, eye2).reshape(2, CMP_STRIDE // 2, 4 * HEAD_DIM, 4 * HEAD_DIM)
    pec = jnp.einsum('kph,kphe->ke', cmp_pe_l, cmp_w_l.reshape(2, CMP_LEN, HEAD_DIM, HEAD_DIM),
                     precision=lax.Precision.HIGHEST)
    pec = jnp.tile(pec, (1, N_KV)).reshape(2, 1, D_KV)
    return ws.astype(BF16), pec


def _cover_t(n_cmp_pad, n_blk_pad, n_cmp, n_blk):
    c0 = np.arange(n_cmp_pad)[None, :] * CMP_STRIDE
    s0 = np.arange(n_blk_pad)[:, None] * SEL_LEN
    m = (c0 < s0 + SEL_LEN) & (c0 + CMP_LEN > s0)
    m &= (np.arange(n_cmp_pad)[None, :] < n_cmp) & (np.arange(n_blk_pad)[:, None] < n_blk)
    return m.astype(np.float32)


def _mm_kernel(x_ref, w_ref, o_ref):
    o_ref[...] = _dot(x_ref[...].astype(BF16), w_ref[...])


def _inproj(x_bf, w_bf, tm):
    m = x_bf.shape[0]
    tn = 768
    return pl.pallas_call(
        _mm_kernel,
        grid=(m // tm, D_H // tn),
        in_specs=[pl.BlockSpec((tm, D_MODEL), lambda i, j: (i, 0)),
                  pl.BlockSpec((D_MODEL, tn), lambda i, j: (0, j))],
        out_specs=pl.BlockSpec((tm, tn), lambda i, j: (i, j)),
        out_shape=jax.ShapeDtypeStruct((m, D_H), F32),
        compiler_params=_cparams(("arbitrary", "arbitrary")),
        name="inproj",
    )(x_bf, w_bf)


def _prep_p_kernel(q_ref, kcv_ref, ksv_ref, kwv_ref, ng_ref, c_ref, s1_ref, s2_ref,
                   *rest):
    (kct_ref, vct_ref, kstt_ref, vstt_ref, kwtt_ref, vwtt_ref,
     qt_ref, ksa_ref, kwg_ref, vst_ref, vwt_ref, gt_ref) = rest[-12:]
    tt = q_ref.shape[0]
    c, s1, s2 = c_ref[...], s1_ref[...], s2_ref[...]
    q = _rope(q_ref[...], _tile_lanes(c, 8), _tile_lanes(s1, 8), _tile_lanes(s2, 8)) * (SCALE * LOG2E)
    qt_ref[0] = q.T.astype(BF16)
    c2, s12, s22 = _tile_lanes(c, 2), _tile_lanes(s1, 2), _tile_lanes(s2, 2)
    kcv = kcv_ref[...]
    ksv = ksv_ref[...]
    kwv = kwv_ref[...]
    ks = _rope(ksv[:, :D_KV], c2, s12, s22)
    kw = _rope(kwv[:, :D_KV], c2, s12, s22)
    kcvt = kcv.T
    kct_ref[0, 0] = kcvt[:D_KV]
    vct_ref[0, 0] = kcvt[D_KV:]
    kst = ks.T
    vst = ksv[:, D_KV:].T
    kwt = kw.T
    vwt = kwv[:, D_KV:].T
    kstt_ref[0, 0] = kst
    vstt_ref[0, 0] = vst
    kwtt_ref[0, 0] = kwt
    vwtt_ref[0, 0] = vwt
    for j in range(tt // KCH):
        vst_ref[0, j] = vst[:, j * KCH:(j + 1) * KCH].astype(BF16)
        vwt_ref[0, j] = vwt[:, j * KCH:(j + 1) * KCH].astype(BF16)
    row = lax.broadcasted_iota(jnp.int32, (tt, LANE), 0)
    lane = lax.broadcasted_iota(jnp.int32, (tt, LANE), 1)
    blk = jnp.right_shift(jnp.bitwise_and(row, KCH - 1), SEL_SHIFT)
    onehot = jnp.where((lane >= HEAD_DIM) & (lane - HEAD_DIM == blk), 1.0, 0.0)
    zpad = jnp.zeros((tt, LANE - HEAD_DIM), F32)
    for g in range(N_KV):
        kg = ks[:, g * HEAD_DIM:(g + 1) * HEAD_DIM]
        ksa_ref[0, g] = (jnp.concatenate([kg, zpad], axis=1) + onehot).astype(BF16)
        kwg_ref[0, g] = kw[:, g * HEAD_DIM:(g + 1) * HEAD_DIM].astype(BF16)
    gt_ref[0] = jax.nn.sigmoid(ng_ref[...]).T[:64, :]


def _prep_prompt(h, tabs, b, t, layer, depth, prev):
    tt = 512
    assert min(WINDOW, t) == tt
    nt = t // tt
    row = lambda bi, ti: bi * nt + ti
    c, s1, s2 = tabs
    tab_spec = pl.BlockSpec((tt, LANE), lambda bi, ti: (ti, 0))
    in_specs = [pl.BlockSpec((tt, 1024), lambda bi, ti: (row(bi, ti), C_Q // 1024)),
                pl.BlockSpec((tt, 512), lambda bi, ti: (row(bi, ti), C_KV // 512)),
                pl.BlockSpec((tt, 512), lambda bi, ti: (row(bi, ti), C_KV // 512 + 1)),
                pl.BlockSpec((tt, 512), lambda bi, ti: (row(bi, ti), C_KV // 512 + 2)),
                pl.BlockSpec((tt, LANE), lambda bi, ti: (row(bi, ti), C_NG // LANE)),
                tab_spec, tab_spec, tab_spec]
    args = [h, h, h, h, h, c, s1, s2]
    aliases = {}
    if prev is not None:
        for j, a in enumerate(prev):
            aliases[len(args)] = j
            args.append(a)
            in_specs.append(pl.BlockSpec(memory_space=pl.ANY))
    full_t = pl.BlockSpec((1, 1, D_KV, tt), lambda bi, ti: (layer, bi, 0, ti))
    win_t = pl.BlockSpec((1, 1, D_KV, tt), lambda bi, ti: (layer, bi, 0, 0))
    st_full = jax.ShapeDtypeStruct((depth, b, D_KV, t), F32)
    st_win = jax.ShapeDtypeStruct((depth, b, D_KV, tt), F32)
    return pl.pallas_call(
        _prep_p_kernel,
        grid=(b, nt),
        in_specs=in_specs,
        out_specs=[full_t, full_t, full_t, full_t, win_t, win_t,
                   pl.BlockSpec((1, 1024, tt), lambda bi, ti: (bi, 0, ti)),
                   pl.BlockSpec((1, N_KV, tt, LANE), lambda bi, ti: (bi, 0, ti, 0)),
                   pl.BlockSpec((1, N_KV, tt, HEAD_DIM), lambda bi, ti: (bi, 0, ti, 0)),
                   pl.BlockSpec((1, tt // KCH, D_KV, KCH), lambda bi, ti: (bi, ti, 0, 0)),
                   pl.BlockSpec((1, tt // KCH, D_KV, KCH), lambda bi, ti: (bi, ti, 0, 0)),
                   pl.BlockSpec((1, 64, tt), lambda bi, ti: (bi, 0, ti))],
        out_shape=[st_full, st_full, st_full, st_full, st_win, st_win,
                   jax.ShapeDtypeStruct((b, 1024, t), BF16),
                   jax.ShapeDtypeStruct((b, N_KV, t, LANE), BF16),
                   jax.ShapeDtypeStruct((b, N_KV, t, HEAD_DIM), BF16),
                   jax.ShapeDtypeStruct((b, t // KCH, D_KV, KCH), BF16),
                   jax.ShapeDtypeStruct((b, t // KCH, D_KV, KCH), BF16),
                   jax.ShapeDtypeStruct((b, 64, t), F32)],
        input_output_aliases=aliases,
        compiler_params=_cparams(("arbitrary", "arbitrary")),
        name="prep_prompt",
    )(*args)


def _gather_rows(perm, page_t, x_ref, p, cpp):
    x = _dot_nt(perm, page_t.astype(BF16))
    for s_ in range(CMP_STRIDE):
        x_ref[s_, p * cpp:(p + 1) * cpp, :] = x[s_ * cpp:(s_ + 1) * cpp, :]


def _compress_rows(x_ref, ws_ref, pec_ref, kv, nch):
    halves = []
    for hf in range(2):
        ln = slice(hf * LANE, (hf + 1) * LANE)
        acc = jnp.zeros((nch, 2 * LANE), F32)
        for s2 in range(CMP_STRIDE // 2):
            x = jnp.concatenate([x_ref[2 * s2, :, ln], x_ref[2 * s2 + 1, :, ln]], axis=1).astype(BF16)
            acc = acc + _dot(x, ws_ref[kv, s2])
        bot_up = jnp.concatenate([acc[1:, LANE:], jnp.zeros((1, LANE), F32)], axis=0)
        halves.append(acc[:, :LANE] + bot_up)
    return jnp.concatenate(halves, axis=1) + pec_ref[kv]


def _row_perm(rows):
    cpp = rows // CMP_STRIDE
    i_ = np.arange(rows)
    return jnp.asarray(i_[None, :] == ((i_ % cpp) * CMP_STRIDE + i_ // cpp)[:, None], BF16)


def _compress_p_kernel(kt_ref, vt_ref, ws_ref, pec_ref, perm_ref, c_ref, s1_ref, s2_ref, kcg_ref, vct_ref,
                       kx_ref, vx_ref, *, rows):
    t = kt_ref.shape[-1]
    nch = t // CMP_STRIDE
    cpp = rows // CMP_STRIDE
    perm = perm_ref[...]
    for p in range(t // rows):
        _gather_rows(perm, kt_ref[0, 0, :, p * rows:(p + 1) * rows], kx_ref, p, cpp)
        _gather_rows(perm, vt_ref[0, 0, :, p * rows:(p + 1) * rows], vx_ref, p, cpp)
    kc = _rope(_compress_rows(kx_ref, ws_ref, pec_ref, 0, nch), c_ref[...], s1_ref[...], s2_ref[...])
    vc = _compress_rows(vx_ref, ws_ref, pec_ref, 1, nch)
    for g in range(N_KV):
        kcg_ref[0, g] = kc[:, g * HEAD_DIM:(g + 1) * HEAD_DIM].astype(BF16)
    vct_ref[0] = vc.T.astype(BF16)


def _compress_prompt(kct, vct_, ws, pec, ctabs, layer, b):
    t = kct.shape[-1]
    nch = t // CMP_STRIDE
    rows = LANE
    perm = _row_perm(rows)
    c, s1, s2 = ctabs
    full = lambda shape: pl.BlockSpec(shape, lambda bi: (0,) * len(shape))
    seq = pl.BlockSpec((1, 1, D_KV, t), lambda bi: (layer, bi, 0, 0))
    return pl.pallas_call(
        functools.partial(_compress_p_kernel, rows=rows),
        grid=(b,),
        in_specs=[seq, seq, full(ws.shape), full(pec.shape), full(perm.shape),
                  full((nch, D_KV)), full((nch, D_KV)), full((nch, D_KV))],
        out_specs=[pl.BlockSpec((1, N_KV, nch, HEAD_DIM), lambda bi: (bi, 0, 0, 0)),
                   pl.BlockSpec((1, D_KV, nch), lambda bi: (bi, 0, 0))],
        out_shape=[jax.ShapeDtypeStruct((b, N_KV, nch, HEAD_DIM), BF16),
                   jax.ShapeDtypeStruct((b, D_KV, nch), BF16)],
        scratch_shapes=[pltpu.VMEM((CMP_STRIDE, nch, D_KV), F32), pltpu.VMEM((CMP_STRIDE, nch, D_KV), F32)],
        compiler_params=_cparams(("arbitrary",)),
        name="compress_prompt",
    )(kct, vct_, ws, pec, perm, c, s1, s2)


def _select_blocks(score, n_blk, axis):
    idx = lax.broadcasted_iota(jnp.int32, score.shape, axis)
    rank = jnp.zeros(score.shape, F32)
    for j in range(n_blk):
        if axis == 0:
            sj = score[j:j + 1, :]
        else:
            sj = score[:, j:j + 1]
        beats = (sj > score) | ((sj == score) & (idx > j))
        rank = rank + jnp.where(beats, 1.0, 0.0)
    keep = (rank < float(N_SEL)) & (score > -jnp.inf) & (idx < n_blk)
    return jnp.where(keep, 1.0, 0.0)


def _select_rows(score, n_blk):
    nb = score.shape[0]
    idx = lax.broadcasted_iota(jnp.int32, score.shape, 0).astype(F32)
    cur = jnp.where(idx < float(n_blk), score, -jnp.inf)
    keep = jnp.zeros(score.shape, F32)
    for _ in range(N_SEL):
        mx = jnp.max(cur, axis=0, keepdims=True)
        first = jnp.min(jnp.where(cur == mx, idx, float(nb)), axis=0, keepdims=True)
        took = jnp.where(mx > -jnp.inf, 1.0, 0.0)
        keep = jnp.maximum(keep, jnp.where(idx == first, took, 0.0))
        cur = jnp.where(idx == first, -jnp.inf, cur)
    return keep


def _attn_p_kernel(qt_ref, kcg_ref, vct_ref, ksa_ref, vst_ref, kwg_ref, vwt_ref, gt_ref, cov_ref, cb_ref, tri_ref,
                   o_ref, selb_ref, m_ref, l_ref, acc_ref, comb_ref, sa_ref, sb_ref, wa_ref, *, n_blk):
    qi = pl.program_id(2)
    t0 = qi * TQ
    n = GROUP * TQ
    qt = qt_ref[0]
    qcat = jnp.concatenate([qt[r * HEAD_DIM:(r + 1) * HEAD_DIM, :] for r in range(GROUP)], axis=1)
    gt = gt_ref[0]

    def heads(x):
        return jnp.concatenate([x] * GROUP, axis=1)

    def gate(branch):
        return jnp.concatenate([gt[branch * GROUP + r:branch * GROUP + r + 1, :] for r in range(GROUP)], axis=1)

    def kchunk(k_ref, c):
        return k_ref[0, 0, pl.ds(pl.multiple_of(c * KCH, KCH), KCH), :]

    s = _dot(kcg_ref[0, 0], qcat) + heads(cb_ref[0])
    e = jnp.exp2(s - jnp.max(s, axis=0, keepdims=True))
    qpos = t0 + jnp.bitwise_and(lax.broadcasted_iota(jnp.int32, (1, n), 1), TQ - 1)
    inv = jnp.where(qpos >= CMP_LEN - 1, 1.0, 0.0) / jnp.sum(e, axis=0, keepdims=True)
    p = (e * inv).astype(BF16)
    comb_ref[...] = gate(0) * _dot(vct_ref[0], p)
    imp4 = _dot(cov_ref[...], p)
    imp = imp4[:, 0:TQ]
    for r in range(1, GROUP):
        imp = imp + imp4[:, r * TQ:(r + 1) * TQ]

    nb = imp.shape[0]
    per = KCH // SEL_LEN
    bidx = lax.broadcasted_iota(jnp.int32, (nb, TQ), 0)
    qp = t0 + lax.broadcasted_iota(jnp.int32, (nb, TQ), 1)
    cur = jnp.right_shift(qp, SEL_SHIFT)
    eligible = bidx * SEL_LEN <= qp
    forced = (bidx == 0) | (bidx == cur) | (bidx == cur - 1)
    score = jnp.where(forced, FORCE_SCORE, jnp.where(eligible, imp, -jnp.inf))
    selb = jnp.where(_select_rows(score, n_blk) > 0.5, 0.0, NEG)
    zrows = jnp.zeros((AUG - per, TQ), F32)
    for c in range(nb // per):
        selb_ref[c] = jnp.concatenate([selb[c * per:(c + 1) * per, :], zrows], axis=0).astype(BF16)

    def reset():
        m_ref[...] = jnp.full((1, n), NEG, F32)
        l_ref[...] = jnp.zeros((1, n), F32)
        acc_ref[...] = jnp.zeros((HEAD_DIM, n), F32)

    def scores(k_ref, c, aug):
        rhs = qcat
        if aug:
            rhs = jnp.concatenate([qcat, heads(selb_ref[c]), jnp.zeros((LANE - HEAD_DIM - AUG, n), BF16)], axis=0)
        return _dot(kchunk(k_ref, c), rhs)

    def update(s, vchunk):
        m_old = m_ref[...]
        m_new = jnp.maximum(m_old, jnp.max(s, axis=0, keepdims=True))
        alpha = jnp.exp2(m_old - m_new)
        e = jnp.exp2(s - m_new)
        l_ref[...] = alpha * l_ref[...] + jnp.sum(e, axis=0, keepdims=True)
        acc_ref[...] = alpha * acc_ref[...] + _dot(vchunk, e.astype(BF16))
        m_ref[...] = m_new

    def flush(branch):
        comb_ref[...] = comb_ref[...] + gate(branch) * (acc_ref[...] / jnp.maximum(l_ref[...], 1e-30))

    causal = lambda: heads(tri_ref[0])
    lowcut = lambda: heads(tri_ref[1])

    def wscores(c):
        return _dot(kchunk(kwg_ref, c), qcat)

    reset()
    sa_ref[...] = scores(ksa_ref, 0, True)

    def pair(i, carry):
        c = 2 * i
        sb_ref[...] = scores(ksa_ref, c + 1, True)
        update(sa_ref[...], vst_ref[0, c])
        sa_ref[...] = scores(ksa_ref, c + 2, True)
        update(sb_ref[...], vst_ref[0, c + 1])
        return carry

    lax.fori_loop(0, qi // 2, pair, 0)
    wfirst = jnp.maximum(qi - 1, 0)

    @pl.when(qi % 2 == 1)
    def _():
        sb_ref[...] = scores(ksa_ref, qi, True)
        update(sa_ref[...], vst_ref[0, qi - 1])
        wa_ref[...] = wscores(wfirst)
        update(sb_ref[...] + causal(), vst_ref[0, qi])

    @pl.when(qi % 2 == 0)
    def _():
        wa_ref[...] = wscores(wfirst)
        update(sa_ref[...] + causal(), vst_ref[0, qi])

    flush(1)

    reset()

    @pl.when(qi >= 1)
    def _():
        sb_ref[...] = wscores(qi)
        update(wa_ref[...] + lowcut(), vwt_ref[0, qi - 1])
        update(sb_ref[...] + causal(), vwt_ref[0, qi])

    @pl.when(qi == 0)
    def _():
        update(wa_ref[...] + causal(), vwt_ref[0, 0])

    flush(2)

    comb = comb_ref[...]
    o_ref[...] = jnp.concatenate([comb[:, r * TQ:(r + 1) * TQ].T for r in range(GROUP)], axis=1)


def _attn_prompt(qt, kcg, vct, ksa, vst, kwg, vwt, gt, cov, cb, tri, b, t, n_blk):
    nq = t // TQ
    nch = t // KCH
    ncp = kcg.shape[2]
    assert WINDOW == KCH and KCH == TQ and KCH // SEL_LEN <= AUG
    kern = functools.partial(_attn_p_kernel, n_blk=n_blk)
    n = GROUP * TQ
    return pl.pallas_call(
        kern,
        grid=(b, N_KV, nq),
        in_specs=[pl.BlockSpec((1, D_KV, TQ), lambda bi, g, qi: (bi, g, qi)),
                  pl.BlockSpec((1, 1, ncp, HEAD_DIM), lambda bi, g, qi: (bi, g, 0, 0)),
                  pl.BlockSpec((1, HEAD_DIM, ncp), lambda bi, g, qi: (bi, g, 0)),
                  pl.BlockSpec((1, 1, t, LANE), lambda bi, g, qi: (bi, g, 0, 0)),
                  pl.BlockSpec((1, nch, HEAD_DIM, KCH), lambda bi, g, qi: (bi, 0, g, 0)),
                  pl.BlockSpec((1, 1, t, HEAD_DIM), lambda bi, g, qi: (bi, g, 0, 0)),
                  pl.BlockSpec((1, nch, HEAD_DIM, KCH), lambda bi, g, qi: (bi, 0, g, 0)),
                  pl.BlockSpec((1, 16, TQ), lambda bi, g, qi: (bi, g, qi)),
                  pl.BlockSpec(cov.shape, lambda bi, g, qi: (0, 0)),
                  pl.BlockSpec((1, ncp, TQ), lambda bi, g, qi: (qi, 0, 0)),
                  pl.BlockSpec(tri.shape, lambda bi, g, qi: (0, 0, 0))],
        out_specs=pl.BlockSpec((TQ, D_KV), lambda bi, g, qi: (bi * nq + qi, g)),
        out_shape=jax.ShapeDtypeStruct((b * t, D_ATTN), F32),
        scratch_shapes=[pltpu.VMEM((cov.shape[0] * SEL_LEN // KCH, AUG, TQ), BF16),
                        pltpu.VMEM((1, n), F32), pltpu.VMEM((1, n), F32),
                        pltpu.VMEM((HEAD_DIM, n), F32), pltpu.VMEM((HEAD_DIM, n), F32),
                        pltpu.VMEM((KCH, n), F32), pltpu.VMEM((KCH, n), F32), pltpu.VMEM((KCH, n), F32)],
        compiler_params=_cparams(("arbitrary", "arbitrary", "arbitrary")),
        name="attn_prompt",
    )(qt, kcg, vct, ksa, vst, kwg, vwt, gt, cov, cb, tri)


def _attn_bias_tiles(t, ncp, n_cmp):
    k = np.arange(KCH)[:, None]
    q = np.arange(TQ)[None, :]
    tri = np.stack([np.where(k <= q, 0.0, NEG), np.where(k > q, 0.0, NEG), np.zeros((KCH, TQ))]).astype(np.float32)
    c = np.arange(ncp)[None, :, None]
    qpos = (np.arange(t // TQ)[:, None, None] * TQ) + np.arange(TQ)[None, None, :]
    cb = np.where((c * CMP_STRIDE + CMP_LEN - 1 <= qpos) & (c < n_cmp), 0.0, NEG).astype(np.float32)
    return jnp.asarray(cb), jnp.asarray(tri)


def _pool_group_mix(pooled, pw_ref, scale, pz):
    mixed = jnp.concatenate(
        [_dot(pooled[:, g * POOL_GROUP:(g + 1) * POOL_GROUP].astype(BF16), pw_ref[g])
         for g in range(len(POOL_WINDOWS))], axis=1)
    return (mixed * scale) * _silu(pz)


def _ab_p_kernel(pu_ref, pz_ref, cb_ref, cc_ref, cx_ref, cz_ref, pup_ref, ccp_ref, cxp_ref,
                 pw_ref, ps_ref, cw_ref, ya_ref, yb_ref, hct_ref, put_ref, *, tm, tiles_per_seq):
    i = pl.program_id(0)
    ti = i % tiles_per_seq
    valid = jnp.where(ti > 0, 1.0, 0.0)
    pu = pu_ref[...]
    ext = jnp.concatenate([pup_ref[...] * valid, pu], axis=0)
    pos = ti * tm + lax.broadcasted_iota(jnp.int32, (tm, POOL_GROUP), 0)
    acc = pu
    means = []
    k = 1
    for gi, w in enumerate(POOL_WINDOWS):
        ch = slice(gi * POOL_GROUP, (gi + 1) * POOL_GROUP)
        while k < w:
            acc = acc + ext[16 - k:16 - k + tm, :]
            k += 1
        count = jnp.minimum(w, pos + 1).astype(F32)
        means.append(acc[:, ch] / count)
    pooled = jnp.concatenate(means, axis=1) - pu
    ya_ref[...] = _pool_group_mix(pooled, pw_ref, ps_ref[...], pz_ref[...]).astype(BF16)

    hc = cc_ref[...] * cx_ref[...]
    hprev = (ccp_ref[...] * cxp_ref[...]) * valid
    hext = jnp.concatenate([hprev, hc], axis=0)
    cw = cw_ref[...]
    conv = hext[14:14 + tm, :] * cw[0:1, :]
    conv = conv + hext[15:15 + tm, :] * cw[1:2, :]
    conv = conv + hc * cw[2:3, :]
    yb_ref[...] = ((cb_ref[...] * conv) * _silu(cz_ref[...])).astype(BF16)
    hct_ref[0] = hc[tm - 8:tm, :]
    put_ref[0] = pu[tm - 16:tm, :]


def _ab_prompt(h, pw_bf, pscale, cw, b, t):
    tm = 512
    m = b * t
    tps = t // tm
    colb = lambda c: (lambda i: (i, c // 512))
    prev = lambda c: (lambda i: (jnp.maximum(i * (tm // 16) - 1, 0), c // 512))
    kern = functools.partial(_ab_p_kernel, tm=tm, tiles_per_seq=tps)
    full = lambda shape: pl.BlockSpec(shape, lambda i: (0,) * len(shape))
    return pl.pallas_call(
        kern,
        grid=(m // tm,),
        in_specs=[pl.BlockSpec((tm, 512), colb(C_PU)), pl.BlockSpec((tm, 512), colb(C_PZ)),
                  pl.BlockSpec((tm, 512), colb(C_CB)), pl.BlockSpec((tm, 512), colb(C_CC)),
                  pl.BlockSpec((tm, 512), colb(C_CX)), pl.BlockSpec((tm, 512), colb(C_CZ)),
                  pl.BlockSpec((16, 512), prev(C_PU)), pl.BlockSpec((16, 512), prev(C_CC)),
                  pl.BlockSpec((16, 512), prev(C_CX)),
                  full((len(POOL_WINDOWS), POOL_GROUP, POOL_GROUP)), full((1, D_POOL)), full((CONV_WIDTH, D_CONV))],
        out_specs=[pl.BlockSpec((tm, D_POOL), lambda i: (i, 0)),
                   pl.BlockSpec((tm, D_CONV), lambda i: (i, 0)),
                   pl.BlockSpec((1, 8, D_CONV), lambda i: (i // tps, 0, 0)),
                   pl.BlockSpec((1, 16, D_POOL), lambda i: (i // tps, 0, 0))],
        out_shape=[jax.ShapeDtypeStruct((m, D_POOL), BF16),
                   jax.ShapeDtypeStruct((m, D_CONV), BF16),
                   jax.ShapeDtypeStruct((b, 8, D_CONV), F32),
                   jax.ShapeDtypeStruct((b, 16, D_POOL), F32)],
        compiler_params=_cparams(("arbitrary",)),
        name="ab_prompt",
    )(h, h, h, h, h, h, h, h, h, pw_bf, pscale, cw)


def _tail_kernel(ya_ref, yb_ref, yc_ref, az_ref, ga_ref, gb_ref, gc_ref, x_ref,
                 pa_ref, pb_ref, pc_ref, wo_ref, lg_ref, lb_ref, o_ref, obf_ref):
    yc = (yc_ref[...] * _silu(az_ref[...])).astype(BF16)
    merged = (jax.nn.sigmoid(ga_ref[...]) * _dot(ya_ref[...], pa_ref[...])
              + jax.nn.sigmoid(gb_ref[...]) * _dot(yb_ref[...], pb_ref[...])
              + jax.nn.sigmoid(gc_ref[...]) * _dot(yc, pc_ref[...]))
    y = _dot(merged.astype(BF16), wo_ref[...])
    z = ALPHA * x_ref[...] + y
    mu = jnp.mean(z, axis=-1, keepdims=True)
    var = jnp.mean(jnp.square(z - mu), axis=-1, keepdims=True)
    out = (z - mu) * lax.rsqrt(var + LN_EPS) * lg_ref[...] + lb_ref[...]
    o_ref[...] = out
    obf_ref[...] = out.astype(BF16)


def _tail(ya, yb, yc, h, x, pa, pb, pc, wo, lg, lb, tm):
    m = x.shape[0]
    full = lambda shape: pl.BlockSpec(shape, lambda i: (0,) * len(shape))
    hcol = lambda c: pl.BlockSpec((tm, 1024), lambda i: (i, c // 1024))
    return pl.pallas_call(
        _tail_kernel,
        grid=(m // tm,),
        in_specs=[pl.BlockSpec((tm, D_POOL), lambda i: (i, 0)),
                  pl.BlockSpec((tm, D_CONV), lambda i: (i, 0)),
                  pl.BlockSpec((tm, D_ATTN), lambda i: (i, 0)),
                  hcol(C_AZ), hcol(C_MG), hcol(C_MG + 1024), hcol(C_MG + 2048),
                  pl.BlockSpec((tm, D_MODEL), lambda i: (i, 0)),
                  full((D_POOL, D_MODEL)), full((D_CONV, D_MODEL)), full((D_ATTN, D_MODEL)),
                  full((D_MODEL, D_MODEL)), full((1, D_MODEL)), full((1, D_MODEL))],
        out_specs=[pl.BlockSpec((tm, D_MODEL), lambda i: (i, 0)),
                   pl.BlockSpec((tm, D_MODEL), lambda i: (i, 0))],
        out_shape=[jax.ShapeDtypeStruct((m, D_MODEL), F32),
                   jax.ShapeDtypeStruct((m, D_MODEL), BF16)],
        compiler_params=_cparams(("arbitrary",)),
        name="tail",
    )(ya, yb, yc, h, h, h, h, x, pa, pb, pc, wo, lg, lb)


def _prep_s_kernel(q_ref, kcv_ref, ksv_ref, kwv_ref, ng_ref, c_ref, s1_ref, s2_ref,
                   kvst_ref, qr_ref, g_ref):
    c, s1, s2 = c_ref[...], s1_ref[...], s2_ref[...]
    qr_ref[...] = _rope(q_ref[...], _tile_lanes(c, 8), _tile_lanes(s1, 8), _tile_lanes(s2, 8)) * SCALE
    c2, s12, s22 = _tile_lanes(c, 2), _tile_lanes(s1, 2), _tile_lanes(s2, 2)
    ksv = ksv_ref[...]
    kwv = kwv_ref[...]
    kvst_ref[:, 0:512] = kcv_ref[...]
    kvst_ref[:, 512:768] = _rope(ksv[:, :D_KV], c2, s12, s22)
    kvst_ref[:, 768:1024] = ksv[:, D_KV:]
    kvst_ref[:, 1024:1280] = _rope(kwv[:, :D_KV], c2, s12, s22)
    kvst_ref[:, 1280:1536] = kwv[:, D_KV:]
    g_ref[...] = jax.nn.sigmoid(ng_ref[...])


def _prep_sample(h, tabs):
    n = h.shape[0]
    c, s1, s2 = tabs
    tab = pl.BlockSpec((1, LANE), lambda i: (0, 0))
    return pl.pallas_call(
        _prep_s_kernel,
        grid=(1,),
        in_specs=[pl.BlockSpec((n, 1024), lambda i: (0, C_Q // 1024)),
                  pl.BlockSpec((n, 512), lambda i: (0, C_KV // 512)),
                  pl.BlockSpec((n, 512), lambda i: (0, C_KV // 512 + 1)),
                  pl.BlockSpec((n, 512), lambda i: (0, C_KV // 512 + 2)),
                  pl.BlockSpec((n, LANE), lambda i: (0, C_NG // LANE)),
                  tab, tab, tab],
        out_specs=[pl.BlockSpec((n, 1536), lambda i: (0, 0)),
                   pl.BlockSpec((n, 1024), lambda i: (0, 0)),
                   pl.BlockSpec((n, LANE), lambda i: (0, 0))],
        out_shape=[jax.ShapeDtypeStruct((n, 1536), F32),
                   jax.ShapeDtypeStruct((n, 1024), F32),
                   jax.ShapeDtypeStruct((n, LANE), F32)],
        compiler_params=_cparams(("arbitrary",)),
        name="prep_sample",
    )(h, h, h, h, h, c, s1, s2)


def _ab_s_kernel(pu_ref, pz_ref, cb_ref, cc_ref, cx_ref, cz_ref, ph_ref, ch_ref,
                 pw_ref, ps_ref, cw_ref, ya_ref, yb_ref, hc_ref, *, pos0):
    pu = pu_ref[...]
    acc = pu
    means = []
    k = 1
    for gi, w in enumerate(POOL_WINDOWS):
        ch = slice(gi * POOL_GROUP, (gi + 1) * POOL_GROUP)
        while k < w:
            acc = acc + ph_ref[POOL_HIST - k]
            k += 1
        means.append(acc[:, ch] / float(min(w, pos0 + 1)))
    pooled = jnp.concatenate(means, axis=1) - pu
    ya_ref[...] = _pool_group_mix(pooled, pw_ref, ps_ref[...], pz_ref[...]).astype(BF16)
    hc = cc_ref[...] * cx_ref[...]
    cw = cw_ref[...]
    conv = ch_ref[0] * cw[0:1, :]
    conv = conv + ch_ref[1] * cw[1:2, :]
    conv = conv + hc * cw[2:3, :]
    yb_ref[...] = ((cb_ref[...] * conv) * _silu(cz_ref[...])).astype(BF16)
    hc_ref[...] = hc


def _ab_sample(h, pool_hist_t, conv_hist_t, pw_bf, pscale, cw, pos0):
    n = h.shape[0]
    colb = lambda c: pl.BlockSpec((n, 512), lambda i: (0, c // 512))
    full = lambda shape: pl.BlockSpec(shape, lambda i: (0,) * len(shape))
    kern = functools.partial(_ab_s_kernel, pos0=pos0)
    return pl.pallas_call(
        kern,
        grid=(1,),
        in_specs=[colb(C_PU), colb(C_PZ), colb(C_CB), colb(C_CC), colb(C_CX), colb(C_CZ),
                  full((POOL_HIST, n, D_POOL)), full((CONV_WIDTH - 1, n, D_CONV)),
                  full((len(POOL_WINDOWS), POOL_GROUP, POOL_GROUP)), full((1, D_POOL)), full((CONV_WIDTH, D_CONV))],
        out_specs=[full((n, D_POOL)), full((n, D_CONV)), full((n, D_CONV))],
        out_shape=[jax.ShapeDtypeStruct((n, D_POOL), BF16),
                   jax.ShapeDtypeStruct((n, D_CONV), BF16),
                   jax.ShapeDtypeStruct((n, D_CONV), F32)],
        compiler_params=_cparams(("arbitrary",)),
        name="ab_sample",
    )(h, h, h, h, h, h, pool_hist_t, conv_hist_t, pw_bf, pscale, cw)


def _attn_s_kernel(pt_ref, *refs, n_pages, past_len, n_cmp, n_blk, wb):
    del pt_ref
    np_ = n_pages
    kc_pages = refs[0:np_]
    vc_pages = refs[np_:2 * np_]
    ks_pages = refs[2 * np_:3 * np_]
    vs_pages = refs[3 * np_:4 * np_]
    (kwin_ref, vwin_ref, qz_ref, new_ref, newc_ref, g_ref, ws_ref, pec_ref, c_ref, s1_ref, s2_ref,
     cov_ref, exp_ref, hm_ref, perm_ref, *rest) = refs[4 * np_:]
    o_ref, kwo_ref, vwo_ref, kx_ref, vx_ref = rest[-5:]
    qpos = past_len
    rows = past_len // np_
    qz = qz_ref[0]
    qzb = qz.astype(BF16)
    new = new_ref[0]
    hm = hm_ref[...]
    nch = past_len // CMP_STRIDE

    def page_t(r):
        return r[0, 0].reshape(D_KV, rows)

    def scores(k_t, k_new):
        s = _dot(qzb, k_t)
        s_new = jnp.sum(qzb.astype(F32) * k_new.astype(BF16).astype(F32), axis=1, keepdims=True)
        return s, s_new

    def attend(sc, v_t, ok, v_new):
        s, s_new = sc
        m = jnp.maximum(jnp.max(jnp.where(ok, s, NEG), axis=1, keepdims=True), s_new)
        e = jnp.where(ok, jnp.exp(s - m), 0.0)
        e_new = jnp.exp(s_new - m)
        den = jnp.maximum(jnp.sum(e, axis=1, keepdims=True) + e_new, 1e-30)
        p = (e / den).astype(BF16)
        p_new = (e_new / den).astype(BF16).astype(F32)
        return _dot_nt(p, v_t) + p_new * v_new.astype(BF16).astype(F32)

    kw_t = kwin_ref[0, 0].reshape(D_KV, wb)
    vw_t = vwin_ref[0, 0].reshape(D_KV, wb)
    wpos = (past_len - wb) + lax.broadcasted_iota(jnp.int32, (N_HEADS, wb), 1)
    ok_w = (wpos <= qpos) & (qpos - wpos < WINDOW)
    o_win = attend(scores(kw_t.astype(BF16), new[2:3, :]), vw_t.astype(BF16), ok_w, new[3:4, :])
    ks_t = jnp.concatenate([page_t(r).astype(BF16) for r in ks_pages], axis=1)
    sc_sel = scores(ks_t, new[0:1, :])

    perm = perm_ref[...]
    cpp = rows // CMP_STRIDE
    for p in range(np_):
        _gather_rows(perm, page_t(kc_pages[p]), kx_ref, p, cpp)
        _gather_rows(perm, page_t(vc_pages[p]), vx_ref, p, cpp)
    kc = _rope(_compress_rows(kx_ref, ws_ref, pec_ref, 0, nch), c_ref[...], s1_ref[...], s2_ref[...])
    vc = _compress_rows(vx_ref, ws_ref, pec_ref, 1, nch)
    ncp = kc.shape[0]
    s_c = _dot_nt(qzb, kc.astype(BF16))
    cidx = lax.broadcasted_iota(jnp.int32, (N_HEADS, ncp), 1)
    vis = (cidx * CMP_STRIDE + (CMP_LEN - 1) <= qpos) & (cidx < n_cmp)
    m_c = jnp.max(jnp.where(vis, s_c, NEG), axis=1, keepdims=True)
    e_c = jnp.where(vis, jnp.exp(s_c - m_c), 0.0)
    p_c = e_c / jnp.maximum(jnp.sum(e_c, axis=1, keepdims=True), 1e-30)
    p_cb = p_c.astype(BF16)
    o_cmp = _dot(p_cb, vc.astype(BF16))
    imp_h = _dot(p_cb, cov_ref[...])
    imp = jnp.concatenate(
        [jnp.sum(imp_h[g * GROUP:(g + 1) * GROUP, :], axis=0, keepdims=True) for g in range(N_KV)], axis=0)

    nbp = imp.shape[1]
    bidx = lax.broadcasted_iota(jnp.int32, (N_KV, nbp), 1)
    cur = qpos // SEL_LEN
    eligible = bidx * SEL_LEN <= qpos
    forced = (bidx == 0) | (bidx == cur) | (bidx == cur - 1)
    score = jnp.where(forced, FORCE_SCORE, jnp.where(eligible, imp, -jnp.inf))
    sel = _select_blocks(score, n_blk, 1)
    sel_h = jnp.concatenate(
        [jnp.broadcast_to(sel[g:g + 1, :], (GROUP, nbp)) for g in range(N_KV)], axis=0)
    kmask = _dot(sel_h.astype(BF16), exp_ref[...])

    vs_t = jnp.concatenate([page_t(r).astype(BF16) for r in vs_pages], axis=1)
    kpos = lax.broadcasted_iota(jnp.int32, (N_HEADS, past_len), 1)
    ok_s = (kmask > 0.5) & (kpos <= qpos)
    o_sel = attend(sc_sel, vs_t, ok_s, new[1:2, :])

    g = g_ref[0]
    o = g[:, 0:1] * o_cmp + g[:, 1:2] * o_sel + g[:, 2:3] * o_win
    o = o * hm
    o_ref[0] = (o[:, 0:64] + o[:, 64:128]) + (o[:, 128:192] + o[:, 192:256])

    cols = newc_ref[0]
    lane = lax.broadcasted_iota(jnp.int32, (D_KV, wb), 1)
    kwo = jnp.where(lane == wb - 1, cols[:, 2:3], pltpu.roll(kw_t, wb - 1, 1))
    vwo = jnp.where(lane == wb - 1, cols[:, 3:4], pltpu.roll(vw_t, wb - 1, 1))
    kwo_ref[0, 0] = kwo.reshape(N_KV, HEAD_DIM, wb)
    vwo_ref[0, 0] = vwo.reshape(N_KV, HEAD_DIM, wb)


def _attn_sample(page_table, kc_t, vc_t, ks_t, vs_t, kwin_t, vwin_t, qz, new_rows, gates, ws, pec, ctabs,
                 cov, expand, headmask, layer, past_len, n_cmp, n_blk, win_prev):
    n, n_pages = page_table.shape
    depth = kwin_t.shape[0]
    wbuf = kwin_t.shape[-1]
    rows_pp = ks_t.shape[-1]
    c, s1, s2 = ctabs
    ncp = c.shape[0]
    perm = _row_perm(rows_pp)

    def page_spec(p):
        return pl.BlockSpec((1, 1, N_KV, HEAD_DIM, rows_pp), lambda i, pt: (layer, pt[i, p], 0, 0, 0))

    full = lambda shape: pl.BlockSpec(shape, lambda i, pt: (0,) * len(shape))
    win_spec = pl.BlockSpec((1, 1, N_KV, HEAD_DIM, wbuf), lambda i, pt: (layer, i, 0, 0, 0))
    in_specs = ([page_spec(p) for p in range(n_pages)] * 4
                + [win_spec, win_spec,
                   pl.BlockSpec((1, N_HEADS, D_KV), lambda i, pt: (i, 0, 0)),
                   pl.BlockSpec((1, 8, D_KV), lambda i, pt: (i, 0, 0)),
                   pl.BlockSpec((1, D_KV, 8), lambda i, pt: (i, 0, 0)),
                   pl.BlockSpec((1, N_HEADS, LANE), lambda i, pt: (i, 0, 0)),
                   full(ws.shape), full(pec.shape),
                   full((ncp, D_KV)), full((ncp, D_KV)), full((ncp, D_KV)),
                   full(cov.shape), full(expand.shape), full(headmask.shape), full(perm.shape)])
    args = ([kc_t] * n_pages + [vc_t] * n_pages + [ks_t] * n_pages + [vs_t] * n_pages
            + [kwin_t, vwin_t, qz, new_rows, jnp.transpose(new_rows, (0, 2, 1)), gates, ws, pec, c, s1, s2,
               cov, expand, headmask, perm])
    aliases = {}
    if win_prev is not None:
        for j, a in enumerate(win_prev):
            aliases[1 + len(args)] = 1 + j
            args.append(a)
            in_specs.append(pl.BlockSpec(memory_space=pl.ANY))
    kern = functools.partial(_attn_s_kernel, n_pages=n_pages, past_len=past_len, n_cmp=n_cmp,
                             n_blk=n_blk, wb=wbuf)
    wout = pl.BlockSpec((1, 1, N_KV, HEAD_DIM, wbuf), lambda i, pt: (layer, i, 0, 0, 0))
    grid_spec = pltpu.PrefetchScalarGridSpec(
        num_scalar_prefetch=1, grid=(n,), in_specs=in_specs,
        out_specs=[pl.BlockSpec((1, N_HEADS, HEAD_DIM), lambda i, pt: (i, 0, 0)), wout, wout],
        scratch_shapes=[pltpu.VMEM((CMP_STRIDE, past_len // CMP_STRIDE, D_KV), F32),
                        pltpu.VMEM((CMP_STRIDE, past_len // CMP_STRIDE, D_KV), F32)])
    wshape = jax.ShapeDtypeStruct((depth, n, N_KV, HEAD_DIM, wbuf), F32)
    return pl.pallas_call(
        kern,
        grid_spec=grid_spec,
        out_shape=[jax.ShapeDtypeStruct((n, N_HEADS, HEAD_DIM), F32), wshape, wshape],
        input_output_aliases=aliases,
        compiler_params=_cparams(("arbitrary",)),
        name="attn_sample",
    )(page_table, *args)


def kernel(x_prompt, x_sample, cache_k_cmp, cache_v_cmp, cache_k_sel, cache_v_sel, page_table, state_k_win, state_v_win, state_pool, state_conv, w_in, pool_w, pool_scale, conv_w, cmp_pe, cmp_w, proj_a, proj_b, proj_c, w_out, ln_g, ln_b):
    b, t, _ = x_prompt.shape
    ns = x_sample.shape[0]
    assert x_sample.shape[1] == 1
    depth = w_in.shape[0]
    page = cache_k_cmp.shape[2]
    n_pages = page_table.shape[1]
    past_len = n_pages * page
    wbuf = state_k_win.shape[2]
    assert t % 512 == 0 and page % CMP_STRIDE == 0 and wbuf == WINDOW and past_len >= WINDOW

    n_cmp_p = (t - CMP_LEN) // CMP_STRIDE + 1
    n_blk_p = -(-t // SEL_LEN)
    nch_p = t // CMP_STRIDE
    tabs_p = _rope_tables(np.arange(t), LANE)
    ctabs_p = _rope_tables(np.arange(nch_p) * CMP_STRIDE + CMP_LEN - 1, D_KV)
    cov_p = jnp.asarray(_cover_t(nch_p, n_blk_p, n_cmp_p, n_blk_p), BF16)
    cb_p, tri_p = _attn_bias_tiles(t, nch_p, n_cmp_p)

    total_s = past_len + 1
    n_cmp_s = (total_s - CMP_LEN) // CMP_STRIDE + 1
    n_blk_s = -(-total_s // SEL_LEN)
    nch_s = past_len // CMP_STRIDE
    tabs_s = _rope_tables(np.asarray([past_len]), LANE)
    ctabs_s = _rope_tables(np.arange(nch_s) * CMP_STRIDE + CMP_LEN - 1, D_KV)
    cov_s = jnp.asarray(_cover_t(nch_s, LANE, n_cmp_s, n_blk_s).T, BF16)
    expand = jnp.asarray((np.arange(LANE)[:, None] == (np.arange(past_len)[None, :] // SEL_LEN)), BF16)
    headmask = jnp.asarray((np.arange(N_HEADS)[:, None] // GROUP) == (np.arange(D_KV)[None, :] // HEAD_DIM), F32)

    fm = lambda a: jnp.transpose(a, (0, 1, 3, 4, 2))
    kc_t, vc_t, ks_t, vs_t = fm(cache_k_cmp), fm(cache_v_cmp), fm(cache_k_sel), fm(cache_v_sel)
    kwin_t, vwin_t = fm(state_k_win), fm(state_v_win)

    xp = x_prompt.reshape(b * t, D_MODEL)
    xs = x_sample.reshape(ns, D_MODEL)
    xp_bf, xs_bf = xp, xs
    st_p, st_s = [], []
    win_s = None
    st6 = None
    for l in range(depth):
        w_bf = _perm_w_in(w_in[l])
        ws, pec = _cmp_weights_rows(cmp_w[l], cmp_pe[l])
        pw_bf = pool_w[l].astype(BF16)
        pscale = pool_scale[l].reshape(1, D_POOL)
        cw = conv_w[l]
        pa, pb, pc, wo = (a[l].astype(BF16) for a in (proj_a, proj_b, proj_c, w_out))
        lg = ln_g[l].reshape(1, D_MODEL)
        lb = ln_b[l].reshape(1, D_MODEL)

        h = _inproj(xp_bf, w_bf, 2048)
        *st6, qt, ksa, kwg, vst, vwt, gt = _prep_prompt(h, tabs_p, b, t, l, depth, st6)
        kcg, vct = _compress_prompt(st6[0], st6[1], ws, pec, ctabs_p, l, b)
        yc = _attn_prompt(qt, kcg, vct, ksa, vst, kwg, vwt, gt, cov_p, cb_p, tri_p, b, t, n_blk_p)
        ya, yb, hct, put = _ab_prompt(h, pw_bf, pscale, cw, b, t)
        xp_new, xp_bf = _tail(ya, yb, yc, h, xp, pa, pb, pc, wo, lg, lb, 512)
        st_p.append((put[:, 16 - POOL_HIST:],
                     hct[:, 8 - (CONV_WIDTH - 1):]))
        xp = xp_new

        hs = _inproj(xs_bf, w_bf, ns)
        kvs, qrot, gates = _prep_sample(hs, tabs_s)
        q3 = qrot.reshape(ns, N_HEADS, 1, HEAD_DIM)
        qz = (q3 * jnp.eye(N_KV, dtype=F32)[jnp.arange(N_HEADS) // GROUP][None, :, :, None]).reshape(ns, N_HEADS, D_KV)
        new_rows = jnp.concatenate(
            [kvs[:, 2 * D_KV:6 * D_KV].reshape(ns, 4, D_KV), jnp.zeros((ns, 4, D_KV), F32)], axis=1)
        g4 = gates[:, :64].reshape(ns, N_KV, 4, GROUP)[:, :, :3, :]
        g_h = jnp.transpose(g4, (0, 1, 3, 2)).reshape(ns, N_HEADS, 3)
        g_h = jnp.concatenate([g_h, jnp.zeros((ns, N_HEADS, LANE - 3), F32)], axis=2)
        ycs, kwo, vwo = _attn_sample(page_table, kc_t, vc_t, ks_t, vs_t, kwin_t, vwin_t, qz, new_rows, g_h, ws, pec,
                                     ctabs_s, cov_s, expand, headmask, l, past_len, n_cmp_s, n_blk_s, win_s)
        win_s = (kwo, vwo)
        ycs = ycs.reshape(ns, D_ATTN)
        pool_hist = state_pool[l]
        conv_hist = state_conv[l]
        yas, ybs, hcs = _ab_sample(hs, jnp.transpose(pool_hist, (1, 0, 2)), jnp.transpose(conv_hist, (1, 0, 2)),
                                   pw_bf, pscale, cw, past_len)
        xs_new, xs_bf = _tail(yas, ybs, ycs, hs, xs, pa, pb, pc, wo, lg, lb, ns)
        kvs5 = kvs.reshape(ns, 1, 6, N_KV, HEAD_DIM)
        st_s.append((kvs5[:, :, 0], kvs5[:, :, 1], kvs5[:, :, 2], kvs5[:, :, 3],
                     jnp.concatenate([pool_hist, hs[:, None, C_PU:C_PU + D_POOL]], axis=1)[:, 1:],
                     jnp.concatenate([conv_hist, hcs[:, None, :]], axis=1)[:, 1:]))
        xs = xs_new

    stack = lambda states, i: jnp.stack([s[i] for s in states], axis=0)
    rm = lambda a: jnp.transpose(a.reshape(a.shape[0], a.shape[1], N_KV, HEAD_DIM, a.shape[3]), (0, 1, 4, 2, 3))
    rm5 = lambda a: jnp.transpose(a, (0, 1, 4, 2, 3))
    return ((xp.reshape(b, t, D_MODEL), xs.reshape(ns, 1, D_MODEL))
            + tuple(rm(a) for a in st6) + (stack(st_p, 0), stack(st_p, 1))
            + tuple(stack(st_s, i) for i in range(4)) + (rm5(win_s[0]), rm5(win_s[1]))
            + (stack(st_s, 4), stack(st_s, 5)))
```

```python
import functools

import numpy as np
import jax
import jax.numpy as jnp
from jax import lax
from jax.experimental import pallas as pl
from jax.experimental.pallas import tpu as pltpu

D_MODEL = 1024
DEPTH = 2
D_POOL = 512
POOL_WINDOWS = (2, 4, 8, 16)
POOL_GROUP = D_POOL // len(POOL_WINDOWS)
POOL_HIST = max(POOL_WINDOWS) - 1
D_CONV = 512
CONV_WIDTH = 3
HEAD_DIM = 64
N_HEADS = 16
N_KV = 4
GROUP = N_HEADS // N_KV
D_ATTN = N_HEADS * HEAD_DIM
D_KV = N_KV * HEAD_DIM
ROT_DIM = HEAD_DIM // 4
ROPE_THETA = 500000.0
CMP_LEN = 32
CMP_STRIDE = 16
SEL_LEN = 64
SEL_SHIFT = 6
N_SEL = 8
WINDOW = 512
FORCE_SCORE = 1.0e4
LN_EPS = 1e-5
ALPHA = (2 * DEPTH) ** 0.25
SCALE = HEAD_DIM ** -0.5
LOG2E = 1.4426950408889634

C_PU, C_PZ, C_CB, C_CC, C_CX, C_CZ = 0, 512, 1024, 1536, 2048, 2560
C_Q = 3072
C_AZ = 4096
C_MG = 5120
C_KV = 8192
C_NG = 9728
D_H = 9984
NG_PAD = 128

NEG = -1.0e30
LANE = 128
TQ = 512
KCH = 512
NSPLIT = 1
AUG = 16
VMEM_LIMIT = 48 * 1024 * 1024

BF16 = jnp.bfloat16
F32 = jnp.float32


def _cparams(sem):
    return pltpu.CompilerParams(dimension_semantics=sem, vmem_limit_bytes=VMEM_LIMIT)


def _dot(a, b):
    return jnp.dot(a, b, preferred_element_type=F32)


def _dot_nt(a, b):
    return lax.dot_general(a, b, (((1,), (1,)), ((), ())), preferred_element_type=F32)


def _silu(x):
    return x * jax.nn.sigmoid(x)


def _perm_w_in(w):
    ab = w[:, 0:3072]
    q = w[:, 3072:4096]
    kv = w[:, 4096:5632]
    ng = w[:, 5632:5680]
    az = w[:, 5680:6704]
    mg = w[:, 6704:9776]
    idx = np.full((NG_PAD,), 48, np.int32)
    for g in range(N_KV):
        for br in range(3):
            for r in range(GROUP):
                idx[g * 16 + br * 4 + r] = (GROUP * g + r) * 3 + br
    ng_ext = jnp.concatenate([ng, jnp.zeros((w.shape[0], 1), w.dtype)], axis=1)
    ng_p = jnp.take(ng_ext, jnp.asarray(idx), axis=1)
    pad = jnp.zeros((w.shape[0], D_H - C_NG - NG_PAD), w.dtype)
    return jnp.concatenate([ab, q, az, mg, kv, ng_p, pad], axis=1).astype(BF16)


def _rope_tables(pos, width, feature_major=False):
    pos = np.asarray(pos, np.float32)
    p = pos.shape[0]
    inv_freq = (1.0 / (ROPE_THETA ** (np.arange(0, ROT_DIM, 2, dtype=np.float32) / ROT_DIM))).astype(np.float32)
    ang = (pos[:, None] * inv_freq[None, :]).astype(np.float32).astype(np.float64)
    cos = np.cos(ang).astype(np.float32)
    sin = np.sin(ang).astype(np.float32)
    z8 = np.zeros((p, 8), np.float32)
    z48 = np.zeros((p, 48), np.float32)
    c64 = np.concatenate([cos, cos, np.ones((p, 48), np.float32)], axis=1)
    s1 = np.concatenate([-sin, z8, z48], axis=1)
    s2 = np.concatenate([z8, sin, z48], axis=1)
    rep = width // HEAD_DIM
    tabs = tuple(np.tile(t, (1, rep)) for t in (c64, s1, s2))
    return tuple(jnp.asarray(np.ascontiguousarray(t.T) if feature_major else t) for t in tabs)


def _rope(x, c, s1, s2):
    w = x.shape[-1]
    up = pltpu.roll(x, w - 8, 1)
    dn = pltpu.roll(x, 8, 1)
    return x * c + up * s1 + dn * s2


def _rope_t(xt, c, s1, s2):
    rep = xt.shape[0] // LANE
    tile = lambda t: t if rep == 1 else jnp.concatenate([t] * rep, axis=0)
    up = jnp.concatenate([xt[8:], xt[:8]], axis=0)
    dn = jnp.concatenate([xt[-8:], xt[:-8]], axis=0)
    return xt * tile(c) + up * tile(s1) + dn * tile(s2)


def _tile_lanes(t, rep):
    return t if rep == 1 else jnp.concatenate([t] * rep, axis=1)


def _cmp_weights_rows(cmp_w_l, cmp_pe_l):
    eye2 = jnp.eye(2, dtype=F32)
    w = cmp_w_l.reshape(2, 2, CMP_STRIDE, HEAD_DIM, HEAD_DIM)
    ws = jnp.einsum('ktshe,gf->ksghtfe', w, eye2).reshape(2, CMP_STRIDE // 2, 4 * HEAD_DIM, 4 * HEAD_DIM)
    pec = jnp.einsum('kph,kphe->ke', cmp_pe_l, cmp_w_l.reshape(2, CMP_LEN, HEAD_DIM, HEAD_DIM),
                     precision=lax.Precision.HIGHEST)
    pec = jnp.tile(pec, (1, N_KV)).reshape(2, 1, D_KV)
    return ws.astype(BF16), pec


def _cover_t(n_cmp_pad, n_blk_pad, n_cmp, n_blk):
    c0 = np.arange(n_cmp_pad)[None, :] * CMP_STRIDE
    s0 = np.arange(n_blk_pad)[:, None] * SEL_LEN
    m = (c0 < s0 + SEL_LEN) & (c0 + CMP_LEN > s0)
    m &= (np.arange(n_cmp_pad)[None, :] < n_cmp) & (np.arange(n_blk_pad)[:, None] < n_blk)
    return m.astype(np.float32)


def _mm_kernel(x_ref, w_ref, o_ref):
    o_ref[...] = _dot(x_ref[...].astype(BF16), w_ref[...])


def _inproj(x_bf, w_bf, tm):
    m = x_bf.shape[0]
    tn = 768
    return pl.pallas_call(
        _mm_kernel,
        grid=(m // tm, D_H // tn),
        in_specs=[pl.BlockSpec((tm, D_MODEL), lambda i, j: (i, 0)),
                  pl.BlockSpec((D_MODEL, tn), lambda i, j: (0, j))],
        out_specs=pl.BlockSpec((tm, tn), lambda i, j: (i, j)),
        out_shape=jax.ShapeDtypeStruct((m, D_H), F32),
        compiler_params=_cparams(("arbitrary", "arbitrary")),
        name="inproj",
    )(x_bf, w_bf)


def _prep_p_kernel(q_ref, kcv_ref, ksv_ref, kwv_ref, ng_ref, c_ref, s1_ref, s2_ref,
                   *rest):
    (kct_ref, vct_ref, kstt_ref, vstt_ref, kwtt_ref, vwtt_ref,
     qt_ref, ksa_ref, kwg_ref, vst_ref, vwt_ref, gt_ref) = rest[-12:]
    tt = q_ref.shape[0]
    c, s1, s2 = c_ref[...], s1_ref[...], s2_ref[...]
    qt_ref[0] = (_rope_t(q_ref[...].T, c, s1, s2) * (SCALE * LOG2E)).astype(BF16)
    kcv = kcv_ref[...]
    ksv = ksv_ref[...]
    kwv = kwv_ref[...]
    kcvt = kcv.T
    kct_ref[0, 0] = kcvt[:D_KV]
    vct_ref[0, 0] = kcvt[D_KV:]
    kst = _rope_t(ksv[:, :D_KV].T, c, s1, s2)
    vst = ksv[:, D_KV:].T
    kwt = _rope_t(kwv[:, :D_KV].T, c, s1, s2)
    vwt = kwv[:, D_KV:].T
    ks = kst.T
    kw = kwt.T
    kstt_ref[0, 0] = kst
    vstt_ref[0, 0] = vst
    kwtt_ref[0, 0] = kwt
    vwtt_ref[0, 0] = vwt
    for j in range(tt // KCH):
        vst_ref[0, j] = vst[:, j * KCH:(j + 1) * KCH].astype(BF16)
        vwt_ref[0, j] = vwt[:, j * KCH:(j + 1) * KCH].astype(BF16)
    row = lax.broadcasted_iota(jnp.int32, (tt, LANE), 0)
    lane = lax.broadcasted_iota(jnp.int32, (tt, LANE), 1)
    blk = jnp.right_shift(jnp.bitwise_and(row, KCH - 1), SEL_SHIFT)
    onehot = jnp.where((lane >= HEAD_DIM) & (lane - HEAD_DIM == blk), 1.0, 0.0)
    zpad = jnp.zeros((tt, LANE - HEAD_DIM), F32)
    for g in range(N_KV):
        kg = ks[:, g * HEAD_DIM:(g + 1) * HEAD_DIM]
        ksa_ref[0, g] = (jnp.concatenate([kg, zpad], axis=1) + onehot).astype(BF16)
        kwg_ref[0, g] = kw[:, g * HEAD_DIM:(g + 1) * HEAD_DIM].astype(BF16)
    gt_ref[0] = jax.nn.sigmoid(ng_ref[...]).T[:64, :]


def _prep_prompt(h, tabs, b, t, layer, depth, prev):
    tt = 512
    assert min(WINDOW, t) == tt
    nt = t // tt
    row = lambda bi, ti: bi * nt + ti
    c, s1, s2 = tabs
    tab_spec = pl.BlockSpec((LANE, tt), lambda bi, ti: (0, ti))
    in_specs = [pl.BlockSpec((tt, 1024), lambda bi, ti: (row(bi, ti), C_Q // 1024)),
                pl.BlockSpec((tt, 512), lambda bi, ti: (row(bi, ti), C_KV // 512)),
                pl.BlockSpec((tt, 512), lambda bi, ti: (row(bi, ti), C_KV // 512 + 1)),
                pl.BlockSpec((tt, 512), lambda bi, ti: (row(bi, ti), C_KV // 512 + 2)),
                pl.BlockSpec((tt, LANE), lambda bi, ti: (row(bi, ti), C_NG // LANE)),
                tab_spec, tab_spec, tab_spec]
    args = [h, h, h, h, h, c, s1, s2]
    aliases = {}
    if prev is not None:
        for j, a in enumerate(prev):
            aliases[len(args)] = j
            args.append(a)
            in_specs.append(pl.BlockSpec(memory_space=pl.ANY))
    full_t = pl.BlockSpec((1, 1, D_KV, tt), lambda bi, ti: (layer, bi, 0, ti))
    win_t = pl.BlockSpec((1, 1, D_KV, tt), lambda bi, ti: (layer, bi, 0, 0))
    st_full = jax.ShapeDtypeStruct((depth, b, D_KV, t), F32)
    st_win = jax.ShapeDtypeStruct((depth, b, D_KV, tt), F32)
    return pl.pallas_call(
        _prep_p_kernel,
        grid=(b, nt),
        in_specs=in_specs,
        out_specs=[full_t, full_t, full_t, full_t, win_t, win_t,
                   pl.BlockSpec((1, 1024, tt), lambda bi, ti: (bi, 0, ti)),
                   pl.BlockSpec((1, N_KV, tt, LANE), lambda bi, ti: (bi, 0, ti, 0)),
                   pl.BlockSpec((1, N_KV, tt, HEAD_DIM), lambda bi, ti: (bi, 0, ti, 0)),
                   pl.BlockSpec((1, tt // KCH, D_KV, KCH), lambda bi, ti: (bi, ti, 0, 0)),
                   pl.BlockSpec((1, tt // KCH, D_KV, KCH), lambda bi, ti: (bi, ti, 0, 0)),
                   pl.BlockSpec((1, 64, tt), lambda bi, ti: (bi, 0, ti))],
        out_shape=[st_full, st_full, st_full, st_full, st_win, st_win,
                   jax.ShapeDtypeStruct((b, 1024, t), BF16),
                   jax.ShapeDtypeStruct((b, N_KV, t, LANE), BF16),
                   jax.ShapeDtypeStruct((b, N_KV, t, HEAD_DIM), BF16),
                   jax.ShapeDtypeStruct((b, t // KCH, D_KV, KCH), BF16),
                   jax.ShapeDtypeStruct((b, t // KCH, D_KV, KCH), BF16),
                   jax.ShapeDtypeStruct((b, 64, t), F32)],
        input_output_aliases=aliases,
        compiler_params=_cparams(("arbitrary", "arbitrary")),
        name="prep_prompt",
    )(*args)


def _gather_rows(perm, page_t, x_ref, p, cpp):
    x = _dot_nt(perm, page_t.astype(BF16))
    for s_ in range(CMP_STRIDE):
        x_ref[s_, p * cpp:(p + 1) * cpp, :] = x[s_ * cpp:(s_ + 1) * cpp, :]


def _compress_rows(x_ref, ws_ref, pec_ref, kv, nch):
    halves = []
    for hf in range(2):
        ln = slice(hf * LANE, (hf + 1) * LANE)
        acc = jnp.zeros((nch, 2 * LANE), F32)
        for s2 in range(CMP_STRIDE // 2):
            x = jnp.concatenate([x_ref[2 * s2, :, ln], x_ref[2 * s2 + 1, :, ln]], axis=1).astype(BF16)
            acc = acc + _dot(x, ws_ref[kv, s2])
        bot_up = jnp.concatenate([acc[1:, LANE:], jnp.zeros((1, LANE), F32)], axis=0)
        halves.append(acc[:, :LANE] + bot_up)
    return jnp.concatenate(halves, axis=1) + pec_ref[kv]


def _row_perm(rows):
    cpp = rows // CMP_STRIDE
    i_ = np.arange(rows)
    return jnp.asarray(i_[None, :] == ((i_ % cpp) * CMP_STRIDE + i_ // cpp)[:, None], BF16)


def _compress_p_kernel(kt_ref, vt_ref, ws_ref, pec_ref, perm_ref, c_ref, s1_ref, s2_ref, kcg_ref, vct_ref,
                       kx_ref, vx_ref, *, rows):
    t = kt_ref.shape[-1]
    nch = t // CMP_STRIDE
    cpp = rows // CMP_STRIDE
    perm = perm_ref[...]
    for p in range(t // rows):
        _gather_rows(perm, kt_ref[0, 0, :, p * rows:(p + 1) * rows], kx_ref, p, cpp)
        _gather_rows(perm, vt_ref[0, 0, :, p * rows:(p + 1) * rows], vx_ref, p, cpp)
    kc = _rope(_compress_rows(kx_ref, ws_ref, pec_ref, 0, nch), c_ref[...], s1_ref[...], s2_ref[...])
    vc = _compress_rows(vx_ref, ws_ref, pec_ref, 1, nch)
    for g in range(N_KV):
        kcg_ref[0, g] = kc[:, g * HEAD_DIM:(g + 1) * HEAD_DIM].astype(BF16)
    vct_ref[0] = vc.T.astype(BF16)


def _compress_prompt(kct, vct_, ws, pec, ctabs, layer, b):
    t = kct.shape[-1]
    nch = t // CMP_STRIDE
    rows = LANE
    perm = _row_perm(rows)
    c, s1, s2 = ctabs
    full = lambda shape: pl.BlockSpec(shape, lambda bi: (0,) * len(shape))
    seq = pl.BlockSpec((1, 1, D_KV, t), lambda bi: (layer, bi, 0, 0))
    return pl.pallas_call(
        functools.partial(_compress_p_kernel, rows=rows),
        grid=(b,),
        in_specs=[seq, seq, full(ws.shape), full(pec.shape), full(perm.shape),
                  full((nch, D_KV)), full((nch, D_KV)), full((nch, D_KV))],
        out_specs=[pl.BlockSpec((1, N_KV, nch, HEAD_DIM), lambda bi: (bi, 0, 0, 0)),
                   pl.BlockSpec((1, D_KV, nch), lambda bi: (bi, 0, 0))],
        out_shape=[jax.ShapeDtypeStruct((b, N_KV, nch, HEAD_DIM), BF16),
                   jax.ShapeDtypeStruct((b, D_KV, nch), BF16)],
        scratch_shapes=[pltpu.VMEM((CMP_STRIDE, nch, D_KV), F32), pltpu.VMEM((CMP_STRIDE, nch, D_KV), F32)],
        compiler_params=_cparams(("arbitrary",)),
        name="compress_prompt",
    )(kct, vct_, ws, pec, perm, c, s1, s2)


def _select_blocks(score, n_blk, axis):
    idx = lax.broadcasted_iota(jnp.int32, score.shape, axis)
    rank = jnp.zeros(score.shape, F32)
    for j in range(n_blk):
        if axis == 0:
            sj = score[j:j + 1, :]
        else:
            sj = score[:, j:j + 1]
        beats = (sj > score) | ((sj == score) & (idx > j))
        rank = rank + jnp.where(beats, 1.0, 0.0)
    keep = (rank < float(N_SEL)) & (score > -jnp.inf) & (idx < n_blk)
    return jnp.where(keep, 1.0, 0.0)


def _select_rows(score, n_blk):
    nb = score.shape[0]
    idx = lax.broadcasted_iota(jnp.int32, score.shape, 0).astype(F32)
    cur = jnp.where(idx < float(n_blk), score, -jnp.inf)
    keep = jnp.zeros(score.shape, F32)
    for _ in range(N_SEL):
        mx = jnp.max(cur, axis=0, keepdims=True)
        first = jnp.min(jnp.where(cur == mx, idx, float(nb)), axis=0, keepdims=True)
        took = jnp.where(mx > -jnp.inf, 1.0, 0.0)
        keep = jnp.maximum(keep, jnp.where(idx == first, took, 0.0))
        cur = jnp.where(idx == first, -jnp.inf, cur)
    return keep


def _attn_p_kernel(qt_ref, kcg_ref, vct_ref, ksa_ref, vst_ref, kwg_ref, vwt_ref, gt_ref, cov_ref, cb_ref, tri_ref,
                   o_ref, selb_ref, m_ref, l_ref, acc_ref, comb_ref, sa_ref, sb_ref, wa_ref, *, n_blk):
    qi = pl.program_id(2)
    t0 = qi * TQ
    n = GROUP * TQ
    qt = qt_ref[0]
    qcat = jnp.concatenate([qt[r * HEAD_DIM:(r + 1) * HEAD_DIM, :] for r in range(GROUP)], axis=1)
    gt = gt_ref[0]

    def heads(x):
        return jnp.concatenate([x] * GROUP, axis=1)

    def gate(branch):
        return jnp.concatenate([gt[branch * GROUP + r:branch * GROUP + r + 1, :] for r in range(GROUP)], axis=1)

    def kchunk(k_ref, c):
        return k_ref[0, 0, pl.ds(pl.multiple_of(c * KCH, KCH), KCH), :]

    s = _dot(kcg_ref[0, 0], qcat) + heads(cb_ref[0])
    e = jnp.exp2(s - jnp.max(s, axis=0, keepdims=True))
    qpos = t0 + jnp.bitwise_and(lax.broadcasted_iota(jnp.int32, (1, n), 1), TQ - 1)
    inv = jnp.where(qpos >= CMP_LEN - 1, 1.0, 0.0) / jnp.sum(e, axis=0, keepdims=True)
    p = (e * inv).astype(BF16)
    comb_ref[...] = gate(0) * _dot(vct_ref[0], p)
    imp4 = _dot(cov_ref[...], p)
    imp = imp4[:, 0:TQ]
    for r in range(1, GROUP):
        imp = imp + imp4[:, r * TQ:(r + 1) * TQ]

    nb = imp.shape[0]
    per = KCH // SEL_LEN
    bidx = lax.broadcasted_iota(jnp.int32, (nb, TQ), 0)
    qp = t0 + lax.broadcasted_iota(jnp.int32, (nb, TQ), 1)
    cur = jnp.right_shift(qp, SEL_SHIFT)
    eligible = bidx * SEL_LEN <= qp
    forced = (bidx == 0) | (bidx == cur) | (bidx == cur - 1)
    score = jnp.where(forced, FORCE_SCORE, jnp.where(eligible, imp, -jnp.inf))
    selb = jnp.where(_select_rows(score, n_blk) > 0.5, 0.0, NEG)
    zrows = jnp.zeros((AUG - per, TQ), F32)
    for c in range(nb // per):
        selb_ref[c] = jnp.concatenate([selb[c * per:(c + 1) * per, :], zrows], axis=0).astype(BF16)

    def reset():
        m_ref[...] = jnp.full((1, n), NEG, F32)
        l_ref[...] = jnp.zeros((1, n), F32)
        acc_ref[...] = jnp.zeros((HEAD_DIM, n), F32)

    def scores(k_ref, c, aug):
        rhs = qcat
        if aug:
            rhs = jnp.concatenate([qcat, heads(selb_ref[c]), jnp.zeros((LANE - HEAD_DIM - AUG, n), BF16)], axis=0)
        return _dot(kchunk(k_ref, c), rhs)

    def update(s, vchunk):
        m_old = m_ref[...]
        m_new = jnp.maximum(m_old, jnp.max(s, axis=0, keepdims=True))
        alpha = jnp.exp2(m_old - m_new)
        e = jnp.exp2(s - m_new)
        l_ref[...] = alpha * l_ref[...] + jnp.sum(e, axis=0, keepdims=True)
        acc_ref[...] = alpha * acc_ref[...] + _dot(vchunk, e.astype(BF16))
        m_ref[...] = m_new

    def flush(branch):
        comb_ref[...] = comb_ref[...] + gate(branch) * (acc_ref[...] / jnp.maximum(l_ref[...], 1e-30))

    causal = lambda: heads(tri_ref[0])
    lowcut = lambda: heads(tri_ref[1])

    def wscores(c):
        return _dot(kchunk(kwg_ref, c), qcat)

    reset()
    sa_ref[...] = scores(ksa_ref, 0, True)

    def pair(i, carry):
        c = 2 * i
        sb_ref[...] = scores(ksa_ref, c + 1, True)
        update(sa_ref[...], vst_ref[0, c])
        sa_ref[...] = scores(ksa_ref, c + 2, True)
        update(sb_ref[...], vst_ref[0, c + 1])
        return carry

    lax.fori_loop(0, qi // 2, pair, 0)
    wfirst = jnp.maximum(qi - 1, 0)

    @pl.when(qi % 2 == 1)
    def _():
        sb_ref[...] = scores(ksa_ref, qi, True)
        update(sa_ref[...], vst_ref[0, qi - 1])
        wa_ref[...] = wscores(wfirst)
        update(sb_ref[...] + causal(), vst_ref[0, qi])

    @pl.when(qi % 2 == 0)
    def _():
        wa_ref[...] = wscores(wfirst)
        update(sa_ref[...] + causal(), vst_ref[0, qi])

    flush(1)

    reset()

    @pl.when(qi >= 1)
    def _():
        sb_ref[...] = wscores(qi)
        update(wa_ref[...] + lowcut(), vwt_ref[0, qi - 1])
        update(sb_ref[...] + causal(), vwt_ref[0, qi])

    @pl.when(qi == 0)
    def _():
        update(wa_ref[...] + causal(), vwt_ref[0, 0])

    flush(2)

    comb = comb_ref[...]
    o_ref[...] = jnp.concatenate([comb[:, r * TQ:(r + 1) * TQ].T for r in range(GROUP)], axis=1)


def _attn_prompt(qt, kcg, vct, ksa, vst, kwg, vwt, gt, cov, cb, tri, b, t, n_blk):
    nq = t // TQ
    nch = t // KCH
    ncp = kcg.shape[2]
    assert WINDOW == KCH and KCH == TQ and KCH // SEL_LEN <= AUG
    kern = functools.partial(_attn_p_kernel, n_blk=n_blk)
    n = GROUP * TQ
    return pl.pallas_call(
        kern,
        grid=(b, N_KV, nq),
        in_specs=[pl.BlockSpec((1, D_KV, TQ), lambda bi, g, qi: (bi, g, qi)),
                  pl.BlockSpec((1, 1, ncp, HEAD_DIM), lambda bi, g, qi: (bi, g, 0, 0)),
                  pl.BlockSpec((1, HEAD_DIM, ncp), lambda bi, g, qi: (bi, g, 0)),
                  pl.BlockSpec((1, 1, t, LANE), lambda bi, g, qi: (bi, g, 0, 0)),
                  pl.BlockSpec((1, nch, HEAD_DIM, KCH), lambda bi, g, qi: (bi, 0, g, 0)),
                  pl.BlockSpec((1, 1, t, HEAD_DIM), lambda bi, g, qi: (bi, g, 0, 0)),
                  pl.BlockSpec((1, nch, HEAD_DIM, KCH), lambda bi, g, qi: (bi, 0, g, 0)),
                  pl.BlockSpec((1, 16, TQ), lambda bi, g, qi: (bi, g, qi)),
                  pl.BlockSpec(cov.shape, lambda bi, g, qi: (0, 0)),
                  pl.BlockSpec((1, ncp, TQ), lambda bi, g, qi: (qi, 0, 0)),
                  pl.BlockSpec(tri.shape, lambda bi, g, qi: (0, 0, 0))],
        out_specs=pl.BlockSpec((TQ, D_KV), lambda bi, g, qi: (bi * nq + qi, g)),
        out_shape=jax.ShapeDtypeStruct((b * t, D_ATTN), F32),
        scratch_shapes=[pltpu.VMEM((cov.shape[0] * SEL_LEN // KCH, AUG, TQ), BF16),
                        pltpu.VMEM((1, n), F32), pltpu.VMEM((1, n), F32),
                        pltpu.VMEM((HEAD_DIM, n), F32), pltpu.VMEM((HEAD_DIM, n), F32),
                        pltpu.VMEM((KCH, n), F32), pltpu.VMEM((KCH, n), F32), pltpu.VMEM((KCH, n), F32)],
        compiler_params=_cparams(("arbitrary", "arbitrary", "arbitrary")),
        name="attn_prompt",
    )(qt, kcg, vct, ksa, vst, kwg, vwt, gt, cov, cb, tri)


def _attn_bias_tiles(t, ncp, n_cmp):
    k = np.arange(KCH)[:, None]
    q = np.arange(TQ)[None, :]
    tri = np.stack([np.where(k <= q, 0.0, NEG), np.where(k > q, 0.0, NEG), np.zeros((KCH, TQ))]).astype(np.float32)
    c = np.arange(ncp)[None, :, None]
    qpos = (np.arange(t // TQ)[:, None, None] * TQ) + np.arange(TQ)[None, None, :]
    cb = np.where((c * CMP_STRIDE + CMP_LEN - 1 <= qpos) & (c < n_cmp), 0.0, NEG).astype(np.float32)
    return jnp.asarray(cb), jnp.asarray(tri)


def _pool_group_mix(pooled, pw_ref, scale, pz):
    mixed = jnp.concatenate(
        [_dot(pooled[:, g * POOL_GROUP:(g + 1) * POOL_GROUP].astype(BF16), pw_ref[g])
         for g in range(len(POOL_WINDOWS))], axis=1)
    return (mixed * scale) * _silu(pz)


def _ab_p_kernel(pu_ref, pz_ref, cb_ref, cc_ref, cx_ref, cz_ref, pup_ref, ccp_ref, cxp_ref,
                 pw_ref, ps_ref, cw_ref, ya_ref, yb_ref, hct_ref, put_ref, *, tm, tiles_per_seq):
    i = pl.program_id(0)
    ti = i % tiles_per_seq
    valid = jnp.where(ti > 0, 1.0, 0.0)
    pu = pu_ref[...]
    ext = jnp.concatenate([pup_ref[...] * valid, pu], axis=0)
    pos = ti * tm + lax.broadcasted_iota(jnp.int32, (tm, POOL_GROUP), 0)
    acc = pu
    means = []
    k = 1
    for gi, w in enumerate(POOL_WINDOWS):
        ch = slice(gi * POOL_GROUP, (gi + 1) * POOL_GROUP)
        while k < w:
            acc = acc + ext[16 - k:16 - k + tm, :]
            k += 1
        count = jnp.minimum(w, pos + 1).astype(F32)
        means.append(acc[:, ch] / count)
    pooled = jnp.concatenate(means, axis=1) - pu
    ya_ref[...] = _pool_group_mix(pooled, pw_ref, ps_ref[...], pz_ref[...]).astype(BF16)

    hc = cc_ref[...] * cx_ref[...]
    hprev = (ccp_ref[...] * cxp_ref[...]) * valid
    hext = jnp.concatenate([hprev, hc], axis=0)
    cw = cw_ref[...]
    conv = hext[14:14 + tm, :] * cw[0:1, :]
    conv = conv + hext[15:15 + tm, :] * cw[1:2, :]
    conv = conv + hc * cw[2:3, :]
    yb_ref[...] = ((cb_ref[...] * conv) * _silu(cz_ref[...])).astype(BF16)
    hct_ref[0] = hc[tm - 8:tm, :]
    put_ref[0] = pu[tm - 16:tm, :]


def _ab_prompt(h, pw_bf, pscale, cw, b, t):
    tm = 512
    m = b * t
    tps = t // tm
    colb = lambda c: (lambda i: (i, c // 512))
    prev = lambda c: (lambda i: (jnp.maximum(i * (tm // 16) - 1, 0), c // 512))
    kern = functools.partial(_ab_p_kernel, tm=tm, tiles_per_seq=tps)
    full = lambda shape: pl.BlockSpec(shape, lambda i: (0,) * len(shape))
    return pl.pallas_call(
        kern,
        grid=(m // tm,),
        in_specs=[pl.BlockSpec((tm, 512), colb(C_PU)), pl.BlockSpec((tm, 512), colb(C_PZ)),
                  pl.BlockSpec((tm, 512), colb(C_CB)), pl.BlockSpec((tm, 512), colb(C_CC)),
                  pl.BlockSpec((tm, 512), colb(C_CX)), pl.BlockSpec((tm, 512), colb(C_CZ)),
                  pl.BlockSpec((16, 512), prev(C_PU)), pl.BlockSpec((16, 512), prev(C_CC)),
                  pl.BlockSpec((16, 512), prev(C_CX)),
                  full((len(POOL_WINDOWS), POOL_GROUP, POOL_GROUP)), full((1, D_POOL)), full((CONV_WIDTH, D_CONV))],
        out_specs=[pl.BlockSpec((tm, D_POOL), lambda i: (i, 0)),
                   pl.BlockSpec((tm, D_CONV), lambda i: (i, 0)),
                   pl.BlockSpec((1, 8, D_CONV), lambda i: (i // tps, 0, 0)),
                   pl.BlockSpec((1, 16, D_POOL), lambda i: (i // tps, 0, 0))],
        out_shape=[jax.ShapeDtypeStruct((m, D_POOL), BF16),
                   jax.ShapeDtypeStruct((m, D_CONV), BF16),
                   jax.ShapeDtypeStruct((b, 8, D_CONV), F32),
                   jax.ShapeDtypeStruct((b, 16, D_POOL), F32)],
        compiler_params=_cparams(("arbitrary",)),
        name="ab_prompt",
    )(h, h, h, h, h, h, h, h, h, pw_bf, pscale, cw)


def _tail_kernel(ya_ref, yb_ref, yc_ref, az_ref, ga_ref, gb_ref, gc_ref, x_ref,
                 pa_ref, pb_ref, pc_ref, wo_ref, lg_ref, lb_ref, o_ref, obf_ref):
    yc = (yc_ref[...] * _silu(az_ref[...])).astype(BF16)
    merged = (jax.nn.sigmoid(ga_ref[...]) * _dot(ya_ref[...], pa_ref[...])
              + jax.nn.sigmoid(gb_ref[...]) * _dot(yb_ref[...], pb_ref[...])
              + jax.nn.sigmoid(gc_ref[...]) * _dot(yc, pc_ref[...]))
    y = _dot(merged.astype(BF16), wo_ref[...])
    z = ALPHA * x_ref[...] + y
    mu = jnp.mean(z, axis=-1, keepdims=True)
    var = jnp.mean(jnp.square(z - mu), axis=-1, keepdims=True)
    out = (z - mu) * lax.rsqrt(var + LN_EPS) * lg_ref[...] + lb_ref[...]
    o_ref[...] = out
    obf_ref[...] = out.astype(BF16)


def _tail(ya, yb, yc, h, x, pa, pb, pc, wo, lg, lb, tm):
    m = x.shape[0]
    full = lambda shape: pl.BlockSpec(shape, lambda i: (0,) * len(shape))
    hcol = lambda c: pl.BlockSpec((tm, 1024), lambda i: (i, c // 1024))
    return pl.pallas_call(
        _tail_kernel,
        grid=(m // tm,),
        in_specs=[pl.BlockSpec((tm, D_POOL), lambda i: (i, 0)),
                  pl.BlockSpec((tm, D_CONV), lambda i: (i, 0)),
                  pl.BlockSpec((tm, D_ATTN), lambda i: (i, 0)),
                  hcol(C_AZ), hcol(C_MG), hcol(C_MG + 1024), hcol(C_MG + 2048),
                  pl.BlockSpec((tm, D_MODEL), lambda i: (i, 0)),
                  full((D_POOL, D_MODEL)), full((D_CONV, D_MODEL)), full((D_ATTN, D_MODEL)),
                  full((D_MODEL, D_MODEL)), full((1, D_MODEL)), full((1, D_MODEL))],
        out_specs=[pl.BlockSpec((tm, D_MODEL), lambda i: (i, 0)),
                   pl.BlockSpec((tm, D_MODEL), lambda i: (i, 0))],
        out_shape=[jax.ShapeDtypeStruct((m, D_MODEL), F32),
                   jax.ShapeDtypeStruct((m, D_MODEL), BF16)],
        compiler_params=_cparams(("arbitrary",)),
        name="tail",
    )(ya, yb, yc, h, h, h, h, x, pa, pb, pc, wo, lg, lb)


def _prep_s_kernel(q_ref, kcv_ref, ksv_ref, kwv_ref, ng_ref, c_ref, s1_ref, s2_ref,
                   kvst_ref, qr_ref, g_ref):
    c, s1, s2 = c_ref[...], s1_ref[...], s2_ref[...]
    qr_ref[...] = _rope(q_ref[...], _tile_lanes(c, 8), _tile_lanes(s1, 8), _tile_lanes(s2, 8)) * SCALE
    c2, s12, s22 = _tile_lanes(c, 2), _tile_lanes(s1, 2), _tile_lanes(s2, 2)
    ksv = ksv_ref[...]
    kwv = kwv_ref[...]
    kvst_ref[:, 0:512] = kcv_ref[...]
    kvst_ref[:, 512:768] = _rope(ksv[:, :D_KV], c2, s12, s22)
    kvst_ref[:, 768:1024] = ksv[:, D_KV:]
    kvst_ref[:, 1024:1280] = _rope(kwv[:, :D_KV], c2, s12, s22)
    kvst_ref[:, 1280:1536] = kwv[:, D_KV:]
    g_ref[...] = jax.nn.sigmoid(ng_ref[...])


def _prep_sample(h, tabs):
    n = h.shape[0]
    c, s1, s2 = tabs
    tab = pl.BlockSpec((1, LANE), lambda i: (0, 0))
    return pl.pallas_call(
        _prep_s_kernel,
        grid=(1,),
        in_specs=[pl.BlockSpec((n, 1024), lambda i: (0, C_Q // 1024)),
                  pl.BlockSpec((n, 512), lambda i: (0, C_KV // 512)),
                  pl.BlockSpec((n, 512), lambda i: (0, C_KV // 512 + 1)),
                  pl.BlockSpec((n, 512), lambda i: (0, C_KV // 512 + 2)),
                  pl.BlockSpec((n, LANE), lambda i: (0, C_NG // LANE)),
                  tab, tab, tab],
        out_specs=[pl.BlockSpec((n, 1536), lambda i: (0, 0)),
                   pl.BlockSpec((n, 1024), lambda i: (0, 0)),
                   pl.BlockSpec((n, LANE), lambda i: (0, 0))],
        out_shape=[jax.ShapeDtypeStruct((n, 1536), F32),
                   jax.ShapeDtypeStruct((n, 1024), F32),
                   jax.ShapeDtypeStruct((n, LANE), F32)],
        compiler_params=_cparams(("arbitrary",)),
        name="prep_sample",
    )(h, h, h, h, h, c, s1, s2)


def _ab_s_kernel(pu_ref, pz_ref, cb_ref, cc_ref, cx_ref, cz_ref, ph_ref, ch_ref,
                 pw_ref, ps_ref, cw_ref, ya_ref, yb_ref, hc_ref, *, pos0):
    pu = pu_ref[...]
    acc = pu
    means = []
    k = 1
    for gi, w in enumerate(POOL_WINDOWS):
        ch = slice(gi * POOL_GROUP, (gi + 1) * POOL_GROUP)
        while k < w:
            acc = acc + ph_ref[POOL_HIST - k]
            k += 1
        means.append(acc[:, ch] / float(min(w, pos0 + 1)))
    pooled = jnp.concatenate(means, axis=1) - pu
    ya_ref[...] = _pool_group_mix(pooled, pw_ref, ps_ref[...], pz_ref[...]).astype(BF16)
    hc = cc_ref[...] * cx_ref[...]
    cw = cw_ref[...]
    conv = ch_ref[0] * cw[0:1, :]
    conv = conv + ch_ref[1] * cw[1:2, :]
    conv = conv + hc * cw[2:3, :]
    yb_ref[...] = ((cb_ref[...] * conv) * _silu(cz_ref[...])).astype(BF16)
    hc_ref[...] = hc


def _ab_sample(h, pool_hist_t, conv_hist_t, pw_bf, pscale, cw, pos0):
    n = h.shape[0]
    colb = lambda c: pl.BlockSpec((n, 512), lambda i: (0, c // 512))
    full = lambda shape: pl.BlockSpec(shape, lambda i: (0,) * len(shape))
    kern = functools.partial(_ab_s_kernel, pos0=pos0)
    return pl.pallas_call(
        kern,
        grid=(1,),
        in_specs=[colb(C_PU), colb(C_PZ), colb(C_CB), colb(C_CC), colb(C_CX), colb(C_CZ),
                  full((POOL_HIST, n, D_POOL)), full((CONV_WIDTH - 1, n, D_CONV)),
                  full((len(POOL_WINDOWS), POOL_GROUP, POOL_GROUP)), full((1, D_POOL)), full((CONV_WIDTH, D_CONV))],
        out_specs=[full((n, D_POOL)), full((n, D_CONV)), full((n, D_CONV))],
        out_shape=[jax.ShapeDtypeStruct((n, D_POOL), BF16),
                   jax.ShapeDtypeStruct((n, D_CONV), BF16),
                   jax.ShapeDtypeStruct((n, D_CONV), F32)],
        compiler_params=_cparams(("arbitrary",)),
        name="ab_sample",
    )(h, h, h, h, h, h, pool_hist_t, conv_hist_t, pw_bf, pscale, cw)


def _attn_s_kernel(pt_ref, *refs, n_pages, past_len, n_cmp, n_blk, wb):
    del pt_ref
    np_ = n_pages
    kc_pages = refs[0:np_]
    vc_pages = refs[np_:2 * np_]
    ks_pages = refs[2 * np_:3 * np_]
    vs_pages = refs[3 * np_:4 * np_]
    (kwin_ref, vwin_ref, qz_ref, new_ref, newc_ref, g_ref, ws_ref, pec_ref, c_ref, s1_ref, s2_ref,
     cov_ref, exp_ref, hm_ref, perm_ref, *rest) = refs[4 * np_:]
    o_ref, kwo_ref, vwo_ref, kx_ref, vx_ref = rest[-5:]
    qpos = past_len
    rows = past_len // np_
    qz = qz_ref[0]
    qzb = qz.astype(BF16)
    new = new_ref[0]
    hm = hm_ref[...]
    nch = past_len // CMP_STRIDE

    def page_t(r):
        return r[0, 0].reshape(D_KV, rows)

    def scores(k_t, k_new):
        s = _dot(qzb, k_t)
        s_new = jnp.sum(qzb.astype(F32) * k_new.astype(BF16).astype(F32), axis=1, keepdims=True)
        return s, s_new

    def attend(sc, v_t, ok, v_new):
        s, s_new = sc
        m = jnp.maximum(jnp.max(jnp.where(ok, s, NEG), axis=1, keepdims=True), s_new)
        e = jnp.where(ok, jnp.exp(s - m), 0.0)
        e_new = jnp.exp(s_new - m)
        den = jnp.maximum(jnp.sum(e, axis=1, keepdims=True) + e_new, 1e-30)
        p = (e / den).astype(BF16)
        p_new = (e_new / den).astype(BF16).astype(F32)
        return _dot_nt(p, v_t) + p_new * v_new.astype(BF16).astype(F32)

    kw_t = kwin_ref[0, 0].reshape(D_KV, wb)
    vw_t = vwin_ref[0, 0].reshape(D_KV, wb)
    wpos = (past_len - wb) + lax.broadcasted_iota(jnp.int32, (N_HEADS, wb), 1)
    ok_w = (wpos <= qpos) & (qpos - wpos < WINDOW)
    o_win = attend(scores(kw_t.astype(BF16), new[2:3, :]), vw_t.astype(BF16), ok_w, new[3:4, :])
    ks_t = jnp.concatenate([page_t(r).astype(BF16) for r in ks_pages], axis=1)
    sc_sel = scores(ks_t, new[0:1, :])

    perm = perm_ref[...]
    cpp = rows // CMP_STRIDE
    for p in range(np_):
        _gather_rows(perm, page_t(kc_pages[p]), kx_ref, p, cpp)
        _gather_rows(perm, page_t(vc_pages[p]), vx_ref, p, cpp)
    kc = _rope(_compress_rows(kx_ref, ws_ref, pec_ref, 0, nch), c_ref[...], s1_ref[...], s2_ref[...])
    vc = _compress_rows(vx_ref, ws_ref, pec_ref, 1, nch)
    ncp = kc.shape[0]
    s_c = _dot_nt(qzb, kc.astype(BF16))
    cidx = lax.broadcasted_iota(jnp.int32, (N_HEADS, ncp), 1)
    vis = (cidx * CMP_STRIDE + (CMP_LEN - 1) <= qpos) & (cidx < n_cmp)
    m_c = jnp.max(jnp.where(vis, s_c, NEG), axis=1, keepdims=True)
    e_c = jnp.where(vis, jnp.exp(s_c - m_c), 0.0)
    p_c = e_c / jnp.maximum(jnp.sum(e_c, axis=1, keepdims=True), 1e-30)
    p_cb = p_c.astype(BF16)
    o_cmp = _dot(p_cb, vc.astype(BF16))
    imp_h = _dot(p_cb, cov_ref[...])
    imp = jnp.concatenate(
        [jnp.sum(imp_h[g * GROUP:(g + 1) * GROUP, :], axis=0, keepdims=True) for g in range(N_KV)], axis=0)

    nbp = imp.shape[1]
    bidx = lax.broadcasted_iota(jnp.int32, (N_KV, nbp), 1)
    cur = qpos // SEL_LEN
    eligible = bidx * SEL_LEN <= qpos
    forced = (bidx == 0) | (bidx == cur) | (bidx == cur - 1)
    score = jnp.where(forced, FORCE_SCORE, jnp.where(eligible, imp, -jnp.inf))
    sel = _select_blocks(score, n_blk, 1)
    sel_h = jnp.concatenate(
        [jnp.broadcast_to(sel[g:g + 1, :], (GROUP, nbp)) for g in range(N_KV)], axis=0)
    kmask = _dot(sel_h.astype(BF16), exp_ref[...])

    vs_t = jnp.concatenate([page_t(r).astype(BF16) for r in vs_pages], axis=1)
    kpos = lax.broadcasted_iota(jnp.int32, (N_HEADS, past_len), 1)
    ok_s = (kmask > 0.5) & (kpos <= qpos)
    o_sel = attend(sc_sel, vs_t, ok_s, new[1:2, :])

    g = g_ref[0]
    o = g[:, 0:1] * o_cmp + g[:, 1:2] * o_sel + g[:, 2:3] * o_win
    o = o * hm
    o_ref[0] = (o[:, 0:64] + o[:, 64:128]) + (o[:, 128:192] + o[:, 192:256])

    cols = newc_ref[0]
    lane = lax.broadcasted_iota(jnp.int32, (D_KV, wb), 1)
    kwo = jnp.where(lane == wb - 1, cols[:, 2:3], pltpu.roll(kw_t, wb - 1, 1))
    vwo = jnp.where(lane == wb - 1, cols[:, 3:4], pltpu.roll(vw_t, wb - 1, 1))
    kwo_ref[0, 0] = kwo.reshape(N_KV, HEAD_DIM, wb)
    vwo_ref[0, 0] = vwo.reshape(N_KV, HEAD_DIM, wb)


def _attn_sample(page_table, kc_t, vc_t, ks_t, vs_t, kwin_t, vwin_t, qz, new_rows, gates, ws, pec, ctabs,
                 cov, expand, headmask, layer, past_len, n_cmp, n_blk, win_prev):
    n, n_pages = page_table.shape
    depth = kwin_t.shape[0]
    wbuf = kwin_t.shape[-1]
    rows_pp = ks_t.shape[-1]
    c, s1, s2 = ctabs
    ncp = c.shape[0]
    perm = _row_perm(rows_pp)

    def page_spec(p):
        return pl.BlockSpec((1, 1, N_KV, HEAD_DIM, rows_pp), lambda i, pt: (layer, pt[i, p], 0, 0, 0))

    full = lambda shape: pl.BlockSpec(shape, lambda i, pt: (0,) * len(shape))
    win_spec = pl.BlockSpec((1, 1, N_KV, HEAD_DIM, wbuf), lambda i, pt: (layer, i, 0, 0, 0))
    in_specs = ([page_spec(p) for p in range(n_pages)] * 4
                + [win_spec, win_spec,
                   pl.BlockSpec((1, N_HEADS, D_KV), lambda i, pt: (i, 0, 0)),
                   pl.BlockSpec((1, 8, D_KV), lambda i, pt: (i, 0, 0)),
                   pl.BlockSpec((1, D_KV, 8), lambda i, pt: (i, 0, 0)),
                   pl.BlockSpec((1, N_HEADS, LANE), lambda i, pt: (i, 0, 0)),
                   full(ws.shape), full(pec.shape),
                   full((ncp, D_KV)), full((ncp, D_KV)), full((ncp, D_KV)),
                   full(cov.shape), full(expand.shape), full(headmask.shape), full(perm.shape)])
    args = ([kc_t] * n_pages + [vc_t] * n_pages + [ks_t] * n_pages + [vs_t] * n_pages
            + [kwin_t, vwin_t, qz, new_rows, jnp.transpose(new_rows, (0, 2, 1)), gates, ws, pec, c, s1, s2,
               cov, expand, headmask, perm])
    aliases = {}
    if win_prev is not None:
        for j, a in enumerate(win_prev):
            aliases[1 + len(args)] = 1 + j
            args.append(a)
            in_specs.append(pl.BlockSpec(memory_space=pl.ANY))
    kern = functools.partial(_attn_s_kernel, n_pages=n_pages, past_len=past_len, n_cmp=n_cmp,
                             n_blk=n_blk, wb=wbuf)
    wout = pl.BlockSpec((1, 1, N_KV, HEAD_DIM, wbuf), lambda i, pt: (layer, i, 0, 0, 0))
    grid_spec = pltpu.PrefetchScalarGridSpec(
        num_scalar_prefetch=1, grid=(n,), in_specs=in_specs,
        out_specs=[pl.BlockSpec((1, N_HEADS, HEAD_DIM), lambda i, pt: (i, 0, 0)), wout, wout],
        scratch_shapes=[pltpu.VMEM((CMP_STRIDE, past_len // CMP_STRIDE, D_KV), F32),
                        pltpu.VMEM((CMP_STRIDE, past_len // CMP_STRIDE, D_KV), F32)])
    wshape = jax.ShapeDtypeStruct((depth, n, N_KV, HEAD_DIM, wbuf), F32)
    return pl.pallas_call(
        kern,
        grid_spec=grid_spec,
        out_shape=[jax.ShapeDtypeStruct((n, N_HEADS, HEAD_DIM), F32), wshape, wshape],
        input_output_aliases=aliases,
        compiler_params=_cparams(("arbitrary",)),
        name="attn_sample",
    )(page_table, *args)


def kernel(x_prompt, x_sample, cache_k_cmp, cache_v_cmp, cache_k_sel, cache_v_sel, page_table, state_k_win, state_v_win, state_pool, state_conv, w_in, pool_w, pool_scale, conv_w, cmp_pe, cmp_w, proj_a, proj_b, proj_c, w_out, ln_g, ln_b):
    b, t, _ = x_prompt.shape
    ns = x_sample.shape[0]
    assert x_sample.shape[1] == 1
    depth = w_in.shape[0]
    page = cache_k_cmp.shape[2]
    n_pages = page_table.shape[1]
    past_len = n_pages * page
    wbuf = state_k_win.shape[2]
    assert t % 512 == 0 and page % CMP_STRIDE == 0 and wbuf == WINDOW and past_len >= WINDOW

    n_cmp_p = (t - CMP_LEN) // CMP_STRIDE + 1
    n_blk_p = -(-t // SEL_LEN)
    nch_p = t // CMP_STRIDE
    tabs_p = _rope_tables(np.arange(t), LANE, feature_major=True)
    ctabs_p = _rope_tables(np.arange(nch_p) * CMP_STRIDE + CMP_LEN - 1, D_KV)
    cov_p = jnp.asarray(_cover_t(nch_p, n_blk_p, n_cmp_p, n_blk_p), BF16)
    cb_p, tri_p = _attn_bias_tiles(t, nch_p, n_cmp_p)

    total_s = past_len + 1
    n_cmp_s = (total_s - CMP_LEN) // CMP_STRIDE + 1
    n_blk_s = -(-total_s // SEL_LEN)
    nch_s = past_len // CMP_STRIDE
    tabs_s = _rope_tables(np.asarray([past_len]), LANE)
    ctabs_s = _rope_tables(np.arange(nch_s) * CMP_STRIDE + CMP_LEN - 1, D_KV)
    cov_s = jnp.asarray(_cover_t(nch_s, LANE, n_cmp_s, n_blk_s).T, BF16)
    expand = jnp.asarray((np.arange(LANE)[:, None] == (np.arange(past_len)[None, :] // SEL_LEN)), BF16)
    headmask = jnp.asarray((np.arange(N_HEADS)[:, None] // GROUP) == (np.arange(D_KV)[None, :] // HEAD_DIM), F32)

    fm = lambda a: jnp.transpose(a, (0, 1, 3, 4, 2))
    kc_t, vc_t, ks_t, vs_t = fm(cache_k_cmp), fm(cache_v_cmp), fm(cache_k_sel), fm(cache_v_sel)
    kwin_t, vwin_t = fm(state_k_win), fm(state_v_win)

    xp = x_prompt.reshape(b * t, D_MODEL)
    xs = x_sample.reshape(ns, D_MODEL)
    xp_bf, xs_bf = xp, xs
    st_p, st_s = [], []
    win_s = None
    st6 = None
    for l in range(depth):
        w_bf = _perm_w_in(w_in[l])
        ws, pec = _cmp_weights_rows(cmp_w[l], cmp_pe[l])
        pw_bf = pool_w[l].astype(BF16)
        pscale = pool_scale[l].reshape(1, D_POOL)
        cw = conv_w[l]
        pa, pb, pc, wo = (a[l].astype(BF16) for a in (proj_a, proj_b, proj_c, w_out))
        lg = ln_g[l].reshape(1, D_MODEL)
        lb = ln_b[l].reshape(1, D_MODEL)

        h = _inproj(xp_bf, w_bf, 2048)
        *st6, qt, ksa, kwg, vst, vwt, gt = _prep_prompt(h, tabs_p, b, t, l, depth, st6)
        kcg, vct = _compress_prompt(st6[0], st6[1], ws, pec, ctabs_p, l, b)
        yc = _attn_prompt(qt, kcg, vct, ksa, vst, kwg, vwt, gt, cov_p, cb_p, tri_p, b, t, n_blk_p)
        ya, yb, hct, put = _ab_prompt(h, pw_bf, pscale, cw, b, t)
        xp_new, xp_bf = _tail(ya, yb, yc, h, xp, pa, pb, pc, wo, lg, lb, 512)
        st_p.append((put[:, 16 - POOL_HIST:],
                     hct[:, 8 - (CONV_WIDTH - 1):]))
        xp = xp_new

        hs = _inproj(xs_bf, w_bf, ns)
        kvs, qrot, gates = _prep_sample(hs, tabs_s)
        q3 = qrot.reshape(ns, N_HEADS, 1, HEAD_DIM)
        qz = (q3 * jnp.eye(N_KV, dtype=F32)[jnp.arange(N_HEADS) // GROUP][None, :, :, None]).reshape(ns, N_HEADS, D_KV)
        new_rows = jnp.concatenate(
            [kvs[:, 2 * D_KV:6 * D_KV].reshape(ns, 4, D_KV), jnp.zeros((ns, 4, D_KV), F32)], axis=1)
        g4 = gates[:, :64].reshape(ns, N_KV, 4, GROUP)[:, :, :3, :]
        g_h = jnp.transpose(g4, (0, 1, 3, 2)).reshape(ns, N_HEADS, 3)
        g_h = jnp.concatenate([g_h, jnp.zeros((ns, N_HEADS, LANE - 3), F32)], axis=2)
        ycs, kwo, vwo = _attn_sample(page_table, kc_t, vc_t, ks_t, vs_t, kwin_t, vwin_t, qz, new_rows, g_h, ws, pec,
                                     ctabs_s, cov_s, expand, headmask, l, past_len, n_cmp_s, n_blk_s, win_s)
        win_s = (kwo, vwo)
        ycs = ycs.reshape(ns, D_ATTN)
        pool_hist = state_pool[l]
        conv_hist = state_conv[l]
        yas, ybs, hcs = _ab_sample(hs, jnp.transpose(pool_hist, (1, 0, 2)), jnp.transpose(conv_hist, (1, 0, 2)),
                                   pw_bf, pscale, cw, past_len)
        xs_new, xs_bf = _tail(yas, ybs, ycs, hs, xs, pa, pb, pc, wo, lg, lb, ns)
        kvs5 = kvs.reshape(ns, 1, 6, N_KV, HEAD_DIM)
        st_s.append((kvs5[:, :, 0], kvs5[:, :, 1], kvs5[:, :, 2], kvs5[:, :, 3],
                     jnp.concatenate([pool_hist, hs[:, None, C_PU:C_PU + D_POOL]], axis=1)[:, 1:],
                     jnp.concatenate([conv_hist, hcs[:, None, :]], axis=1)[:, 1:]))
        xs = xs_new

    stack = lambda states, i: jnp.stack([s[i] for s in states], axis=0)
    rm = lambda a: jnp.transpose(a.reshape(a.shape[0], a.shape[1], N_KV, HEAD_DIM, a.shape[3]), (0, 1, 4, 2, 3))
    rm5 = lambda a: jnp.transpose(a, (0, 1, 4, 2, 3))
    return ((xp.reshape(b, t, D_MODEL), xs.reshape(ns, 1, D_MODEL))
            + tuple(rm(a) for a in st6) + (stack(st_p, 0), stack(st_p, 1))
            + tuple(stack(st_s, i) for i in range(4)) + (rm5(win_s[0]), rm5(win_s[1]))
            + (stack(st_s, 4), stack(st_s, 5)))
```

```python
import functools

import numpy as np
import jax
import jax.numpy as jnp
from jax import lax
from jax.experimental import pallas as pl
from jax.experimental.pallas import tpu as pltpu

D_MODEL = 1024
DEPTH = 2
D_POOL = 512
POOL_WINDOWS = (2, 4, 8, 16)
POOL_GROUP = D_POOL // len(POOL_WINDOWS)
POOL_HIST = max(POOL_WINDOWS) - 1
D_CONV = 512
CONV_WIDTH = 3
HEAD_DIM = 64
N_HEADS = 16
N_KV = 4
GROUP = N_HEADS // N_KV
D_ATTN = N_HEADS * HEAD_DIM
D_KV = N_KV * HEAD_DIM
ROT_DIM = HEAD_DIM // 4
ROPE_THETA = 500000.0
CMP_LEN = 32
CMP_STRIDE = 16
SEL_LEN = 64
SEL_SHIFT = 6
N_SEL = 8
WINDOW = 512
FORCE_SCORE = 1.0e4
LN_EPS = 1e-5
ALPHA = (2 * DEPTH) ** 0.25
SCALE = HEAD_DIM ** -0.5
LOG2E = 1.4426950408889634

C_PU, C_PZ, C_CB, C_CC, C_CX, C_CZ = 0, 512, 1024, 1536, 2048, 2560
C_Q = 3072
C_AZ = 4096
C_MG = 5120
C_KV = 8192
C_NG = 9728
D_H = 9984
NG_PAD = 128

NEG = -1.0e30
LANE = 128
TQ = 512
KCH = 512
NSPLIT = 1
AUG = 16
VMEM_LIMIT = 48 * 1024 * 1024

BF16 = jnp.bfloat16
F32 = jnp.float32


def _cparams(sem):
    return pltpu.CompilerParams(dimension_semantics=sem, vmem_limit_bytes=VMEM_LIMIT)


def _dot(a, b):
    return jnp.dot(a, b, preferred_element_type=F32)


def _dot_nt(a, b):
    return lax.dot_general(a, b, (((1,), (1,)), ((), ())), preferred_element_type=F32)


def _silu(x):
    return x * jax.nn.sigmoid(x)


def _perm_w_in(w):
    ab = w[:, 0:3072]
    q = w[:, 3072:4096]
    kv = w[:, 4096:5632]
    ng = w[:, 5632:5680]
    az = w[:, 5680:6704]
    mg = w[:, 6704:9776]
    idx = np.full((NG_PAD,), 48, np.int32)
    for g in range(N_KV):
        for br in range(3):
            for r in range(GROUP):
                idx[g * 16 + br * 4 + r] = (GROUP * g + r) * 3 + br
    ng_ext = jnp.concatenate([ng, jnp.zeros((w.shape[0], 1), w.dtype)], axis=1)
    ng_p = jnp.take(ng_ext, jnp.asarray(idx), axis=1)
    pad = jnp.zeros((w.shape[0], D_H - C_NG - NG_PAD), w.dtype)
    return jnp.concatenate([ab, q, az, mg, kv, ng_p, pad], axis=1).astype(BF16)


def _rope_tables(pos, width, feature_major=False):
    pos = np.asarray(pos, np.float32)
    p = pos.shape[0]
    inv_freq = (1.0 / (ROPE_THETA ** (np.arange(0, ROT_DIM, 2, dtype=np.float32) / ROT_DIM))).astype(np.float32)
    ang = (pos[:, None] * inv_freq[None, :]).astype(np.float32).astype(np.float64)
    cos = np.cos(ang).astype(np.float32)
    sin = np.sin(ang).astype(np.float32)
    z8 = np.zeros((p, 8), np.float32)
    z48 = np.zeros((p, 48), np.float32)
    c64 = np.concatenate([cos, cos, np.ones((p, 48), np.float32)], axis=1)
    s1 = np.concatenate([-sin, z8, z48], axis=1)
    s2 = np.concatenate([z8, sin, z48], axis=1)
    rep = width // HEAD_DIM
    tabs = tuple(np.tile(t, (1, rep)) for t in (c64, s1, s2))
    return tuple(jnp.asarray(np.ascontiguousarray(t.T) if feature_major else t) for t in tabs)


def _rope(x, c, s1, s2):
    w = x.shape[-1]
    up = pltpu.roll(x, w - 8, 1)
    dn = pltpu.roll(x, 8, 1)
    return x * c + up * s1 + dn * s2


def _rope_t(xt, c, s1, s2):
    rep = xt.shape[0] // LANE
    tile = lambda t: t if rep == 1 else jnp.concatenate([t] * rep, axis=0)
    up = jnp.concatenate([xt[8:], xt[:8]], axis=0)
    dn = jnp.concatenate([xt[-8:], xt[:-8]], axis=0)
    return xt * tile(c) + up * tile(s1) + dn * tile(s2)


def _tile_lanes(t, rep):
    return t if rep == 1 else jnp.concatenate([t] * rep, axis=1)


def _cmp_weights_rows(cmp_w_l, cmp_pe_l):
    eye2 = jnp.eye(2, dtype=F32)
    w = cmp_w_l.reshape(2, 2, CMP_STRIDE, HEAD_DIM, HEAD_DIM)
    ws = jnp.einsum('ktshe,gf->ksghtfe', w, eye2).reshape(2, CMP_STRIDE // 2, 4 * HEAD_DIM, 4 * HEAD_DIM)
    pec = jnp.einsum('kph,kphe->ke', cmp_pe_l, cmp_w_l.reshape(2, CMP_LEN, HEAD_DIM, HEAD_DIM),
                     precision=lax.Precision.HIGHEST)
    pec = jnp.tile(pec, (1, N_KV)).reshape(2, 1, D_KV)
    return ws.astype(BF16), pec


def _cover_t(n_cmp_pad, n_blk_pad, n_cmp, n_blk):
    c0 = np.arange(n_cmp_pad)[None, :] * CMP_STRIDE
    s0 = np.arange(n_blk_pad)[:, None] * SEL_LEN
    m = (c0 < s0 + SEL_LEN) & (c0 + CMP_LEN > s0)
    m &= (np.arange(n_cmp_pad)[None, :] < n_cmp) & (np.arange(n_blk_pad)[:, None] < n_blk)
    return m.astype(np.float32)


def _mm_kernel(x_ref, w_ref, o_ref):
    o_ref[...] = _dot(x_ref[...].astype(BF16), w_ref[...])


def _mm_cast_kernel(x_ref, w_ref, o_ref, xb_ref):
    @pl.when(pl.program_id(1) == 0)
    def _():
        xb_ref[...] = x_ref[...].astype(BF16)

    o_ref[...] = _dot(xb_ref[...], w_ref[...])


def _inproj(x_bf, w_bf, tm):
    m = x_bf.shape[0]
    tn = 768
    cast_once = x_bf.dtype != BF16
    return pl.pallas_call(
        _mm_cast_kernel if cast_once else _mm_kernel,
        grid=(m // tm, D_H // tn),
        in_specs=[pl.BlockSpec((tm, D_MODEL), lambda i, j: (i, 0)),
                  pl.BlockSpec((D_MODEL, tn), lambda i, j: (0, j))],
        out_specs=pl.BlockSpec((tm, tn), lambda i, j: (i, j)),
        out_shape=jax.ShapeDtypeStruct((m, D_H), F32),
        scratch_shapes=[pltpu.VMEM((tm, D_MODEL), BF16)] if cast_once else [],
        compiler_params=_cparams(("arbitrary", "arbitrary")),
        name="inproj",
    )(x_bf, w_bf)


def _prep_p_kernel(q_ref, kcv_ref, ksv_ref, kwv_ref, ng_ref, c_ref, s1_ref, s2_ref,
                   *rest):
    (kct_ref, vct_ref, kstt_ref, vstt_ref, kwtt_ref, vwtt_ref,
     qt_ref, ksa_ref, kwg_ref, vst_ref, vwt_ref, gt_ref) = rest[-12:]
    tt = q_ref.shape[0]
    c, s1, s2 = c_ref[...], s1_ref[...], s2_ref[...]
    qt_ref[0] = (_rope_t(q_ref[...].T, c, s1, s2) * (SCALE * LOG2E)).astype(BF16)
    kcv = kcv_ref[...]
    ksv = ksv_ref[...]
    kwv = kwv_ref[...]
    kcvt = kcv.T
    kct_ref[0, 0] = kcvt[:D_KV]
    vct_ref[0, 0] = kcvt[D_KV:]
    kst = _rope_t(ksv[:, :D_KV].T, c, s1, s2)
    vst = ksv[:, D_KV:].T
    kwt = _rope_t(kwv[:, :D_KV].T, c, s1, s2)
    vwt = kwv[:, D_KV:].T
    ks = kst.T
    kw = kwt.T
    kstt_ref[0, 0] = kst
    vstt_ref[0, 0] = vst
    kwtt_ref[0, 0] = kwt
    vwtt_ref[0, 0] = vwt
    for j in range(tt // KCH):
        vst_ref[0, j] = vst[:, j * KCH:(j + 1) * KCH].astype(BF16)
        vwt_ref[0, j] = vwt[:, j * KCH:(j + 1) * KCH].astype(BF16)
    row = lax.broadcasted_iota(jnp.int32, (tt, LANE), 0)
    lane = lax.broadcasted_iota(jnp.int32, (tt, LANE), 1)
    blk = jnp.right_shift(jnp.bitwise_and(row, KCH - 1), SEL_SHIFT)
    onehot = jnp.where((lane >= HEAD_DIM) & (lane - HEAD_DIM == blk), 1.0, 0.0)
    zpad = jnp.zeros((tt, LANE - HEAD_DIM), F32)
    for g in range(N_KV):
        kg = ks[:, g * HEAD_DIM:(g + 1) * HEAD_DIM]
        ksa_ref[0, g] = (jnp.concatenate([kg, zpad], axis=1) + onehot).astype(BF16)
        kwg_ref[0, g] = kw[:, g * HEAD_DIM:(g + 1) * HEAD_DIM].astype(BF16)
    gt_ref[0] = jax.nn.sigmoid(ng_ref[...]).T[:64, :]


def _prep_prompt(h, tabs, b, t, layer, depth, prev):
    tt = 512
    assert min(WINDOW, t) == tt
    nt = t // tt
    row = lambda bi, ti: bi * nt + ti
    c, s1, s2 = tabs
    tab_spec = pl.BlockSpec((LANE, tt), lambda bi, ti: (0, ti))
    in_specs = [pl.BlockSpec((tt, 1024), lambda bi, ti: (row(bi, ti), C_Q // 1024)),
                pl.BlockSpec((tt, 512), lambda bi, ti: (row(bi, ti), C_KV // 512)),
                pl.BlockSpec((tt, 512), lambda bi, ti: (row(bi, ti), C_KV // 512 + 1)),
                pl.BlockSpec((tt, 512), lambda bi, ti: (row(bi, ti), C_KV // 512 + 2)),
                pl.BlockSpec((tt, LANE), lambda bi, ti: (row(bi, ti), C_NG // LANE)),
                tab_spec, tab_spec, tab_spec]
    args = [h, h, h, h, h, c, s1, s2]
    aliases = {}
    if prev is not None:
        for j, a in enumerate(prev):
            aliases[len(args)] = j
            args.append(a)
            in_specs.append(pl.BlockSpec(memory_space=pl.ANY))
    full_t = pl.BlockSpec((1, 1, D_KV, tt), lambda bi, ti: (layer, bi, 0, ti))
    win_t = pl.BlockSpec((1, 1, D_KV, tt), lambda bi, ti: (layer, bi, 0, 0))
    st_full = jax.ShapeDtypeStruct((depth, b, D_KV, t), F32)
    st_win = jax.ShapeDtypeStruct((depth, b, D_KV, tt), F32)
    return pl.pallas_call(
        _prep_p_kernel,
        grid=(b, nt),
        in_specs=in_specs,
        out_specs=[full_t, full_t, full_t, full_t, win_t, win_t,
                   pl.BlockSpec((1, 1024, tt), lambda bi, ti: (bi, 0, ti)),
                   pl.BlockSpec((1, N_KV, tt, LANE), lambda bi, ti: (bi, 0, ti, 0)),
                   pl.BlockSpec((1, N_KV, tt, HEAD_DIM), lambda bi, ti: (bi, 0, ti, 0)),
                   pl.BlockSpec((1, tt // KCH, D_KV, KCH), lambda bi, ti: (bi, ti, 0, 0)),
                   pl.BlockSpec((1, tt // KCH, D_KV, KCH), lambda bi, ti: (bi, ti, 0, 0)),
                   pl.BlockSpec((1, 64, tt), lambda bi, ti: (bi, 0, ti))],
        out_shape=[st_full, st_full, st_full, st_full, st_win, st_win,
                   jax.ShapeDtypeStruct((b, 1024, t), BF16),
                   jax.ShapeDtypeStruct((b, N_KV, t, LANE), BF16),
                   jax.ShapeDtypeStruct((b, N_KV, t, HEAD_DIM), BF16),
                   jax.ShapeDtypeStruct((b, t // KCH, D_KV, KCH), BF16),
                   jax.ShapeDtypeStruct((b, t // KCH, D_KV, KCH), BF16),
                   jax.ShapeDtypeStruct((b, 64, t), F32)],
        input_output_aliases=aliases,
        compiler_params=_cparams(("arbitrary", "arbitrary")),
        name="prep_prompt",
    )(*args)


def _gather_rows(perm, page_t, x_ref, p, cpp):
    x = _dot_nt(perm, page_t.astype(BF16))
    for s_ in range(CMP_STRIDE):
        x_ref[s_, p * cpp:(p + 1) * cpp, :] = x[s_ * cpp:(s_ + 1) * cpp, :]


def _compress_rows(x_ref, ws_ref, pec_ref, kv, nch):
    halves = []
    for hf in range(2):
        ln = slice(hf * LANE, (hf + 1) * LANE)
        acc = jnp.zeros((nch, 2 * LANE), F32)
        for s2 in range(CMP_STRIDE // 2):
            x = jnp.concatenate([x_ref[2 * s2, :, ln], x_ref[2 * s2 + 1, :, ln]], axis=1).astype(BF16)
            acc = acc + _dot(x, ws_ref[kv, s2])
        bot_up = jnp.concatenate([acc[1:, LANE:], jnp.zeros((1, LANE), F32)], axis=0)
        halves.append(acc[:, :LANE] + bot_up)
    return jnp.concatenate(halves, axis=1) + pec_ref[kv]


def _row_perm(rows):
    cpp = rows // CMP_STRIDE
    i_ = np.arange(rows)
    return jnp.asarray(i_[None, :] == ((i_ % cpp) * CMP_STRIDE + i_ // cpp)[:, None], BF16)


def _compress_p_kernel(kt_ref, vt_ref, ws_ref, pec_ref, perm_ref, c_ref, s1_ref, s2_ref, kcg_ref, vct_ref,
                       kx_ref, vx_ref, *, rows):
    t = kt_ref.shape[-1]
    nch = t // CMP_STRIDE
    cpp = rows // CMP_STRIDE
    perm = perm_ref[...]
    for p in range(t // rows):
        _gather_rows(perm, kt_ref[0, 0, :, p * rows:(p + 1) * rows], kx_ref, p, cpp)
        _gather_rows(perm, vt_ref[0, 0, :, p * rows:(p + 1) * rows], vx_ref, p, cpp)
    kc = _rope(_compress_rows(kx_ref, ws_ref, pec_ref, 0, nch), c_ref[...], s1_ref[...], s2_ref[...])
    vc = _compress_rows(vx_ref, ws_ref, pec_ref, 1, nch)
    for g in range(N_KV):
        kcg_ref[0, g] = kc[:, g * HEAD_DIM:(g + 1) * HEAD_DIM].astype(BF16)
    vct_ref[0] = vc.T.astype(BF16)


def _compress_prompt(kct, vct_, ws, pec, ctabs, layer, b):
    t = kct.shape[-1]
    nch = t // CMP_STRIDE
    rows = LANE
    perm = _row_perm(rows)
    c, s1, s2 = ctabs
    full = lambda shape: pl.BlockSpec(shape, lambda bi: (0,) * len(shape))
    seq = pl.BlockSpec((1, 1, D_KV, t), lambda bi: (layer, bi, 0, 0))
    return pl.pallas_call(
        functools.partial(_compress_p_kernel, rows=rows),
        grid=(b,),
        in_specs=[seq, seq, full(ws.shape), full(pec.shape), full(perm.shape),
                  full((nch, D_KV)), full((nch, D_KV)), full((nch, D_KV))],
        out_specs=[pl.BlockSpec((1, N_KV, nch, HEAD_DIM), lambda bi: (bi, 0, 0, 0)),
                   pl.BlockSpec((1, D_KV, nch), lambda bi: (bi, 0, 0))],
        out_shape=[jax.ShapeDtypeStruct((b, N_KV, nch, HEAD_DIM), BF16),
                   jax.ShapeDtypeStruct((b, D_KV, nch), BF16)],
        scratch_shapes=[pltpu.VMEM((CMP_STRIDE, nch, D_KV), F32), pltpu.VMEM((CMP_STRIDE, nch, D_KV), F32)],
        compiler_params=_cparams(("arbitrary",)),
        name="compress_prompt",
    )(kct, vct_, ws, pec, perm, c, s1, s2)


def _select_blocks(score, n_blk, axis):
    idx = lax.broadcasted_iota(jnp.int32, score.shape, axis)
    rank = jnp.zeros(score.shape, F32)
    for j in range(n_blk):
        if axis == 0:
            sj = score[j:j + 1, :]
        else:
            sj = score[:, j:j + 1]
        beats = (sj > score) | ((sj == score) & (idx > j))
        rank = rank + jnp.where(beats, 1.0, 0.0)
    keep = (rank < float(N_SEL)) & (score > -jnp.inf) & (idx < n_blk)
    return jnp.where(keep, 1.0, 0.0)


def _select_rows(score, n_blk):
    nb = score.shape[0]
    idx = lax.broadcasted_iota(jnp.int32, score.shape, 0).astype(F32)
    cur = jnp.where(idx < float(n_blk), score, -jnp.inf)
    keep = jnp.zeros(score.shape, F32)
    for _ in range(N_SEL):
        mx = jnp.max(cur, axis=0, keepdims=True)
        first = jnp.min(jnp.where(cur == mx, idx, float(nb)), axis=0, keepdims=True)
        took = jnp.where(mx > -jnp.inf, 1.0, 0.0)
        keep = jnp.maximum(keep, jnp.where(idx == first, took, 0.0))
        cur = jnp.where(idx == first, -jnp.inf, cur)
    return keep


def _attn_p_kernel(qt_ref, kcg_ref, vct_ref, ksa_ref, vst_ref, kwg_ref, vwt_ref, gt_ref, cov_ref, cb_ref, tri_ref,
                   o_ref, selb_ref, m_ref, l_ref, acc_ref, comb_ref, sa_ref, sb_ref, wa_ref, *, n_blk):
    qi = pl.program_id(2)
    t0 = qi * TQ
    n = GROUP * TQ
    qt = qt_ref[0]
    qcat = jnp.concatenate([qt[r * HEAD_DIM:(r + 1) * HEAD_DIM, :] for r in range(GROUP)], axis=1)
    gt = gt_ref[0]

    def heads(x):
        return jnp.concatenate([x] * GROUP, axis=1)

    def gate(branch):
        return jnp.concatenate([gt[branch * GROUP + r:branch * GROUP + r + 1, :] for r in range(GROUP)], axis=1)

    def kchunk(k_ref, c):
        return k_ref[0, 0, pl.ds(pl.multiple_of(c * KCH, KCH), KCH), :]

    s = _dot(kcg_ref[0, 0], qcat) + heads(cb_ref[0])
    e = jnp.exp2(s - jnp.max(s, axis=0, keepdims=True))
    qpos = t0 + jnp.bitwise_and(lax.broadcasted_iota(jnp.int32, (1, n), 1), TQ - 1)
    inv = jnp.where(qpos >= CMP_LEN - 1, 1.0, 0.0) / jnp.sum(e, axis=0, keepdims=True)
    p = (e * inv).astype(BF16)
    comb_ref[...] = gate(0) * _dot(vct_ref[0], p)
    imp4 = _dot(cov_ref[...], p)
    imp = imp4[:, 0:TQ]
    for r in range(1, GROUP):
        imp = imp + imp4[:, r * TQ:(r + 1) * TQ]

    nb = imp.shape[0]
    per = KCH // SEL_LEN
    bidx = lax.broadcasted_iota(jnp.int32, (nb, TQ), 0)
    qp = t0 + lax.broadcasted_iota(jnp.int32, (nb, TQ), 1)
    cur = jnp.right_shift(qp, SEL_SHIFT)
    eligible = bidx * SEL_LEN <= qp
    forced = (bidx == 0) | (bidx == cur) | (bidx == cur - 1)
    score = jnp.where(forced, FORCE_SCORE, jnp.where(eligible, imp, -jnp.inf))
    selb = jnp.where(_select_rows(score, n_blk) > 0.5, 0.0, NEG)
    zrows = jnp.zeros((AUG - per, TQ), F32)
    for c in range(nb // per):
        selb_ref[c] = jnp.concatenate([selb[c * per:(c + 1) * per, :], zrows], axis=0).astype(BF16)

    def reset():
        m_ref[...] = jnp.full((1, n), NEG, F32)
        l_ref[...] = jnp.zeros((1, n), F32)
        acc_ref[...] = jnp.zeros((HEAD_DIM, n), F32)

    def scores(k_ref, c, aug):
        rhs = qcat
        if aug:
            rhs = jnp.concatenate([qcat, heads(selb_ref[c]), jnp.zeros((LANE - HEAD_DIM - AUG, n), BF16)], axis=0)
        return _dot(kchunk(k_ref, c), rhs)

    def update(s, vchunk):
        m_old = m_ref[...]
        m_new = jnp.maximum(m_old, jnp.max(s, axis=0, keepdims=True))
        alpha = jnp.exp2(m_old - m_new)
        e = jnp.exp2(s - m_new)
        l_ref[...] = alpha * l_ref[...] + jnp.sum(e, axis=0, keepdims=True)
        acc_ref[...] = alpha * acc_ref[...] + _dot(vchunk, e.astype(BF16))
        m_ref[...] = m_new

    def flush(branch):
        comb_ref[...] = comb_ref[...] + gate(branch) * (acc_ref[...] / jnp.maximum(l_ref[...], 1e-30))

    causal = lambda: heads(tri_ref[0])
    lowcut = lambda: heads(tri_ref[1])

    def wscores(c):
        return _dot(kchunk(kwg_ref, c), qcat)

    reset()
    sa_ref[...] = scores(ksa_ref, 0, True)

    def pair(i, carry):
        c = 2 * i
        sb_ref[...] = scores(ksa_ref, c + 1, True)
        update(sa_ref[...], vst_ref[0, c])
        sa_ref[...] = scores(ksa_ref, c + 2, True)
        update(sb_ref[...], vst_ref[0, c + 1])
        return carry

    lax.fori_loop(0, qi // 2, pair, 0)
    wfirst = jnp.maximum(qi - 1, 0)

    @pl.when(qi % 2 == 1)
    def _():
        sb_ref[...] = scores(ksa_ref, qi, True)
        update(sa_ref[...], vst_ref[0, qi - 1])
        wa_ref[...] = wscores(wfirst)
        update(sb_ref[...] + causal(), vst_ref[0, qi])

    @pl.when(qi % 2 == 0)
    def _():
        wa_ref[...] = wscores(wfirst)
        update(sa_ref[...] + causal(), vst_ref[0, qi])

    flush(1)

    reset()

    @pl.when(qi >= 1)
    def _():
        sb_ref[...] = wscores(qi)
        update(wa_ref[...] + lowcut(), vwt_ref[0, qi - 1])
        update(sb_ref[...] + causal(), vwt_ref[0, qi])

    @pl.when(qi == 0)
    def _():
        update(wa_ref[...] + causal(), vwt_ref[0, 0])

    flush(2)

    comb = comb_ref[...]
    o_ref[...] = jnp.concatenate([comb[:, r * TQ:(r + 1) * TQ].T for r in range(GROUP)], axis=1)


def _attn_prompt(qt, kcg, vct, ksa, vst, kwg, vwt, gt, cov, cb, tri, b, t, n_blk):
    nq = t // TQ
    nch = t // KCH
    ncp = kcg.shape[2]
    assert WINDOW == KCH and KCH == TQ and KCH // SEL_LEN <= AUG
    kern = functools.partial(_attn_p_kernel, n_blk=n_blk)
    n = GROUP * TQ
    return pl.pallas_call(
        kern,
        grid=(b, N_KV, nq),
        in_specs=[pl.BlockSpec((1, D_KV, TQ), lambda bi, g, qi: (bi, g, qi)),
                  pl.BlockSpec((1, 1, ncp, HEAD_DIM), lambda bi, g, qi: (bi, g, 0, 0)),
                  pl.BlockSpec((1, HEAD_DIM, ncp), lambda bi, g, qi: (bi, g, 0)),
                  pl.BlockSpec((1, 1, t, LANE), lambda bi, g, qi: (bi, g, 0, 0)),
                  pl.BlockSpec((1, nch, HEAD_DIM, KCH), lambda bi, g, qi: (bi, 0, g, 0)),
                  pl.BlockSpec((1, 1, t, HEAD_DIM), lambda bi, g, qi: (bi, g, 0, 0)),
                  pl.BlockSpec((1, nch, HEAD_DIM, KCH), lambda bi, g, qi: (bi, 0, g, 0)),
                  pl.BlockSpec((1, 16, TQ), lambda bi, g, qi: (bi, g, qi)),
                  pl.BlockSpec(cov.shape, lambda bi, g, qi: (0, 0)),
                  pl.BlockSpec((1, ncp, TQ), lambda bi, g, qi: (qi, 0, 0)),
                  pl.BlockSpec(tri.shape, lambda bi, g, qi: (0, 0, 0))],
        out_specs=pl.BlockSpec((TQ, D_KV), lambda bi, g, qi: (bi * nq + qi, g)),
        out_shape=jax.ShapeDtypeStruct((b * t, D_ATTN), F32),
        scratch_shapes=[pltpu.VMEM((cov.shape[0] * SEL_LEN // KCH, AUG, TQ), BF16),
                        pltpu.VMEM((1, n), F32), pltpu.VMEM((1, n), F32),
                        pltpu.VMEM((HEAD_DIM, n), F32), pltpu.VMEM((HEAD_DIM, n), F32),
                        pltpu.VMEM((KCH, n), F32), pltpu.VMEM((KCH, n), F32), pltpu.VMEM((KCH, n), F32)],
        compiler_params=_cparams(("arbitrary", "arbitrary", "arbitrary")),
        name="attn_prompt",
    )(qt, kcg, vct, ksa, vst, kwg, vwt, gt, cov, cb, tri)


def _attn_bias_tiles(t, ncp, n_cmp):
    k = np.arange(KCH)[:, None]
    q = np.arange(TQ)[None, :]
    tri = np.stack([np.where(k <= q, 0.0, NEG), np.where(k > q, 0.0, NEG), np.zeros((KCH, TQ))]).astype(np.float32)
    c = np.arange(ncp)[None, :, None]
    qpos = (np.arange(t // TQ)[:, None, None] * TQ) + np.arange(TQ)[None, None, :]
    cb = np.where((c * CMP_STRIDE + CMP_LEN - 1 <= qpos) & (c < n_cmp), 0.0, NEG).astype(np.float32)
    return jnp.asarray(cb), jnp.asarray(tri)


def _pool_group_mix(pooled, pw_ref, scale, pz):
    mixed = jnp.concatenate(
        [_dot(pooled[:, g * POOL_GROUP:(g + 1) * POOL_GROUP].astype(BF16), pw_ref[g])
         for g in range(len(POOL_WINDOWS))], axis=1)
    return (mixed * scale) * _silu(pz)


def _ab_p_kernel(pu_ref, pz_ref, cb_ref, cc_ref, cx_ref, cz_ref, pup_ref, ccp_ref, cxp_ref,
                 pw_ref, ps_ref, cw_ref, ya_ref, yb_ref, hct_ref, put_ref, *, tm, tiles_per_seq):
    i = pl.program_id(0)
    ti = i % tiles_per_seq
    valid = jnp.where(ti > 0, 1.0, 0.0)
    pu = pu_ref[...]
    ext = jnp.concatenate([pup_ref[...] * valid, pu], axis=0)
    pos = ti * tm + lax.broadcasted_iota(jnp.int32, (tm, POOL_GROUP), 0)
    acc = pu
    means = []
    k = 1
    for gi, w in enumerate(POOL_WINDOWS):
        ch = slice(gi * POOL_GROUP, (gi + 1) * POOL_GROUP)
        while k < w:
            acc = acc + ext[16 - k:16 - k + tm, :]
            k += 1
        count = jnp.minimum(w, pos + 1).astype(F32)
        means.append(acc[:, ch] / count)
    pooled = jnp.concatenate(means, axis=1) - pu
    ya_ref[...] = _pool_group_mix(pooled, pw_ref, ps_ref[...], pz_ref[...]).astype(BF16)

    hc = cc_ref[...] * cx_ref[...]
    hprev = (ccp_ref[...] * cxp_ref[...]) * valid
    hext = jnp.concatenate([hprev, hc], axis=0)
    cw = cw_ref[...]
    conv = hext[14:14 + tm, :] * cw[0:1, :]
    conv = conv + hext[15:15 + tm, :] * cw[1:2, :]
    conv = conv + hc * cw[2:3, :]
    yb_ref[...] = ((cb_ref[...] * conv) * _silu(cz_ref[...])).astype(BF16)
    hct_ref[0] = hc[tm - 8:tm, :]
    put_ref[0] = pu[tm - 16:tm, :]


def _ab_prompt(h, pw_bf, pscale, cw, b, t):
    tm = 512
    m = b * t
    tps = t // tm
    colb = lambda c: (lambda i: (i, c // 512))
    prev = lambda c: (lambda i: (jnp.maximum(i * (tm // 16) - 1, 0), c // 512))
    kern = functools.partial(_ab_p_kernel, tm=tm, tiles_per_seq=tps)
    full = lambda shape: pl.BlockSpec(shape, lambda i: (0,) * len(shape))
    return pl.pallas_call(
        kern,
        grid=(m // tm,),
        in_specs=[pl.BlockSpec((tm, 512), colb(C_PU)), pl.BlockSpec((tm, 512), colb(C_PZ)),
                  pl.BlockSpec((tm, 512), colb(C_CB)), pl.BlockSpec((tm, 512), colb(C_CC)),
                  pl.BlockSpec((tm, 512), colb(C_CX)), pl.BlockSpec((tm, 512), colb(C_CZ)),
                  pl.BlockSpec((16, 512), prev(C_PU)), pl.BlockSpec((16, 512), prev(C_CC)),
                  pl.BlockSpec((16, 512), prev(C_CX)),
                  full((len(POOL_WINDOWS), POOL_GROUP, POOL_GROUP)), full((1, D_POOL)), full((CONV_WIDTH, D_CONV))],
        out_specs=[pl.BlockSpec((tm, D_POOL), lambda i: (i, 0)),
                   pl.BlockSpec((tm, D_CONV), lambda i: (i, 0)),
                   pl.BlockSpec((1, 8, D_CONV), lambda i: (i // tps, 0, 0)),
                   pl.BlockSpec((1, 16, D_POOL), lambda i: (i // tps, 0, 0))],
        out_shape=[jax.ShapeDtypeStruct((m, D_POOL), BF16),
                   jax.ShapeDtypeStruct((m, D_CONV), BF16),
                   jax.ShapeDtypeStruct((b, 8, D_CONV), F32),
                   jax.ShapeDtypeStruct((b, 16, D_POOL), F32)],
        compiler_params=_cparams(("arbitrary",)),
        name="ab_prompt",
    )(h, h, h, h, h, h, h, h, h, pw_bf, pscale, cw)


def _tail_kernel(ya_ref, yb_ref, yc_ref, az_ref, ga_ref, gb_ref, gc_ref, x_ref,
                 pa_ref, pb_ref, pc_ref, wo_ref, lg_ref, lb_ref, o_ref, obf_ref):
    yc = (yc_ref[...] * _silu(az_ref[...])).astype(BF16)
    merged = (jax.nn.sigmoid(ga_ref[...]) * _dot(ya_ref[...], pa_ref[...])
              + jax.nn.sigmoid(gb_ref[...]) * _dot(yb_ref[...], pb_ref[...])
              + jax.nn.sigmoid(gc_ref[...]) * _dot(yc, pc_ref[...]))
    y = _dot(merged.astype(BF16), wo_ref[...])
    z = ALPHA * x_ref[...] + y
    mu = jnp.mean(z, axis=-1, keepdims=True)
    var = jnp.mean(jnp.square(z - mu), axis=-1, keepdims=True)
    out = (z - mu) * lax.rsqrt(var + LN_EPS) * lg_ref[...] + lb_ref[...]
    o_ref[...] = out
    obf_ref[...] = out.astype(BF16)


def _tail(ya, yb, yc, h, x, pa, pb, pc, wo, lg, lb, tm):
    m = x.shape[0]
    full = lambda shape: pl.BlockSpec(shape, lambda i: (0,) * len(shape))
    hcol = lambda c: pl.BlockSpec((tm, 1024), lambda i: (i, c // 1024))
    return pl.pallas_call(
        _tail_kernel,
        grid=(m // tm,),
        in_specs=[pl.BlockSpec((tm, D_POOL), lambda i: (i, 0)),
                  pl.BlockSpec((tm, D_CONV), lambda i: (i, 0)),
                  pl.BlockSpec((tm, D_ATTN), lambda i: (i, 0)),
                  hcol(C_AZ), hcol(C_MG), hcol(C_MG + 1024), hcol(C_MG + 2048),
                  pl.BlockSpec((tm, D_MODEL), lambda i: (i, 0)),
                  full((D_POOL, D_MODEL)), full((D_CONV, D_MODEL)), full((D_ATTN, D_MODEL)),
                  full((D_MODEL, D_MODEL)), full((1, D_MODEL)), full((1, D_MODEL))],
        out_specs=[pl.BlockSpec((tm, D_MODEL), lambda i: (i, 0)),
                   pl.BlockSpec((tm, D_MODEL), lambda i: (i, 0))],
        out_shape=[jax.ShapeDtypeStruct((m, D_MODEL), F32),
                   jax.ShapeDtypeStruct((m, D_MODEL), BF16)],
        compiler_params=_cparams(("arbitrary",)),
        name="tail",
    )(ya, yb, yc, h, h, h, h, x, pa, pb, pc, wo, lg, lb)


def _prep_s_kernel(q_ref, kcv_ref, ksv_ref, kwv_ref, ng_ref, c_ref, s1_ref, s2_ref,
                   kvst_ref, qr_ref, g_ref):
    c, s1, s2 = c_ref[...], s1_ref[...], s2_ref[...]
    qr_ref[...] = _rope(q_ref[...], _tile_lanes(c, 8), _tile_lanes(s1, 8), _tile_lanes(s2, 8)) * SCALE
    c2, s12, s22 = _tile_lanes(c, 2), _tile_lanes(s1, 2), _tile_lanes(s2, 2)
    ksv = ksv_ref[...]
    kwv = kwv_ref[...]
    kvst_ref[:, 0:512] = kcv_ref[...]
    kvst_ref[:, 512:768] = _rope(ksv[:, :D_KV], c2, s12, s22)
    kvst_ref[:, 768:1024] = ksv[:, D_KV:]
    kvst_ref[:, 1024:1280] = _rope(kwv[:, :D_KV], c2, s12, s22)
    kvst_ref[:, 1280:1536] = kwv[:, D_KV:]
    g_ref[...] = jax.nn.sigmoid(ng_ref[...])


def _prep_sample(h, tabs):
    n = h.shape[0]
    c, s1, s2 = tabs
    tab = pl.BlockSpec((1, LANE), lambda i: (0, 0))
    return pl.pallas_call(
        _prep_s_kernel,
        grid=(1,),
        in_specs=[pl.BlockSpec((n, 1024), lambda i: (0, C_Q // 1024)),
                  pl.BlockSpec((n, 512), lambda i: (0, C_KV // 512)),
                  pl.BlockSpec((n, 512), lambda i: (0, C_KV // 512 + 1)),
                  pl.BlockSpec((n, 512), lambda i: (0, C_KV // 512 + 2)),
                  pl.BlockSpec((n, LANE), lambda i: (0, C_NG // LANE)),
                  tab, tab, tab],
        out_specs=[pl.BlockSpec((n, 1536), lambda i: (0, 0)),
                   pl.BlockSpec((n, 1024), lambda i: (0, 0)),
                   pl.BlockSpec((n, LANE), lambda i: (0, 0))],
        out_shape=[jax.ShapeDtypeStruct((n, 1536), F32),
                   jax.ShapeDtypeStruct((n, 1024), F32),
                   jax.ShapeDtypeStruct((n, LANE), F32)],
        compiler_params=_cparams(("arbitrary",)),
        name="prep_sample",
    )(h, h, h, h, h, c, s1, s2)


def _ab_s_kernel(pu_ref, pz_ref, cb_ref, cc_ref, cx_ref, cz_ref, ph_ref, ch_ref,
                 pw_ref, ps_ref, cw_ref, ya_ref, yb_ref, hc_ref, *, pos0):
    pu = pu_ref[...]
    acc = pu
    means = []
    k = 1
    for gi, w in enumerate(POOL_WINDOWS):
        ch = slice(gi * POOL_GROUP, (gi + 1) * POOL_GROUP)
        while k < w:
            acc = acc + ph_ref[POOL_HIST - k]
            k += 1
        means.append(acc[:, ch] / float(min(w, pos0 + 1)))
    pooled = jnp.concatenate(means, axis=1) - pu
    ya_ref[...] = _pool_group_mix(pooled, pw_ref, ps_ref[...], pz_ref[...]).astype(BF16)
    hc = cc_ref[...] * cx_ref[...]
    cw = cw_ref[...]
    conv = ch_ref[0] * cw[0:1, :]
    conv = conv + ch_ref[1] * cw[1:2, :]
    conv = conv + hc * cw[2:3, :]
    yb_ref[...] = ((cb_ref[...] * conv) * _silu(cz_ref[...])).astype(BF16)
    hc_ref[...] = hc


def _ab_sample(h, pool_hist_t, conv_hist_t, pw_bf, pscale, cw, pos0):
    n = h.shape[0]
    colb = lambda c: pl.BlockSpec((n, 512), lambda i: (0, c // 512))
    full = lambda shape: pl.BlockSpec(shape, lambda i: (0,) * len(shape))
    kern = functools.partial(_ab_s_kernel, pos0=pos0)
    return pl.pallas_call(
        kern,
        grid=(1,),
        in_specs=[colb(C_PU), colb(C_PZ), colb(C_CB), colb(C_CC), colb(C_CX), colb(C_CZ),
                  full((POOL_HIST, n, D_POOL)), full((CONV_WIDTH - 1, n, D_CONV)),
                  full((len(POOL_WINDOWS), POOL_GROUP, POOL_GROUP)), full((1, D_POOL)), full((CONV_WIDTH, D_CONV))],
        out_specs=[full((n, D_POOL)), full((n, D_CONV)), full((n, D_CONV))],
        out_shape=[jax.ShapeDtypeStruct((n, D_POOL), BF16),
                   jax.ShapeDtypeStruct((n, D_CONV), BF16),
                   jax.ShapeDtypeStruct((n, D_CONV), F32)],
        compiler_params=_cparams(("arbitrary",)),
        name="ab_sample",
    )(h, h, h, h, h, h, pool_hist_t, conv_hist_t, pw_bf, pscale, cw)


def _attn_s_kernel(pt_ref, *refs, n_pages, past_len, n_cmp, n_blk, wb):
    del pt_ref
    np_ = n_pages
    kc_pages = refs[0:np_]
    vc_pages = refs[np_:2 * np_]
    ks_pages = refs[2 * np_:3 * np_]
    vs_pages = refs[3 * np_:4 * np_]
    (kwin_ref, vwin_ref, qz_ref, new_ref, newc_ref, g_ref, ws_ref, pec_ref, c_ref, s1_ref, s2_ref,
     cov_ref, exp_ref, hm_ref, perm_ref, *rest) = refs[4 * np_:]
    o_ref, kwo_ref, vwo_ref, kx_ref, vx_ref = rest[-5:]
    qpos = past_len
    rows = past_len // np_
    qz = qz_ref[0]
    qzb = qz.astype(BF16)
    new = new_ref[0]
    hm = hm_ref[...]
    nch = past_len // CMP_STRIDE

    def page_t(r):
        return r[0, 0].reshape(D_KV, rows)

    def scores(k_t, k_new):
        s = _dot(qzb, k_t)
        s_new = jnp.sum(qzb.astype(F32) * k_new.astype(BF16).astype(F32), axis=1, keepdims=True)
        return s, s_new

    def attend(sc, v_t, ok, v_new):
        s, s_new = sc
        m = jnp.maximum(jnp.max(jnp.where(ok, s, NEG), axis=1, keepdims=True), s_new)
        e = jnp.where(ok, jnp.exp(s - m), 0.0)
        e_new = jnp.exp(s_new - m)
        den = jnp.maximum(jnp.sum(e, axis=1, keepdims=True) + e_new, 1e-30)
        p = (e / den).astype(BF16)
        p_new = (e_new / den).astype(BF16).astype(F32)
        return _dot_nt(p, v_t) + p_new * v_new.astype(BF16).astype(F32)

    kw_t = kwin_ref[0, 0].reshape(D_KV, wb)
    vw_t = vwin_ref[0, 0].reshape(D_KV, wb)
    wpos = (past_len - wb) + lax.broadcasted_iota(jnp.int32, (N_HEADS, wb), 1)
    ok_w = (wpos <= qpos) & (qpos - wpos < WINDOW)
    o_win = attend(scores(kw_t.astype(BF16), new[2:3, :]), vw_t.astype(BF16), ok_w, new[3:4, :])
    ks_t = jnp.concatenate([page_t(r).astype(BF16) for r in ks_pages], axis=1)
    sc_sel = scores(ks_t, new[0:1, :])

    perm = perm_ref[...]
    cpp = rows // CMP_STRIDE
    for p in range(np_):
        _gather_rows(perm, page_t(kc_pages[p]), kx_ref, p, cpp)
        _gather_rows(perm, page_t(vc_pages[p]), vx_ref, p, cpp)
    kc = _rope(_compress_rows(kx_ref, ws_ref, pec_ref, 0, nch), c_ref[...], s1_ref[...], s2_ref[...])
    vc = _compress_rows(vx_ref, ws_ref, pec_ref, 1, nch)
    ncp = kc.shape[0]
    s_c = _dot_nt(qzb, kc.astype(BF16))
    cidx = lax.broadcasted_iota(jnp.int32, (N_HEADS, ncp), 1)
    vis = (cidx * CMP_STRIDE + (CMP_LEN - 1) <= qpos) & (cidx < n_cmp)
    m_c = jnp.max(jnp.where(vis, s_c, NEG), axis=1, keepdims=True)
    e_c = jnp.where(vis, jnp.exp(s_c - m_c), 0.0)
    p_c = e_c / jnp.maximum(jnp.sum(e_c, axis=1, keepdims=True), 1e-30)
    p_cb = p_c.astype(BF16)
    o_cmp = _dot(p_cb, vc.astype(BF16))
    imp_h = _dot(p_cb, cov_ref[...])
    imp = jnp.concatenate(
        [jnp.sum(imp_h[g * GROUP:(g + 1) * GROUP, :], axis=0, keepdims=True) for g in range(N_KV)], axis=0)

    nbp = imp.shape[1]
    bidx = lax.broadcasted_iota(jnp.int32, (N_KV, nbp), 1)
    cur = qpos // SEL_LEN
    eligible = bidx * SEL_LEN <= qpos
    forced = (bidx == 0) | (bidx == cur) | (bidx == cur - 1)
    score = jnp.where(forced, FORCE_SCORE, jnp.where(eligible, imp, -jnp.inf))
    sel = _select_blocks(score, n_blk, 1)
    sel_h = jnp.concatenate(
        [jnp.broadcast_to(sel[g:g + 1, :], (GROUP, nbp)) for g in range(N_KV)], axis=0)
    kmask = _dot(sel_h.astype(BF16), exp_ref[...])

    vs_t = jnp.concatenate([page_t(r).astype(BF16) for r in vs_pages], axis=1)
    kpos = lax.broadcasted_iota(jnp.int32, (N_HEADS, past_len), 1)
    ok_s = (kmask > 0.5) & (kpos <= qpos)
    o_sel = attend(sc_sel, vs_t, ok_s, new[1:2, :])

    g = g_ref[0]
    o = g[:, 0:1] * o_cmp + g[:, 1:2] * o_sel + g[:, 2:3] * o_win
    o = o * hm
    o_ref[0] = (o[:, 0:64] + o[:, 64:128]) + (o[:, 128:192] + o[:, 192:256])

    cols = newc_ref[0]
    lane = lax.broadcasted_iota(jnp.int32, (D_KV, wb), 1)
    kwo = jnp.where(lane == wb - 1, cols[:, 2:3], pltpu.roll(kw_t, wb - 1, 1))
    vwo = jnp.where(lane == wb - 1, cols[:, 3:4], pltpu.roll(vw_t, wb - 1, 1))
    kwo_ref[0, 0] = kwo.reshape(N_KV, HEAD_DIM, wb)
    vwo_ref[0, 0] = vwo.reshape(N_KV, HEAD_DIM, wb)


def _attn_sample(page_table, kc_t, vc_t, ks_t, vs_t, kwin_t, vwin_t, qz, new_rows, gates, ws, pec, ctabs,
                 cov, expand, headmask, layer, past_len, n_cmp, n_blk, win_prev):
    n, n_pages = page_table.shape
    depth = kwin_t.shape[0]
    wbuf = kwin_t.shape[-1]
    rows_pp = ks_t.shape[-1]
    c, s1, s2 = ctabs
    ncp = c.shape[0]
    perm = _row_perm(rows_pp)

    def page_spec(p):
        return pl.BlockSpec((1, 1, N_KV, HEAD_DIM, rows_pp), lambda i, pt: (layer, pt[i, p], 0, 0, 0))

    full = lambda shape: pl.BlockSpec(shape, lambda i, pt: (0,) * len(shape))
    win_spec = pl.BlockSpec((1, 1, N_KV, HEAD_DIM, wbuf), lambda i, pt: (layer, i, 0, 0, 0))
    in_specs = ([page_spec(p) for p in range(n_pages)] * 4
                + [win_spec, win_spec,
                   pl.BlockSpec((1, N_HEADS, D_KV), lambda i, pt: (i, 0, 0)),
                   pl.BlockSpec((1, 8, D_KV), lambda i, pt: (i, 0, 0)),
                   pl.BlockSpec((1, D_KV, 8), lambda i, pt: (i, 0, 0)),
                   pl.BlockSpec((1, N_HEADS, LANE), lambda i, pt: (i, 0, 0)),
                   full(ws.shape), full(pec.shape),
                   full((ncp, D_KV)), full((ncp, D_KV)), full((ncp, D_KV)),
                   full(cov.shape), full(expand.shape), full(headmask.shape), full(perm.shape)])
    args = ([kc_t] * n_pages + [vc_t] * n_pages + [ks_t] * n_pages + [vs_t] * n_pages
            + [kwin_t, vwin_t, qz, new_rows, jnp.transpose(new_rows, (0, 2, 1)), gates, ws, pec, c, s1, s2,
               cov, expand, headmask, perm])
    aliases = {}
    if win_prev is not None:
        for j, a in enumerate(win_prev):
            aliases[1 + len(args)] = 1 + j
            args.append(a)
            in_specs.append(pl.BlockSpec(memory_space=pl.ANY))
    kern = functools.partial(_attn_s_kernel, n_pages=n_pages, past_len=past_len, n_cmp=n_cmp,
                             n_blk=n_blk, wb=wbuf)
    wout = pl.BlockSpec((1, 1, N_KV, HEAD_DIM, wbuf), lambda i, pt: (layer, i, 0, 0, 0))
    grid_spec = pltpu.PrefetchScalarGridSpec(
        num_scalar_prefetch=1, grid=(n,), in_specs=in_specs,
        out_specs=[pl.BlockSpec((1, N_HEADS, HEAD_DIM), lambda i, pt: (i, 0, 0)), wout, wout],
        scratch_shapes=[pltpu.VMEM((CMP_STRIDE, past_len // CMP_STRIDE, D_KV), F32),
                        pltpu.VMEM((CMP_STRIDE, past_len // CMP_STRIDE, D_KV), F32)])
    wshape = jax.ShapeDtypeStruct((depth, n, N_KV, HEAD_DIM, wbuf), F32)
    return pl.pallas_call(
        kern,
        grid_spec=grid_spec,
        out_shape=[jax.ShapeDtypeStruct((n, N_HEADS, HEAD_DIM), F32), wshape, wshape],
        input_output_aliases=aliases,
        compiler_params=_cparams(("arbitrary",)),
        name="attn_sample",
    )(page_table, *args)


def kernel(x_prompt, x_sample, cache_k_cmp, cache_v_cmp, cache_k_sel, cache_v_sel, page_table, state_k_win, state_v_win, state_pool, state_conv, w_in, pool_w, pool_scale, conv_w, cmp_pe, cmp_w, proj_a, proj_b, proj_c, w_out, ln_g, ln_b):
    b, t, _ = x_prompt.shape
    ns = x_sample.shape[0]
    assert x_sample.shape[1] == 1
    depth = w_in.shape[0]
    page = cache_k_cmp.shape[2]
    n_pages = page_table.shape[1]
    past_len = n_pages * page
    wbuf = state_k_win.shape[2]
    assert t % 512 == 0 and page % CMP_STRIDE == 0 and wbuf == WINDOW and past_len >= WINDOW

    n_cmp_p = (t - CMP_LEN) // CMP_STRIDE + 1
    n_blk_p = -(-t // SEL_LEN)
    nch_p = t // CMP_STRIDE
    tabs_p = _rope_tables(np.arange(t), LANE, feature_major=True)
    ctabs_p = _rope_tables(np.arange(nch_p) * CMP_STRIDE + CMP_LEN - 1, D_KV)
    cov_p = jnp.asarray(_cover_t(nch_p, n_blk_p, n_cmp_p, n_blk_p), BF16)
    cb_p, tri_p = _attn_bias_tiles(t, nch_p, n_cmp_p)

    total_s = past_len + 1
    n_cmp_s = (total_s - CMP_LEN) // CMP_STRIDE + 1
    n_blk_s = -(-total_s // SEL_LEN)
    nch_s = past_len // CMP_STRIDE
    tabs_s = _rope_tables(np.asarray([past_len]), LANE)
    ctabs_s = _rope_tables(np.arange(nch_s) * CMP_STRIDE + CMP_LEN - 1, D_KV)
    cov_s = jnp.asarray(_cover_t(nch_s, LANE, n_cmp_s, n_blk_s).T, BF16)
    expand = jnp.asarray((np.arange(LANE)[:, None] == (np.arange(past_len)[None, :] // SEL_LEN)), BF16)
    headmask = jnp.asarray((np.arange(N_HEADS)[:, None] // GROUP) == (np.arange(D_KV)[None, :] // HEAD_DIM), F32)

    fm = lambda a: jnp.transpose(a, (0, 1, 3, 4, 2))
    kc_t, vc_t, ks_t, vs_t = fm(cache_k_cmp), fm(cache_v_cmp), fm(cache_k_sel), fm(cache_v_sel)
    kwin_t, vwin_t = fm(state_k_win), fm(state_v_win)

    xp = x_prompt.reshape(b * t, D_MODEL)
    xs = x_sample.reshape(ns, D_MODEL)
    xp_bf, xs_bf = xp, xs
    st_p, st_s = [], []
    win_s = None
    st6 = None
    for l in range(depth):
        w_bf = _perm_w_in(w_in[l])
        ws, pec = _cmp_weights_rows(cmp_w[l], cmp_pe[l])
        pw_bf = pool_w[l].astype(BF16)
        pscale = pool_scale[l].reshape(1, D_POOL)
        cw = conv_w[l]
        pa, pb, pc, wo = (a[l].astype(BF16) for a in (proj_a, proj_b, proj_c, w_out))
        lg = ln_g[l].reshape(1, D_MODEL)
        lb = ln_b[l].reshape(1, D_MODEL)

        h = _inproj(xp_bf, w_bf, 2048)
        *st6, qt, ksa, kwg, vst, vwt, gt = _prep_prompt(h, tabs_p, b, t, l, depth, st6)
        kcg, vct = _compress_prompt(st6[0], st6[1], ws, pec, ctabs_p, l, b)
        yc = _attn_prompt(qt, kcg, vct, ksa, vst, kwg, vwt, gt, cov_p, cb_p, tri_p, b, t, n_blk_p)
        ya, yb, hct, put = _ab_prompt(h, pw_bf, pscale, cw, b, t)
        xp_new, xp_bf = _tail(ya, yb, yc, h, xp, pa, pb, pc, wo, lg, lb, 512)
        st_p.append((put[:, 16 - POOL_HIST:],
                     hct[:, 8 - (CONV_WIDTH - 1):]))
        xp = xp_new

        hs = _inproj(xs_bf, w_bf, ns)
        kvs, qrot, gates = _prep_sample(hs, tabs_s)
        q3 = qrot.reshape(ns, N_HEADS, 1, HEAD_DIM)
        qz = (q3 * jnp.eye(N_KV, dtype=F32)[jnp.arange(N_HEADS) // GROUP][None, :, :, None]).reshape(ns, N_HEADS, D_KV)
        new_rows = jnp.concatenate(
            [kvs[:, 2 * D_KV:6 * D_KV].reshape(ns, 4, D_KV), jnp.zeros((ns, 4, D_KV), F32)], axis=1)
        g4 = gates[:, :64].reshape(ns, N_KV, 4, GROUP)[:, :, :3, :]
        g_h = jnp.transpose(g4, (0, 1, 3, 2)).reshape(ns, N_HEADS, 3)
        g_h = jnp.concatenate([g_h, jnp.zeros((ns, N_HEADS, LANE - 3), F32)], axis=2)
        ycs, kwo, vwo = _attn_sample(page_table, kc_t, vc_t, ks_t, vs_t, kwin_t, vwin_t, qz, new_rows, g_h, ws, pec,
                                     ctabs_s, cov_s, expand, headmask, l, past_len, n_cmp_s, n_blk_s, win_s)
        win_s = (kwo, vwo)
        ycs = ycs.reshape(ns, D_ATTN)
        pool_hist = state_pool[l]
        conv_hist = state_conv[l]
        yas, ybs, hcs = _ab_sample(hs, jnp.transpose(pool_hist, (1, 0, 2)), jnp.transpose(conv_hist, (1, 0, 2)),
                                   pw_bf, pscale, cw, past_len)
        xs_new, xs_bf = _tail(yas, ybs, ycs, hs, xs, pa, pb, pc, wo, lg, lb, ns)
        kvs5 = kvs.reshape(ns, 1, 6, N_KV, HEAD_DIM)
        st_s.append((kvs5[:, :, 0], kvs5[:, :, 1], kvs5[:, :, 2], kvs5[:, :, 3],
                     jnp.concatenate([pool_hist, hs[:, None, C_PU:C_PU + D_POOL]], axis=1)[:, 1:],
                     jnp.concatenate([conv_hist, hcs[:, None, :]], axis=1)[:, 1:]))
        xs = xs_new

    stack = lambda states, i: jnp.stack([s[i] for s in states], axis=0)
    rm = lambda a: jnp.transpose(a.reshape(a.shape[0], a.shape[1], N_KV, HEAD_DIM, a.shape[3]), (0, 1, 4, 2, 3))
    rm5 = lambda a: jnp.transpose(a, (0, 1, 4, 2, 3))
    return ((xp.reshape(b, t, D_MODEL), xs.reshape(ns, 1, D_MODEL))
            + tuple(rm(a) for a in st6) + (stack(st_p, 0), stack(st_p, 1))
            + tuple(stack(st_s, i) for i in range(4)) + (rm5(win_s[0]), rm5(win_s[1]))
            + (stack(st_s, 4), stack(st_s, 5)))
```
